```python
import math
import jax, jax.numpy as jnp
from jax import lax
import numpy as np


D_MODEL = 1024
BATCH = 8
SEQ = 4096
DEPTH = 2

RWKV_HEADS = 8
RWKV_HEAD_DIM = 64
RWKV_WIDTH = RWKV_HEADS * RWKV_HEAD_DIM
DECAY_LORA = 64
ICLR_LORA = 64
GN_EPS = 64e-5
ATT_HEADS = 8
ATT_HEAD_DIM = 64
ATT_WIDTH = ATT_HEADS * ATT_HEAD_DIM
KV_LATENT = 128
IDX_HEADS = 8
IDX_HEAD_DIM = 64
TOPK_MAX = 256
Q_BLOCK = 128
NUM_BUCKETS = 32
MAX_DISTANCE = 128
NORM_EPS = 1e-6
COL_SIZES = (RWKV_WIDTH, RWKV_WIDTH, RWKV_WIDTH, DECAY_LORA, ICLR_LORA, RWKV_WIDTH,
             ATT_WIDTH, KV_LATENT, ATT_WIDTH, IDX_HEADS * IDX_HEAD_DIM, IDX_HEAD_DIM, IDX_HEADS,
             D_MODEL, D_MODEL)
N_SHIFT = 3 * RWKV_WIDTH + DECAY_LORA + ICLR_LORA
N_IN = sum(COL_SIZES)

kernel_name = "hybrid_rwkv7_dsa_gated_block"


def rms_norm(x, g):
    xf = x.astype(jnp.float32)
    y = xf * lax.rsqrt(jnp.mean(xf * xf, axis=-1, keepdims=True) + NORM_EPS)
    return (y * g.astype(jnp.float32)).astype(x.dtype)


def token_shift(u):
    return jnp.pad(u, ((0, 0), (1, 0), (0, 0)))[:, :-1]


def t5_bucket(dist):
    max_exact = NUM_BUCKETS // 2
    is_small = dist < max_exact
    d = jnp.maximum(dist, 1).astype(jnp.float32)
    large = max_exact + (jnp.log(d / max_exact) / math.log(MAX_DISTANCE / max_exact)
                         * (NUM_BUCKETS - max_exact)).astype(jnp.int32)
    large = jnp.minimum(large, NUM_BUCKETS - 1)
    return jnp.where(is_small, dist, large)


def rwkv7_scan(r, w, k, v, a_vec, b_vec):
    B, S, H, N = r.shape

    def step(state, inp):
        r_t, w_t, k_t, v_t, a_t, b_t = inp
        sa = jnp.einsum('bhvk,bhk->bhv', state, a_t)
        state = (state * w_t[:, :, None, :] + sa[..., None] * b_t[:, :, None, :]
                 + v_t[..., None] * k_t[:, :, None, :])
        y_t = jnp.einsum('bhvk,bhk->bhv', state, r_t)
        return state, y_t

    xs = tuple(jnp.moveaxis(t, 1, 0) for t in (r, w, k, v, a_vec, b_vec))
    s0 = jnp.zeros((B, H, N, N), jnp.float32)
    _, y = lax.scan(step, s0, xs)
    return jnp.moveaxis(y, 0, 1)


def rwkv7_branch(pr, pk, pv, pwd, pad_, w0, w2, a0, a2, k_k, k_a, r_k, lnx_g, lnx_b):
    B, S, _ = pr.shape
    f32 = jnp.float32
    heads = lambda t: t.astype(f32).reshape(B, S, RWKV_HEADS, RWKV_HEAD_DIM)
    w_log = -jax.nn.softplus(-(w0 + jnp.tanh(pwd) @ w2).astype(f32)) - 0.5
    decay = jnp.exp(-jnp.exp(w_log))
    a = jax.nn.sigmoid((a0 + pad_ @ a2).astype(f32))
    kk = heads(pk * k_k)
    kk = kk / jnp.maximum(jnp.sqrt(jnp.sum(kk * kk, axis=-1, keepdims=True)), 1e-12)
    k = pk.astype(f32) * (1.0 + (a - 1.0) * k_a)
    r_h, k_h, v_h, a_h, w_h = heads(pr), heads(k), heads(pv), heads(a), heads(decay)
    y = rwkv7_scan(r_h, w_h, k_h, v_h, -kk, kk * a_h)
    mu = jnp.mean(y, axis=-1, keepdims=True)
    var = jnp.mean(jnp.square(y - mu), axis=-1, keepdims=True)
    yn = ((y - mu) * lax.rsqrt(var + GN_EPS)).reshape(B, S, RWKV_WIDTH) * lnx_g + lnx_b
    bonus = jnp.sum(r_h * k_h * r_k.astype(f32), axis=-1, keepdims=True) * v_h
    out = yn + bonus.reshape(B, S, RWKV_WIDTH)
    return out.astype(pr.dtype)


def dsa_branch(q, ckv, qi, ki, wi, kv_norm_g, w_uk, w_uv, rel_bias):
    B, S, _ = q.shape
    topk = min(TOPK_MAX, S // 4)
    nb = S // Q_BLOCK
    q = q.reshape(B, S, ATT_HEADS, ATT_HEAD_DIM)
    lat = rms_norm(ckv, kv_norm_g)
    q_abs = jnp.einsum('bshd,chd->bshc', q, w_uk) * (ATT_HEAD_DIM ** -0.5)
    qi = qi.reshape(B, S, IDX_HEADS, IDX_HEAD_DIM)
    key_pos = jnp.arange(S, dtype=jnp.int32)
    qpos = key_pos.reshape(nb, Q_BLOCK)
    b_idx = jnp.arange(B)[:, None, None]

    def blocks(t):
        return jnp.moveaxis(t.reshape((B, nb, Q_BLOCK) + t.shape[2:]), 1, 0)

    def one_block(args):
        qa_b, qi_b, wi_b, pos_b = args
        s_idx = jax.nn.relu(jnp.einsum('bqhd,bsd->bqhs', qi_b, ki))
        score = jnp.einsum('bqh,bqhs->bqs', wi_b, s_idx).astype(jnp.float32)
        causal = key_pos[None, :] <= pos_b[:, None]
        score = jnp.where(causal[None], score, -jnp.inf)
        _, sel = lax.top_k(score, topk)
        lat_sel = lat[b_idx, sel]
        dist = pos_b[None, :, None] - sel
        valid = dist >= 0
        bias = rel_bias[t5_bucket(jnp.maximum(dist, 0))]
        logits = (jnp.einsum('bqhc,bqkc->bqhk', qa_b, lat_sel).astype(jnp.float32)
                  + jnp.moveaxis(bias, -1, 2).astype(jnp.float32))
        logits = jnp.where(valid[:, :, None, :], logits, -jnp.inf)
        p = jax.nn.softmax(logits, axis=-1)
        return jnp.einsum('bqhk,bqkc->bqhc', p.astype(lat_sel.dtype), lat_sel)

    o_lat = lax.map(one_block, (blocks(q_abs), blocks(qi), blocks(wi), qpos))
    o_lat = jnp.moveaxis(o_lat, 0, 1).reshape(B, S, ATT_HEADS, KV_LATENT)
    o = jnp.einsum('bshc,chd->bshd', o_lat, w_uv)
    return o.reshape(B, S, ATT_WIDTH)


def setup_inputs(seed: int = 0) -> dict:
    key = jax.random.key(seed)
    ks = jax.random.split(key, 24)
    f32 = jnp.float32

    def nrm(k, shape, s):
        return jax.random.normal(k, shape, f32) * s

    return {
        "x": nrm(ks[0], (BATCH, SEQ, D_MODEL), 1.0),
        "c": nrm(ks[1], (BATCH, D_MODEL), 1.0),
        "ada_w": nrm(ks[2], (DEPTH, D_MODEL, 3 * D_MODEL), 0.5 * D_MODEL ** -0.5),
        "ada_b": nrm(ks[3], (DEPTH, 3 * D_MODEL), 0.02),
        "norm_g": 1.0 + nrm(ks[4], (DEPTH, D_MODEL), 0.05),
        "w_in": nrm(ks[5], (DEPTH, D_MODEL, N_IN), D_MODEL ** -0.5),
        "shift_mu": jax.random.uniform(ks[6], (DEPTH, N_SHIFT), f32),
        "w0": jax.random.uniform(ks[7], (DEPTH, RWKV_WIDTH), f32, -6.0, -1.0),
        "w2": nrm(ks[8], (DEPTH, DECAY_LORA, RWKV_WIDTH), 0.1),
        "a0": nrm(ks[9], (DEPTH, RWKV_WIDTH), 0.5),
        "a2": nrm(ks[10], (DEPTH, ICLR_LORA, RWKV_WIDTH), 0.1),
        "k_k": 0.85 + nrm(ks[11], (DEPTH, RWKV_WIDTH), 0.05),
        "k_a": 1.0 + nrm(ks[12], (DEPTH, RWKV_WIDTH), 0.05),
        "r_k": nrm(ks[13], (DEPTH, RWKV_HEADS, RWKV_HEAD_DIM), 0.1),
        "lnx_g": 1.0 + nrm(ks[14], (DEPTH, RWKV_WIDTH), 0.05),
        "lnx_b": nrm(ks[15], (DEPTH, RWKV_WIDTH), 0.02),
        "kv_norm_g": 1.0 + nrm(ks[16], (DEPTH, KV_LATENT), 0.05),
        "w_uk": nrm(ks[17], (DEPTH, KV_LATENT, ATT_HEADS, ATT_HEAD_DIM), KV_LATENT ** -0.5),
        "w_uv": nrm(ks[18], (DEPTH, KV_LATENT, ATT_HEADS, ATT_HEAD_DIM), KV_LATENT ** -0.5),
        "w_pa": nrm(ks[19], (DEPTH, RWKV_WIDTH, D_MODEL), RWKV_WIDTH ** -0.5),
        "w_pb": nrm(ks[20], (DEPTH, ATT_WIDTH, D_MODEL), ATT_WIDTH ** -0.5),
        "w_o": nrm(ks[21], (DEPTH, D_MODEL, D_MODEL), D_MODEL ** -0.5),
        "rel_bias": nrm(ks[22], (NUM_BUCKETS, ATT_HEADS), 0.5),
        "final_g": 1.0 + nrm(ks[23], (D_MODEL,), 0.05),
    }


def reference(x, c, ada_w, ada_b, norm_g, w_in, shift_mu, w0, w2, a0, a2, k_k, k_a, r_k,
              lnx_g, lnx_b, kv_norm_g, w_uk, w_uv, w_pa, w_pb, w_o, rel_bias, final_g):
    split_at = np.cumsum(np.array(COL_SIZES))[:-1].tolist()
    c_act = jax.nn.silu(c)
    for l in range(DEPTH):
        mod = c_act @ ada_w[l] + ada_b[l]
        shift, scale, gate = jnp.split(mod, 3, axis=-1)
        h = rms_norm(x, norm_g[l]) * (1.0 + scale[:, None, :]) + shift[:, None, :]
        p = h @ w_in[l]
        ps = p[..., :N_SHIFT]
        ps = ps + (token_shift(ps) - ps) * shift_mu[l]
        p = jnp.concatenate([ps, p[..., N_SHIFT:]], axis=-1)
        (pr, pk, pv, pwd, pad_, z_a, q, ckv, z_b, qi, ki, wi, g_a, g_b) = jnp.split(p, split_at, axis=-1)
        y_a = rwkv7_branch(pr, pk, pv, pwd, pad_, w0[l], w2[l], a0[l], a2[l], k_k[l], k_a[l],
                           r_k[l], lnx_g[l], lnx_b[l])
        y_b = dsa_branch(q, ckv, qi, ki, wi, kv_norm_g[l], w_uk[l], w_uv[l], rel_bias)
        br_a = (y_a * jax.nn.silu(z_a)) @ w_pa[l]
        br_b = (y_b * jax.nn.silu(z_b)) @ w_pb[l]
        merged = jax.nn.sigmoid(g_a) * br_a + jax.nn.sigmoid(g_b) * br_b
        x = x + gate[:, None, :] * (merged @ w_o[l])
    return rms_norm(x, final_g)
```

```python
import functools
import math

import jax
import jax.numpy as jnp
from jax import lax
from jax.experimental import pallas as pl
from jax.experimental.pallas import tpu as pltpu

F32 = jnp.float32
BF16 = jnp.bfloat16
I32 = jnp.int32

DECAY_LORA = 64
ICLR_LORA = 64
GN_EPS = 64e-5
IDX_HEADS = 8
IDX_HEAD_DIM = 64
TOPK_MAX = 256
Q_BLOCK = 128
MAX_DISTANCE = 128
NORM_EPS = 1e-6

LANES = 128
VMEM_LIMIT_BYTES = 56 * 1024 * 1024

HEAD_DIM = 64
CHUNK = 64
INT_MIN = -(2 ** 31)
INT_MAX = 2 ** 31 - 1
NEG_MASK = -2e30
NEG_INIT = -1e30


def _sigmoid(x):
    return 1.0 / (1.0 + jnp.exp(-x))


def _dot(a, b, dims=(((1,), (0,)), ((), ()))):
    return lax.dot_general(a, b, dims, preferred_element_type=F32)


_NN = (((1,), (0,)), ((), ()))
_NT = (((1,), (1,)), ((), ()))
_TN = (((0,), (0,)), ((), ()))


def _split(x, n):
    parts = []
    rem = x
    for i in range(n):
        p = rem.astype(BF16)
        parts.append(p)
        if i + 1 < n:
            rem = rem - p.astype(F32)
    return parts


def _mm(a, b, passes, dims=_NN):
    if passes == 1:
        return _dot(a.astype(BF16), b.astype(BF16), dims)
    if passes == 3:
        a1, a2 = _split(a, 2)
        b1, b2 = _split(b, 2)
        return _dot(a1, b1, dims) + (_dot(a1, b2, dims) + _dot(a2, b1, dims))
    a1, a2, a3 = _split(a, 3)
    b1, b2, b3 = _split(b, 3)
    hi = _dot(a1, b1, dims)
    mid = _dot(a1, b2, dims) + _dot(a2, b1, dims)
    lo = _dot(a2, b2, dims) + (_dot(a1, b3, dims) + _dot(a3, b1, dims))
    return hi + (mid + lo)


def _mm_exact_rhs(a, b_bf16, n):
    out = None
    for p in _split(a, n):
        t = _dot(p, b_bf16)
        out = t if out is None else out + t
    return out


def _mod_kernel(c_ref, w_ref, b_ref, o_ref):
    c = c_ref[...]
    ca = c * _sigmoid(c)
    o_ref[0] = _mm(ca, w_ref[0], 6) + b_ref[0]


def _mod_call(c, ada_w, ada_b):
    depth, d, d3 = ada_w.shape
    bsz = c.shape[0]
    tn = d
    return pl.pallas_call(
        _mod_kernel,
        grid=(depth, d3 // tn),
        in_specs=[
            pl.BlockSpec((bsz, d), lambda l, n: (0, 0)),
            pl.BlockSpec((1, d, tn), lambda l, n: (l, 0, n)),
            pl.BlockSpec((1, 1, tn), lambda l, n: (l, 0, n)),
        ],
        out_specs=pl.BlockSpec((1, bsz, tn), lambda l, n: (l, 0, n)),
        out_shape=jax.ShapeDtypeStruct((depth, bsz, d3), F32),
        compiler_params=pltpu.CompilerParams(
            dimension_semantics=("arbitrary", "arbitrary"),
            vmem_limit_bytes=VMEM_LIMIT_BYTES),
        name="adaln_mod",
    )(c, ada_w, ada_b.reshape(depth, 1, d3))


def _inproj_kernel(x_ref, mod_ref, g_ref, w_ref, mu_ref, kvg_ref,
                   sh_ref, za_ref, q_ref, lat_ref, zb_ref, qi_ref, ki_ref, wi_ref, ga_ref, gb_ref,
                   carry_ref, *, d, n_shift, widths):
    j = pl.program_id(1)
    xb = x_ref[0]
    tm = xb.shape[0]
    shift = mod_ref[0, :, 0:d]
    scale = mod_ref[0, :, d:2 * d]
    ms = jnp.mean(xb * xb, axis=-1, keepdims=True)
    h = xb * lax.rsqrt(ms + NORM_EPS) * g_ref[...]
    h = h * (1.0 + scale) + shift
    hb = h.astype(BF16)

    @pl.when(j == 0)
    def _():
        carry_ref[...] = jnp.zeros_like(carry_ref)

    ps = _dot(hb, w_ref[:, 0:n_shift])
    prev = pltpu.roll(ps, 1, 0)
    row = lax.broadcasted_iota(I32, (tm, 1), 0)
    prev = jnp.where(row == 0, carry_ref[...], prev)
    carry_ref[...] = ps[tm - 1:tm, :]
    sh_ref[0] = ps + (prev - ps) * mu_ref[...]

    off = n_shift
    outs = (za_ref, q_ref, lat_ref, zb_ref, qi_ref, ki_ref, wi_ref, ga_ref, gb_ref)
    for o_ref, wd in zip(outs, widths):
        p = _dot(hb, w_ref[:, off:off + wd])
        if o_ref is lat_ref:
            p = p * lax.rsqrt(jnp.mean(p * p, axis=-1, keepdims=True) + NORM_EPS) * kvg_ref[...]
        o_ref[0] = p
        off += wd


def _inproj_call(x, mod, norm_g, w_packed, mu, kvg, *, n_shift, widths, tm):
    bsz, s, d = x.shape
    npk = w_packed.shape[1]
    d3 = mod.shape[-1]
    row_spec = lambda w: pl.BlockSpec((1, tm, w), lambda b, j: (b, j, 0))
    const = lambda shape: pl.BlockSpec(shape, lambda b, j: tuple(0 for _ in shape))
    out_widths = (n_shift,) + tuple(widths)
    return pl.pallas_call(
        functools.partial(_inproj_kernel, d=d, n_shift=n_shift, widths=tuple(widths)),
        grid=(bsz, s // tm),
        in_specs=[
            row_spec(d),
            pl.BlockSpec((1, 1, d3), lambda b, j: (b, 0, 0)),
            const((1, d)),
            pl.BlockSpec((d, npk), lambda b, j: (0, 0), pipeline_mode=pl.Buffered(1)),
            const((1, n_shift)),
            const((1, widths[2])),
        ],
        out_specs=[row_spec(w) for w in out_widths],
        out_shape=[jax.ShapeDtypeStruct((bsz, s, w), F32) for w in out_widths],
        scratch_shapes=[pltpu.VMEM((1, n_shift), F32)],
        compiler_params=pltpu.CompilerParams(
            dimension_semantics=("arbitrary", "arbitrary"),
            vmem_limit_bytes=VMEM_LIMIT_BYTES),
        name="inproj",
    )(x, mod.reshape(bsz, 1, d3), norm_g.reshape(1, d), w_packed, mu.reshape(1, n_shift),
      kvg.reshape(1, -1))


def _rwkv_kernel(sh_ref, vec_ref, w2_ref, a2_ref, ones_ref, out_ref, st_ref,
                 *, width, p_chain, p_inv, p_state):
    j = pl.program_id(1)
    L = CHUNK
    n_pairs = width // LANES

    @pl.when(j == 0)
    def _():
        st_ref[...] = jnp.zeros_like(st_ref)

    blk = sh_ref[0]
    pr = blk[:, 0:width]
    pk = blk[:, width:2 * width]
    pv = blk[:, 2 * width:3 * width]
    lora = blk[:, 3 * width:3 * width + LANES]
    lane = lax.broadcasted_iota(I32, (1, LANES), 1)
    lo_half = lane < HEAD_DIM
    t_in = jnp.where(lo_half, jnp.tanh(lora), lora)
    dec_in = _mm(t_in, w2_ref[...], 3)
    a_in = _mm(t_in, a2_ref[...], 3)

    w0 = vec_ref[0:1, :]
    a0 = vec_ref[1:2, :]
    k_k = vec_ref[2:3, :]
    k_a = vec_ref[3:4, :]
    r_k = vec_ref[4:5, :]
    lnx_g = vec_ref[5:6, :]
    lnx_b = vec_ref[6:7, :]
    ones_bd = ones_ref[...]

    z = -(w0 + dec_in)
    sp = jnp.maximum(z, 0.0) + jnp.log1p(jnp.exp(-jnp.abs(z)))
    logw = -jnp.exp(-sp - 0.5)
    a = _sigmoid(a0 + a_in)
    kk0 = pk * k_k
    ss = _mm_exact_rhs(kk0 * kk0, ones_bd, 3)
    kk = kk0 / jnp.maximum(jnp.sqrt(ss), 1e-12)
    k_eff = pk * (1.0 + (a - 1.0) * k_a)
    a_vec = -kk
    b_vec = kk * a

    r_i = lax.broadcasted_iota(I32, (L, L), 0)
    c_i = lax.broadcasted_iota(I32, (L, L), 1)
    tril = jnp.where(c_i <= r_i, 1.0, 0.0).astype(BF16)
    cl = None
    for part in _split(logw, 3):
        t = _dot(tril, part)
        cl = t if cl is None else cl + t
    cl_last = cl[L - 1:L, :]
    rt = pr * jnp.exp(cl)
    at = a_vec * jnp.exp(cl - logw)
    e_inv = jnp.exp(-cl)
    bt = b_vec * e_inv
    kt = k_eff * e_inv
    e_l = jnp.exp(cl_last - cl)
    bl = b_vec * e_l
    kl = k_eff * e_l
    p_last = jnp.exp(cl_last)

    rr = lax.broadcasted_iota(I32, (LANES, LANES), 0)
    cc = lax.broadcasted_iota(I32, (LANES, LANES), 1)
    same = (rr // L) == (cc // L)
    strict = same & ((cc % L) < (rr % L))
    incl = same & ((cc % L) <= (rr % L))
    eye = rr == cc
    eye_f = jnp.where(eye, 1.0, 0.0)

    def stack(x):
        return jnp.concatenate([jnp.where(lo_half, x, 0.0), jnp.where(lo_half, 0.0, x)], axis=0)

    ys = []
    for p in range(n_pairs):
        sl = slice(p * LANES, (p + 1) * LANES)
        atm, rtm, btm, ktm = stack(at[:, sl]), stack(rt[:, sl]), stack(bt[:, sl]), stack(kt[:, sl])
        vm, blm, klm = stack(pv[:, sl]), stack(bl[:, sl]), stack(kl[:, sl])
        g = _mm(jnp.concatenate([atm, rtm], axis=0), jnp.concatenate([btm, ktm], axis=0),
                p_chain, _NT)
        a_ab = jnp.where(strict, g[0:LANES, 0:LANES], 0.0)
        a_ak = jnp.where(strict, g[0:LANES, LANES:], 0.0)
        a_rb = jnp.where(incl, g[LANES:, 0:LANES], 0.0)
        a_rk = jnp.where(incl, g[LANES:, LANES:], 0.0)
        tinv = eye_f + a_ab
        qn = a_ab
        sq = 1
        while 2 * sq < L:
            qn = _mm(qn, qn, p_inv)
            tinv = tinv + _mm(tinv, qn, p_inv)
            sq *= 2
        akv = _mm(a_ak, vm, p_chain)
        wz = _mm(tinv, jnp.concatenate([atm, akv], axis=1), p_inv)
        wm = wz[:, 0:LANES]
        zm = wz[:, LANES:]
        zv = jnp.concatenate([zm, vm], axis=0)
        rqm = rtm + _mm(a_rb, wm, p_chain)
        y0m = _mm(jnp.concatenate([a_rb, a_rk], axis=1), zv, p_chain)
        st = st_ref[p]
        ym = _mm(rqm, st, p_state) + y0m
        ys.append(ym[0:L] + ym[L:])
        mt = jnp.where(eye, p_last[:, sl], 0.0) + _mm(blm, wm, p_state, _TN)
        ct = _mm(jnp.concatenate([blm, klm], axis=0), zv, p_state, _TN)
        st_ref[p] = _mm(mt, st, p_state) + ct
    y = jnp.concatenate(ys, axis=1)

    inv_n = 1.0 / HEAD_DIM
    mu = _mm_exact_rhs(y, ones_bd, 3) * inv_n
    dlt = y - mu
    var = _mm_exact_rhs(dlt * dlt, ones_bd, 3) * inv_n
    yn = dlt * lax.rsqrt(var + GN_EPS) * lnx_g + lnx_b
    bonus = _mm_exact_rhs(pr * k_eff * r_k, ones_bd, 3) * pv
    out_ref[0] = yn + bonus


def _rwkv_call(sh, vecs, w2p, a2p, ones_bd, *, width, p_chain=3, p_inv=3, p_state=3):
    bsz, s, n_shift = sh.shape
    n_pairs = width // LANES
    const = lambda shape: pl.BlockSpec(shape, lambda b, j: tuple(0 for _ in shape))
    return pl.pallas_call(
        functools.partial(_rwkv_kernel, width=width, p_chain=p_chain, p_inv=p_inv, p_state=p_state),
        grid=(bsz, s // CHUNK),
        in_specs=[
            pl.BlockSpec((1, CHUNK, n_shift), lambda b, j: (b, j, 0)),
            const(vecs.shape),
            const(w2p.shape),
            const(a2p.shape),
            const(ones_bd.shape),
        ],
        out_specs=pl.BlockSpec((1, CHUNK, width), lambda b, j: (b, j, 0)),
        out_shape=jax.ShapeDtypeStruct((bsz, s, width), F32),
        scratch_shapes=[pltpu.VMEM((n_pairs, LANES, LANES), F32)],
        compiler_params=pltpu.CompilerParams(
            dimension_semantics=("arbitrary", "arbitrary"),
            vmem_limit_bytes=VMEM_LIMIT_BYTES),
        name="rwkv7_scan",
    )(sh, vecs, w2p, a2p, ones_bd)


def _dsa_kernel(q_ref, qi_ref, wi_ref, ki_ref, lat_ref, wuk_ref, wuv_ref, bias_ref, far_ref, out_ref,
                key_ref, acc_ref, m_ref, l_ref, mt_ref,
                *, topk, kc, n_heads, seq, scale):
    qb = pl.program_id(1)
    QB = Q_BLOCK
    NEAR = 2 * QB
    q_end = (qb + 1) * QB
    n_idx = (jnp.maximum(q_end, NEAR) + kc - 1) // kc
    lane = lax.broadcasted_iota(I32, (1, LANES), 1)
    lo_half = lane < HEAD_DIM
    tpos = qb * QB + lax.broadcasted_iota(I32, (QB, 1), 0)

    qi = qi_ref[0]
    wi = wi_ref[0]
    parts = []
    for h in range(n_heads):
        pair = qi[:, (h // 2) * LANES:(h // 2 + 1) * LANES]
        parts.append(jnp.where(lo_half, pair, 0.0) if h % 2 == 0 else jnp.where(lo_half, 0.0, pair))
    qis_hi, qis_lo = _split(jnp.concatenate(parts, axis=0), 2)
    wis = [wi[:, h:h + 1] for h in range(n_heads)]

    def idx_body(c, carry):
        ks = pl.multiple_of(c * kc, kc)
        ki_hi, ki_lo = _split(ki_ref[0, pl.ds(ks, kc), :], 2)
        s_all = _dot(qis_hi, ki_hi, _NT) + (_dot(qis_hi, ki_lo, _NT) + _dot(qis_lo, ki_hi, _NT))
        s_all = jnp.maximum(s_all, 0.0)
        score = wis[0] * s_all[0:QB]
        for h in range(1, n_heads):
            score = score + wis[h] * s_all[h * QB:(h + 1) * QB]
        score = score + 0.0
        bits = pltpu.bitcast(score, I32)
        key = bits ^ ((bits >> 31) & INT_MAX)
        spos = ks + lax.broadcasted_iota(I32, (1, kc), 1)
        key_ref[:, pl.ds(ks, kc)] = jnp.where(spos <= tpos, key, INT_MIN)
        return carry

    lax.fori_loop(0, n_idx, idx_body, 0)

    def count(pred):
        def body(c, acc):
            ks = pl.multiple_of(c * kc, kc)
            kch = key_ref[:, pl.ds(ks, kc)]
            col = ks + lax.broadcasted_iota(I32, (1, kc), 1)
            hit = jnp.where(pred(kch, col), 1.0, 0.0)
            for t in range(kc // LANES):
                acc = acc + hit[:, t * LANES:(t + 1) * LANES]
            return acc
        acc = lax.fori_loop(0, n_idx, body, jnp.zeros((QB, LANES), F32))
        return jnp.sum(acc, axis=1, keepdims=True)

    kf = float(topk)

    def bit_body(i, thr_u):
        cand_u = thr_u | lax.shift_left(jnp.int32(1), 31 - i)
        cand = cand_u ^ INT_MIN
        cnt = count(lambda kch, col: kch >= cand)
        return jnp.where(cnt >= kf, cand_u, thr_u)

    thr = lax.fori_loop(0, 32, bit_body, jnp.zeros((QB, 1), I32)) ^ INT_MIN

    cnt_gt = count(lambda kch, col: kch > thr)
    cnt_ge = count(lambda kch, col: kch >= thr)
    need = kf - cnt_gt
    excess = jnp.where((cnt_ge > kf) & (thr > INT_MIN), 1.0, 0.0)
    mt_ref[...] = jnp.full((QB, 1), INT_MAX, I32)

    @pl.when(jnp.max(excess) > 0.0)
    def _():
        nbits = max(1, (seq - 1).bit_length())

        def tie_body(i, mcur):
            cand = mcur | lax.shift_left(jnp.int32(1), nbits - 1 - i)
            cnt = count(lambda kch, col: (kch == thr) & (col < cand))
            return jnp.where(cnt < need, cand, mcur)

        mt_ref[...] = lax.fori_loop(0, nbits, tie_body, jnp.zeros((QB, 1), I32))

    mtie = mt_ref[...]

    qabs = _dot(q_ref[0].astype(BF16), wuk_ref[...]) * scale
    qs = jnp.concatenate([qabs[:, h * LANES:(h + 1) * LANES] for h in range(n_heads)],
                         axis=0).astype(BF16)
    m_ref[...] = jnp.full(m_ref.shape, NEG_INIT, F32)
    l_ref[...] = jnp.zeros(l_ref.shape, F32)
    acc_ref[...] = jnp.zeros(acc_ref.shape, F32)

    def att_chunk(ks, width, bias_of_head, col_limit):
        latb = lat_ref[0, pl.ds(ks, width), :].astype(BF16)
        logits = _dot(qs, latb, _NT)
        kch = key_ref[:, pl.ds(ks, width)]
        col = ks + lax.broadcasted_iota(I32, (1, width), 1)
        sel = ((kch > thr) | ((kch == thr) & (col <= mtie))) & (kch > INT_MIN)
        if col_limit is not None:
            sel = sel & (col < col_limit)
        ps = []
        for h in range(n_heads):
            rows = slice(h * QB, (h + 1) * QB)
            lg = jnp.where(sel, logits[rows] + bias_of_head(h), NEG_MASK)
            m_old = m_ref[rows]
            m_new = jnp.maximum(m_old, jnp.max(lg, axis=1, keepdims=True))
            p = jnp.exp(lg - m_new)
            alpha = jnp.exp(m_old - m_new)
            l_ref[rows] = alpha * l_ref[rows] + jnp.sum(p, axis=1, keepdims=True)
            acc_ref[rows] = alpha * acc_ref[rows]
            m_ref[rows] = m_new
            ps.append(p.astype(BF16))
        acc_ref[...] += _dot(jnp.concatenate(ps, axis=0), latb)

    far_end = jnp.maximum(qb - 1, 0) * QB
    n_far = (far_end + kc - 1) // kc

    def far_body(c, carry):
        att_chunk(pl.multiple_of(c * kc, kc), kc, lambda h: far_ref[h], far_end)
        return carry

    lax.fori_loop(0, n_far, far_body, 0)

    near_start = pl.multiple_of(far_end, QB)
    tile_id = jnp.where(qb == 0, 1, 0)
    att_chunk(near_start, NEAR, lambda h: bias_ref[tile_id, h], None)

    o = acc_ref[...] / l_ref[...]
    o_wide = jnp.concatenate([o[h * QB:(h + 1) * QB] for h in range(n_heads)], axis=1).astype(BF16)
    out_ref[0] = _dot(o_wide, wuv_ref[...])


def _dsa_call(q, qi, wi, ki2, lat, wuk_bd, wuv_bd, bias_near, bias_far, *, topk, kc):
    bsz, s, aw = q.shape
    n_heads = aw // HEAD_DIM
    c_lat = lat.shape[-1]
    nb = s // Q_BLOCK
    const = lambda shape: pl.BlockSpec(shape, lambda b, j: tuple(0 for _ in shape))
    blk = lambda w: pl.BlockSpec((1, Q_BLOCK, w), lambda b, j: (b, j, 0))
    full = lambda w: pl.BlockSpec((1, s, w), lambda b, j: (b, 0, 0))
    return pl.pallas_call(
        functools.partial(_dsa_kernel, topk=topk, kc=kc, n_heads=n_heads, seq=s,
                          scale=float(HEAD_DIM) ** -0.5),
        grid=(bsz, nb),
        in_specs=[
            blk(aw), blk(qi.shape[-1]), blk(LANES), full(LANES), full(c_lat),
            const(wuk_bd.shape), const(wuv_bd.shape), const(bias_near.shape), const(bias_far.shape),
        ],
        out_specs=blk(aw),
        out_shape=jax.ShapeDtypeStruct((bsz, s, aw), F32),
        scratch_shapes=[
            pltpu.VMEM((Q_BLOCK, s), I32),
            pltpu.VMEM((n_heads * Q_BLOCK, c_lat), F32),
            pltpu.VMEM((n_heads * Q_BLOCK, 1), F32),
            pltpu.VMEM((n_heads * Q_BLOCK, 1), F32),
            pltpu.VMEM((Q_BLOCK, 1), I32),
        ],
        compiler_params=pltpu.CompilerParams(
            dimension_semantics=("arbitrary", "arbitrary"),
            vmem_limit_bytes=VMEM_LIMIT_BYTES),
        name="dsa_attention",
    )(q, qi, wi, ki2, lat, wuk_bd, wuv_bd, bias_near, bias_far)


def _out_kernel(ya_ref, za_ref, yb_ref, zb_ref, ga_ref, gb_ref, x_ref, mod_ref,
                wpa_ref, wpb_ref, wo_ref, fg_ref, o_ref, *, d, final):
    za = za_ref[0]
    zb = zb_ref[0]
    ua = (ya_ref[0] * (za * _sigmoid(za))).astype(BF16)
    ub = (yb_ref[0] * (zb * _sigmoid(zb))).astype(BF16)
    br_a = _dot(ua, wpa_ref[...])
    br_b = _dot(ub, wpb_ref[...])
    merged = _sigmoid(ga_ref[0]) * br_a + _sigmoid(gb_ref[0]) * br_b
    gate = mod_ref[0, :, 2 * d:3 * d]
    xo = x_ref[0] + gate * _dot(merged.astype(BF16), wo_ref[...])
    if final:
        xo = xo * lax.rsqrt(jnp.mean(xo * xo, axis=-1, keepdims=True) + NORM_EPS) * fg_ref[...]
    o_ref[0] = xo


def _out_call(ya, za, yb, zb, ga, gb, x, mod, wpa, wpb, wo, final_g, *, final, tm):
    bsz, s, d = x.shape
    d3 = mod.shape[-1]
    row = lambda w: pl.BlockSpec((1, tm, w), lambda b, j: (b, j, 0))
    const = lambda shape: pl.BlockSpec(shape, lambda b, j: tuple(0 for _ in shape))
    return pl.pallas_call(
        functools.partial(_out_kernel, d=d, final=final),
        grid=(bsz, s // tm),
        in_specs=[
            row(ya.shape[-1]), row(za.shape[-1]), row(yb.shape[-1]), row(zb.shape[-1]),
            row(d), row(d), row(d),
            pl.BlockSpec((1, 1, d3), lambda b, j: (b, 0, 0)),
            const(wpa.shape), const(wpb.shape), const(wo.shape), const((1, d)),
        ],
        out_specs=row(d),
        out_shape=jax.ShapeDtypeStruct((bsz, s, d), F32),
        compiler_params=pltpu.CompilerParams(
            dimension_semantics=("arbitrary", "arbitrary"),
            vmem_limit_bytes=VMEM_LIMIT_BYTES),
        name="gated_out",
    )(ya, za, yb, zb, ga, gb, x, mod.reshape(bsz, 1, d3), wpa, wpb, wo, final_g.reshape(1, d))


def _t5_bucket(dist, num_buckets):
    max_exact = num_buckets // 2
    is_small = dist < max_exact
    dd = jnp.maximum(dist, 1).astype(F32)
    large = max_exact + (jnp.log(dd / max_exact) / math.log(MAX_DISTANCE / max_exact)
                         * (num_buckets - max_exact)).astype(I32)
    large = jnp.minimum(large, num_buckets - 1)
    return jnp.where(is_small, dist, large)


def _bias_tiles(rel_bias):
    nbk = rel_bias.shape[0]
    i = jnp.arange(Q_BLOCK, dtype=I32)[:, None]
    jj = jnp.arange(2 * Q_BLOCK, dtype=I32)[None, :]
    tiles = []
    for base in (Q_BLOCK, 0):
        bucket = _t5_bucket(jnp.maximum(base + i - jj, 0), nbk)
        tiles.append(jnp.moveaxis(rel_bias[bucket], -1, 0))
    far = rel_bias[_t5_bucket(jnp.full((1, 1), MAX_DISTANCE, I32), nbk)]
    return jnp.stack(tiles, axis=0), jnp.moveaxis(far, -1, 0)


def _block_diag_heads(w, rows_per_head, cols_per_head):
    n = w.shape[0]
    eye = jnp.eye(n, dtype=w.dtype)
    return jnp.einsum('hrc,hg->hrgc', w, eye).reshape(n * rows_per_head, n * cols_per_head)


def kernel(x, c, ada_w, ada_b, norm_g, w_in, shift_mu, w0, w2, a0, a2, k_k, k_a, r_k, lnx_g, lnx_b,
           kv_norm_g, w_uk, w_uv, w_pa, w_pb, w_o, rel_bias, final_g):
    bsz, s, d = x.shape
    depth = w_in.shape[0]
    rw = w0.shape[-1]
    c_lat, n_heads, dh = w_uk.shape[1:]
    aw = n_heads * dh
    iw = IDX_HEADS * IDX_HEAD_DIM
    n_shift = 3 * rw + DECAY_LORA + ICLR_LORA
    topk = min(TOPK_MAX, s // 4)
    assert DECAY_LORA + ICLR_LORA == LANES and dh == HEAD_DIM and IDX_HEAD_DIM == HEAD_DIM
    assert s % (2 * Q_BLOCK) == 0 and rw % LANES == 0

    sizes = (rw, rw, rw, DECAY_LORA, ICLR_LORA, rw, aw, c_lat, aw, iw, IDX_HEAD_DIM, IDX_HEADS, d, d)
    offs = [0]
    for sz in sizes:
        offs.append(offs[-1] + sz)
    col = lambda i: slice(offs[i], offs[i + 1])
    widths = (rw, aw, c_lat, aw, iw, LANES, LANES, d, d)

    mod = _mod_call(c, ada_w, ada_b)
    bias_near, bias_far = _bias_tiles(rel_bias)
    ones_bd = _block_diag_heads(jnp.ones((rw // HEAD_DIM, HEAD_DIM, HEAD_DIM), BF16), HEAD_DIM, HEAD_DIM)
    kc = 512 if s % 512 == 0 else 2 * Q_BLOCK
    tm_in = 256
    tm_out = 512 if s % 512 == 0 else 256

    for l in range(depth):
        w = w_in[l]
        wi_pad = jnp.zeros((d, LANES - IDX_HEADS), w.dtype)
        w_packed = jnp.concatenate(
            [w[:, 0:n_shift], w[:, col(5)], w[:, col(6)], w[:, col(7)], w[:, col(8)], w[:, col(9)],
             w[:, col(10)], w[:, col(10)], w[:, col(11)], wi_pad, w[:, col(12)], w[:, col(13)]],
            axis=1).astype(BF16)
        sh, za, q, lat, zb, qi, ki2, wi, ga, gb = _inproj_call(
            x, mod[l], norm_g[l], w_packed, shift_mu[l], kv_norm_g[l],
            n_shift=n_shift, widths=widths, tm=tm_in)

        vecs = jnp.stack([w0[l], a0[l], k_k[l], k_a[l], r_k[l].reshape(-1), lnx_g[l], lnx_b[l],
                          jnp.zeros((rw,), F32)], axis=0)
        w2p = jnp.concatenate([w2[l], jnp.zeros((ICLR_LORA, rw), F32)], axis=0)
        a2p = jnp.concatenate([jnp.zeros((DECAY_LORA, rw), F32), a2[l]], axis=0)
        ya = _rwkv_call(sh, vecs, w2p, a2p, ones_bd, width=rw)

        wuk_bd = _block_diag_heads(jnp.transpose(w_uk[l], (1, 2, 0)), dh, c_lat).astype(BF16)
        wuv_bd = _block_diag_heads(jnp.transpose(w_uv[l], (1, 0, 2)), c_lat, dh).astype(BF16)
        yb = _dsa_call(q, qi, wi, ki2, lat, wuk_bd, wuv_bd, bias_near, bias_far, topk=topk, kc=kc)

        x = _out_call(ya, za, yb, zb, ga, gb, x, mod[l], w_pa[l].astype(BF16), w_pb[l].astype(BF16),
                      w_o[l].astype(BF16), final_g, final=(l == depth - 1), tm=tm_out)
    return x
```

```python
import functools
import math

import jax
import jax.numpy as jnp
from jax import lax
from jax.experimental import pallas as pl
from jax.experimental.pallas import tpu as pltpu

F32 = jnp.float32
BF16 = jnp.bfloat16
I32 = jnp.int32

DECAY_LORA = 64
ICLR_LORA = 64
GN_EPS = 64e-5
IDX_HEADS = 8
IDX_HEAD_DIM = 64
TOPK_MAX = 256
Q_BLOCK = 128
MAX_DISTANCE = 128
NORM_EPS = 1e-6

LANES = 128
VMEM_LIMIT_BYTES = 56 * 1024 * 1024

HEAD_DIM = 64
CHUNK = 64
INT_MIN = -(2 ** 31)
INT_MAX = 2 ** 31 - 1
NEG_MASK = -2e30
NEG_INIT = -1e30


def _sigmoid(x):
    return 1.0 / (1.0 + jnp.exp(-x))


def _dot(a, b, dims=(((1,), (0,)), ((), ()))):
    return lax.dot_general(a, b, dims, preferred_element_type=F32)


_NN = (((1,), (0,)), ((), ()))
_NT = (((1,), (1,)), ((), ()))
_TN = (((0,), (0,)), ((), ()))


def _split(x, n):
    parts = []
    rem = x
    for i in range(n):
        p = rem.astype(BF16)
        parts.append(p)
        if i + 1 < n:
            rem = rem - p.astype(F32)
    return parts


def _mm(a, b, passes, dims=_NN):
    if passes == 1:
        return _dot(a.astype(BF16), b.astype(BF16), dims)
    if passes == 3:
        a1, a2 = _split(a, 2)
        b1, b2 = _split(b, 2)
        return _dot(a1, b1, dims) + (_dot(a1, b2, dims) + _dot(a2, b1, dims))
    a1, a2, a3 = _split(a, 3)
    b1, b2, b3 = _split(b, 3)
    hi = _dot(a1, b1, dims)
    mid = _dot(a1, b2, dims) + _dot(a2, b1, dims)
    lo = _dot(a2, b2, dims) + (_dot(a1, b3, dims) + _dot(a3, b1, dims))
    return hi + (mid + lo)


def _mm_exact_rhs(a, b_bf16, n):
    out = None
    for p in _split(a, n):
        t = _dot(p, b_bf16)
        out = t if out is None else out + t
    return out


def _mod_kernel(c_ref, w_ref, b_ref, o_ref):
    c = c_ref[...]
    ca = c * _sigmoid(c)
    o_ref[0] = _mm(ca, w_ref[0], 6) + b_ref[0]


def _mod_call(c, ada_w, ada_b):
    depth, d, d3 = ada_w.shape
    bsz = c.shape[0]
    tn = d
    return pl.pallas_call(
        _mod_kernel,
        grid=(depth, d3 // tn),
        in_specs=[
            pl.BlockSpec((bsz, d), lambda l, n: (0, 0)),
            pl.BlockSpec((1, d, tn), lambda l, n: (l, 0, n)),
            pl.BlockSpec((1, 1, tn), lambda l, n: (l, 0, n)),
        ],
        out_specs=pl.BlockSpec((1, bsz, tn), lambda l, n: (l, 0, n)),
        out_shape=jax.ShapeDtypeStruct((depth, bsz, d3), F32),
        compiler_params=pltpu.CompilerParams(
            dimension_semantics=("arbitrary", "arbitrary"),
            vmem_limit_bytes=VMEM_LIMIT_BYTES),
        name="adaln_mod",
    )(c, ada_w, ada_b.reshape(depth, 1, d3))


def _inproj_kernel(x_ref, mod_ref, g_ref, w_ref, mu_ref, kvg_ref,
                   sh_ref, za_ref, q_ref, lat_ref, zb_ref, qi_ref, ki_ref, wi_ref, ga_ref, gb_ref,
                   carry_ref, *, d, n_shift, widths):
    j = pl.program_id(1)
    xb = x_ref[0]
    tm = xb.shape[0]
    shift = mod_ref[0, :, 0:d]
    scale = mod_ref[0, :, d:2 * d]
    ms = jnp.mean(xb * xb, axis=-1, keepdims=True)
    h = xb * lax.rsqrt(ms + NORM_EPS) * g_ref[...]
    h = h * (1.0 + scale) + shift
    hb = h.astype(BF16)

    @pl.when(j == 0)
    def _():
        carry_ref[...] = jnp.zeros_like(carry_ref)

    ps = _dot(hb, w_ref[:, 0:n_shift])
    prev = pltpu.roll(ps, 1, 0)
    row = lax.broadcasted_iota(I32, (tm, 1), 0)
    prev = jnp.where(row == 0, carry_ref[...], prev)
    carry_ref[...] = ps[tm - 1:tm, :]
    sh_ref[0] = ps + (prev - ps) * mu_ref[...]

    off = n_shift
    outs = (za_ref, q_ref, lat_ref, zb_ref, qi_ref, ki_ref, wi_ref, ga_ref, gb_ref)
    for o_ref, wd in zip(outs, widths):
        p = _dot(hb, w_ref[:, off:off + wd])
        if o_ref is lat_ref:
            p = p * lax.rsqrt(jnp.mean(p * p, axis=-1, keepdims=True) + NORM_EPS) * kvg_ref[...]
        o_ref[0] = p
        off += wd


def _inproj_call(x, mod, norm_g, w_packed, mu, kvg, *, n_shift, widths, tm):
    bsz, s, d = x.shape
    npk = w_packed.shape[1]
    d3 = mod.shape[-1]
    row_spec = lambda w: pl.BlockSpec((1, tm, w), lambda b, j: (b, j, 0))
    const = lambda shape: pl.BlockSpec(shape, lambda b, j: tuple(0 for _ in shape))
    out_widths = (n_shift,) + tuple(widths)
    return pl.pallas_call(
        functools.partial(_inproj_kernel, d=d, n_shift=n_shift, widths=tuple(widths)),
        grid=(bsz, s // tm),
        in_specs=[
            row_spec(d),
            pl.BlockSpec((1, 1, d3), lambda b, j: (b, 0, 0)),
            const((1, d)),
            pl.BlockSpec((d, npk), lambda b, j: (0, 0), pipeline_mode=pl.Buffered(1)),
            const((1, n_shift)),
            const((1, widths[2])),
        ],
        out_specs=[row_spec(w) for w in out_widths],
        out_shape=[jax.ShapeDtypeStruct((bsz, s, w), F32) for w in out_widths],
        scratch_shapes=[pltpu.VMEM((1, n_shift), F32)],
        compiler_params=pltpu.CompilerParams(
            dimension_semantics=("arbitrary", "arbitrary"),
            vmem_limit_bytes=VMEM_LIMIT_BYTES),
        name="inproj",
    )(x, mod.reshape(bsz, 1, d3), norm_g.reshape(1, d), w_packed, mu.reshape(1, n_shift),
      kvg.reshape(1, -1))


def _rwkv_kernel(sh_ref, vec_ref, w2_ref, a2_ref, ones_ref, out_ref, st_ref,
                 *, width, p_chain, p_inv, p_state):
    j = pl.program_id(1)
    L = CHUNK
    n_pairs = width // LANES

    @pl.when(j == 0)
    def _():
        st_ref[...] = jnp.zeros_like(st_ref)

    blk = sh_ref[0]
    pr = blk[:, 0:width]
    pk = blk[:, width:2 * width]
    pv = blk[:, 2 * width:3 * width]
    lora = blk[:, 3 * width:3 * width + LANES]
    lane = lax.broadcasted_iota(I32, (1, LANES), 1)
    lo_half = lane < HEAD_DIM
    t_in = jnp.where(lo_half, jnp.tanh(lora), lora)
    dec_in = _mm(t_in, w2_ref[...], 3)
    a_in = _mm(t_in, a2_ref[...], 3)

    w0 = vec_ref[0:1, :]
    a0 = vec_ref[1:2, :]
    k_k = vec_ref[2:3, :]
    k_a = vec_ref[3:4, :]
    r_k = vec_ref[4:5, :]
    lnx_g = vec_ref[5:6, :]
    lnx_b = vec_ref[6:7, :]
    ones_bd = ones_ref[...]

    z = -(w0 + dec_in)
    sp = jnp.maximum(z, 0.0) + jnp.log1p(jnp.exp(-jnp.abs(z)))
    logw = -jnp.exp(-sp - 0.5)
    a = _sigmoid(a0 + a_in)
    kk0 = pk * k_k
    ss = _mm_exact_rhs(kk0 * kk0, ones_bd, 3)
    kk = kk0 / jnp.maximum(jnp.sqrt(ss), 1e-12)
    k_eff = pk * (1.0 + (a - 1.0) * k_a)
    a_vec = -kk
    b_vec = kk * a

    r_i = lax.broadcasted_iota(I32, (L, L), 0)
    c_i = lax.broadcasted_iota(I32, (L, L), 1)
    tril = jnp.where(c_i <= r_i, 1.0, 0.0).astype(BF16)
    cl = None
    for part in _split(logw, 3):
        t = _dot(tril, part)
        cl = t if cl is None else cl + t
    cl_last = cl[L - 1:L, :]
    rt = pr * jnp.exp(cl)
    at = a_vec * jnp.exp(cl - logw)
    e_inv = jnp.exp(-cl)
    bt = b_vec * e_inv
    kt = k_eff * e_inv
    e_l = jnp.exp(cl_last - cl)
    bl = b_vec * e_l
    kl = k_eff * e_l
    p_last = jnp.exp(cl_last)

    rr = lax.broadcasted_iota(I32, (LANES, LANES), 0)
    cc = lax.broadcasted_iota(I32, (LANES, LANES), 1)
    same = (rr // L) == (cc // L)
    strict = same & ((cc % L) < (rr % L))
    incl = same & ((cc % L) <= (rr % L))
    eye = rr == cc
    eye_f = jnp.where(eye, 1.0, 0.0)

    def stack(x):
        return jnp.concatenate([jnp.where(lo_half, x, 0.0), jnp.where(lo_half, 0.0, x)], axis=0)

    ys = []
    for p in range(n_pairs):
        sl = slice(p * LANES, (p + 1) * LANES)
        atm, rtm, btm, ktm = stack(at[:, sl]), stack(rt[:, sl]), stack(bt[:, sl]), stack(kt[:, sl])
        vm, blm, klm = stack(pv[:, sl]), stack(bl[:, sl]), stack(kl[:, sl])
        g = _mm(jnp.concatenate([atm, rtm], axis=0), jnp.concatenate([btm, ktm], axis=0),
                p_chain, _NT)
        a_ab = jnp.where(strict, g[0:LANES, 0:LANES], 0.0)
        a_ak = jnp.where(strict, g[0:LANES, LANES:], 0.0)
        a_rb = jnp.where(incl, g[LANES:, 0:LANES], 0.0)
        a_rk = jnp.where(incl, g[LANES:, LANES:], 0.0)
        tinv = eye_f + a_ab
        qn = a_ab
        sq = 1
        while 2 * sq < L:
            qn = _mm(qn, qn, p_inv)
            tinv = tinv + _mm(tinv, qn, p_inv)
            sq *= 2
        akv = _mm(a_ak, vm, p_chain)
        wz = _mm(tinv, jnp.concatenate([atm, akv], axis=1), p_inv)
        wm = wz[:, 0:LANES]
        zm = wz[:, LANES:]
        zv = jnp.concatenate([zm, vm], axis=0)
        rqm = rtm + _mm(a_rb, wm, p_chain)
        y0m = _mm(jnp.concatenate([a_rb, a_rk], axis=1), zv, p_chain)
        st = st_ref[p]
        ym = _mm(rqm, st, p_state) + y0m
        ys.append(ym[0:L] + ym[L:])
        mt = jnp.where(eye, p_last[:, sl], 0.0) + _mm(blm, wm, p_state, _TN)
        ct = _mm(jnp.concatenate([blm, klm], axis=0), zv, p_state, _TN)
        st_ref[p] = _mm(mt, st, p_state) + ct
    y = jnp.concatenate(ys, axis=1)

    inv_n = 1.0 / HEAD_DIM
    mu = _mm_exact_rhs(y, ones_bd, 3) * inv_n
    dlt = y - mu
    var = _mm_exact_rhs(dlt * dlt, ones_bd, 3) * inv_n
    yn = dlt * lax.rsqrt(var + GN_EPS) * lnx_g + lnx_b
    bonus = _mm_exact_rhs(pr * k_eff * r_k, ones_bd, 3) * pv
    out_ref[0] = yn + bonus


def _rwkv_call(sh, vecs, w2p, a2p, ones_bd, *, width, p_chain=1, p_inv=1, p_state=3):
    bsz, s, n_shift = sh.shape
    n_pairs = width // LANES
    const = lambda shape: pl.BlockSpec(shape, lambda b, j: tuple(0 for _ in shape))
    return pl.pallas_call(
        functools.partial(_rwkv_kernel, width=width, p_chain=p_chain, p_inv=p_inv, p_state=p_state),
        grid=(bsz, s // CHUNK),
        in_specs=[
            pl.BlockSpec((1, CHUNK, n_shift), lambda b, j: (b, j, 0)),
            const(vecs.shape),
            const(w2p.shape),
            const(a2p.shape),
            const(ones_bd.shape),
        ],
        out_specs=pl.BlockSpec((1, CHUNK, width), lambda b, j: (b, j, 0)),
        out_shape=jax.ShapeDtypeStruct((bsz, s, width), F32),
        scratch_shapes=[pltpu.VMEM((n_pairs, LANES, LANES), F32)],
        compiler_params=pltpu.CompilerParams(
            dimension_semantics=("arbitrary", "arbitrary"),
            vmem_limit_bytes=VMEM_LIMIT_BYTES),
        name="rwkv7_scan",
    )(sh, vecs, w2p, a2p, ones_bd)


def _dsa_kernel(q_ref, qi_ref, wi_ref, ki_ref, lat_ref, wuk_ref, wuv_ref, bias_ref, far_ref, out_ref,
                key_ref, acc_ref, m_ref, l_ref, mt_ref,
                *, topk, kc, n_heads, seq, scale):
    qb = pl.program_id(1)
    QB = Q_BLOCK
    NEAR = 2 * QB
    q_end = (qb + 1) * QB
    n_idx = (jnp.maximum(q_end, NEAR) + kc - 1) // kc
    lane = lax.broadcasted_iota(I32, (1, LANES), 1)
    lo_half = lane < HEAD_DIM
    tpos = qb * QB + lax.broadcasted_iota(I32, (QB, 1), 0)

    qi = qi_ref[0]
    wi = wi_ref[0]
    parts = []
    for h in range(n_heads):
        pair = qi[:, (h // 2) * LANES:(h // 2 + 1) * LANES]
        parts.append(jnp.where(lo_half, pair, 0.0) if h % 2 == 0 else jnp.where(lo_half, 0.0, pair))
    qis = jnp.concatenate(parts, axis=0).astype(BF16)
    wis = [wi[:, h:h + 1] for h in range(n_heads)]

    def idx_body(c, carry):
        ks = pl.multiple_of(c * kc, kc)
        kib = ki_ref[0, pl.ds(ks, kc), :].astype(BF16)
        s_all = jnp.maximum(_dot(qis, kib, _NT), 0.0)
        score = wis[0] * s_all[0:QB]
        for h in range(1, n_heads):
            score = score + wis[h] * s_all[h * QB:(h + 1) * QB]
        score = score + 0.0
        bits = pltpu.bitcast(score, I32)
        key = bits ^ ((bits >> 31) & INT_MAX)
        spos = ks + lax.broadcasted_iota(I32, (1, kc), 1)
        key_ref[:, pl.ds(ks, kc)] = jnp.where(spos <= tpos, key, INT_MIN)
        return carry

    lax.fori_loop(0, n_idx, idx_body, 0)

    def count(pred):
        def body(c, acc):
            ks = pl.multiple_of(c * kc, kc)
            kch = key_ref[:, pl.ds(ks, kc)]
            col = ks + lax.broadcasted_iota(I32, (1, kc), 1)
            hit = jnp.where(pred(kch, col), 1.0, 0.0)
            for t in range(kc // LANES):
                acc = acc + hit[:, t * LANES:(t + 1) * LANES]
            return acc
        acc = lax.fori_loop(0, n_idx, body, jnp.zeros((QB, LANES), F32))
        return jnp.sum(acc, axis=1, keepdims=True)

    kf = float(topk)

    def bit_body(i, thr_u):
        cand_u = thr_u | lax.shift_left(jnp.int32(1), 31 - i)
        cand = cand_u ^ INT_MIN
        cnt = count(lambda kch, col: kch >= cand)
        return jnp.where(cnt >= kf, cand_u, thr_u)

    thr = lax.fori_loop(0, 32, bit_body, jnp.zeros((QB, 1), I32)) ^ INT_MIN

    cnt_gt = count(lambda kch, col: kch > thr)
    cnt_ge = count(lambda kch, col: kch >= thr)
    need = kf - cnt_gt
    excess = jnp.where((cnt_ge > kf) & (thr > INT_MIN), 1.0, 0.0)
    mt_ref[...] = jnp.full((QB, 1), INT_MAX, I32)

    @pl.when(jnp.max(excess) > 0.0)
    def _():
        nbits = max(1, (seq - 1).bit_length())

        def tie_body(i, mcur):
            cand = mcur | lax.shift_left(jnp.int32(1), nbits - 1 - i)
            cnt = count(lambda kch, col: (kch == thr) & (col < cand))
            return jnp.where(cnt < need, cand, mcur)

        mt_ref[...] = lax.fori_loop(0, nbits, tie_body, jnp.zeros((QB, 1), I32))

    mtie = mt_ref[...]

    qabs = _dot(q_ref[0].astype(BF16), wuk_ref[...]) * scale
    qs = jnp.concatenate([qabs[:, h * LANES:(h + 1) * LANES] for h in range(n_heads)],
                         axis=0).astype(BF16)
    m_ref[...] = jnp.full(m_ref.shape, NEG_INIT, F32)
    l_ref[...] = jnp.zeros(l_ref.shape, F32)
    acc_ref[...] = jnp.zeros(acc_ref.shape, F32)

    def att_chunk(ks, width, bias_of_head, col_limit):
        latb = lat_ref[0, pl.ds(ks, width), :].astype(BF16)
        logits = _dot(qs, latb, _NT)
        kch = key_ref[:, pl.ds(ks, width)]
        col = ks + lax.broadcasted_iota(I32, (1, width), 1)
        sel = ((kch > thr) | ((kch == thr) & (col <= mtie))) & (kch > INT_MIN)
        if col_limit is not None:
            sel = sel & (col < col_limit)
        ps = []
        for h in range(n_heads):
            rows = slice(h * QB, (h + 1) * QB)
            lg = jnp.where(sel, logits[rows] + bias_of_head(h), NEG_MASK)
            m_old = m_ref[rows]
            m_new = jnp.maximum(m_old, jnp.max(lg, axis=1, keepdims=True))
            p = jnp.exp(lg - m_new)
            alpha = jnp.exp(m_old - m_new)
            l_ref[rows] = alpha * l_ref[rows] + jnp.sum(p, axis=1, keepdims=True)
            acc_ref[rows] = alpha * acc_ref[rows]
            m_ref[rows] = m_new
            ps.append(p.astype(BF16))
        acc_ref[...] += _dot(jnp.concatenate(ps, axis=0), latb)

    far_end = jnp.maximum(qb - 1, 0) * QB
    n_far = (far_end + kc - 1) // kc

    def far_body(c, carry):
        att_chunk(pl.multiple_of(c * kc, kc), kc, lambda h: far_ref[h], far_end)
        return carry

    lax.fori_loop(0, n_far, far_body, 0)

    near_start = pl.multiple_of(far_end, QB)
    tile_id = jnp.where(qb == 0, 1, 0)
    att_chunk(near_start, NEAR, lambda h: bias_ref[tile_id, h], None)

    o = acc_ref[...] / l_ref[...]
    o_wide = jnp.concatenate([o[h * QB:(h + 1) * QB] for h in range(n_heads)], axis=1).astype(BF16)
    out_ref[0] = _dot(o_wide, wuv_ref[...])


def _dsa_call(q, qi, wi, ki2, lat, wuk_bd, wuv_bd, bias_near, bias_far, *, topk, kc):
    bsz, s, aw = q.shape
    n_heads = aw // HEAD_DIM
    c_lat = lat.shape[-1]
    nb = s // Q_BLOCK
    const = lambda shape: pl.BlockSpec(shape, lambda b, j: tuple(0 for _ in shape))
    blk = lambda w: pl.BlockSpec((1, Q_BLOCK, w), lambda b, j: (b, j, 0))
    full = lambda w: pl.BlockSpec((1, s, w), lambda b, j: (b, 0, 0))
    return pl.pallas_call(
        functools.partial(_dsa_kernel, topk=topk, kc=kc, n_heads=n_heads, seq=s,
                          scale=float(HEAD_DIM) ** -0.5),
        grid=(bsz, nb),
        in_specs=[
            blk(aw), blk(qi.shape[-1]), blk(LANES), full(LANES), full(c_lat),
            const(wuk_bd.shape), const(wuv_bd.shape), const(bias_near.shape), const(bias_far.shape),
        ],
        out_specs=blk(aw),
        out_shape=jax.ShapeDtypeStruct((bsz, s, aw), F32),
        scratch_shapes=[
            pltpu.VMEM((Q_BLOCK, s), I32),
            pltpu.VMEM((n_heads * Q_BLOCK, c_lat), F32),
            pltpu.VMEM((n_heads * Q_BLOCK, 1), F32),
            pltpu.VMEM((n_heads * Q_BLOCK, 1), F32),
            pltpu.VMEM((Q_BLOCK, 1), I32),
        ],
        compiler_params=pltpu.CompilerParams(
            dimension_semantics=("arbitrary", "arbitrary"),
            vmem_limit_bytes=VMEM_LIMIT_BYTES),
        name="dsa_attention",
    )(q, qi, wi, ki2, lat, wuk_bd, wuv_bd, bias_near, bias_far)


def _out_kernel(ya_ref, za_ref, yb_ref, zb_ref, ga_ref, gb_ref, x_ref, mod_ref,
                wpa_ref, wpb_ref, wo_ref, fg_ref, o_ref, *, d, final):
    za = za_ref[0]
    zb = zb_ref[0]
    ua = (ya_ref[0] * (za * _sigmoid(za))).astype(BF16)
    ub = (yb_ref[0] * (zb * _sigmoid(zb))).astype(BF16)
    br_a = _dot(ua, wpa_ref[...])
    br_b = _dot(ub, wpb_ref[...])
    merged = _sigmoid(ga_ref[0]) * br_a + _sigmoid(gb_ref[0]) * br_b
    gate = mod_ref[0, :, 2 * d:3 * d]
    xo = x_ref[0] + gate * _dot(merged.astype(BF16), wo_ref[...])
    if final:
        xo = xo * lax.rsqrt(jnp.mean(xo * xo, axis=-1, keepdims=True) + NORM_EPS) * fg_ref[...]
    o_ref[0] = xo


def _out_call(ya, za, yb, zb, ga, gb, x, mod, wpa, wpb, wo, final_g, *, final, tm):
    bsz, s, d = x.shape
    d3 = mod.shape[-1]
    row = lambda w: pl.BlockSpec((1, tm, w), lambda b, j: (b, j, 0))
    const = lambda shape: pl.BlockSpec(shape, lambda b, j: tuple(0 for _ in shape))
    return pl.pallas_call(
        functools.partial(_out_kernel, d=d, final=final),
        grid=(bsz, s // tm),
        in_specs=[
            row(ya.shape[-1]), row(za.shape[-1]), row(yb.shape[-1]), row(zb.shape[-1]),
            row(d), row(d), row(d),
            pl.BlockSpec((1, 1, d3), lambda b, j: (b, 0, 0)),
            const(wpa.shape), const(wpb.shape), const(wo.shape), const((1, d)),
        ],
        out_specs=row(d),
        out_shape=jax.ShapeDtypeStruct((bsz, s, d), F32),
        compiler_params=pltpu.CompilerParams(
            dimension_semantics=("arbitrary", "arbitrary"),
            vmem_limit_bytes=VMEM_LIMIT_BYTES),
        name="gated_out",
    )(ya, za, yb, zb, ga, gb, x, mod.reshape(bsz, 1, d3), wpa, wpb, wo, final_g.reshape(1, d))


def _t5_bucket(dist, num_buckets):
    max_exact = num_buckets // 2
    is_small = dist < max_exact
    dd = jnp.maximum(dist, 1).astype(F32)
    large = max_exact + (jnp.log(dd / max_exact) / math.log(MAX_DISTANCE / max_exact)
                         * (num_buckets - max_exact)).astype(I32)
    large = jnp.minimum(large, num_buckets - 1)
    return jnp.where(is_small, dist, large)


def _bias_tiles(rel_bias):
    nbk = rel_bias.shape[0]
    i = jnp.arange(Q_BLOCK, dtype=I32)[:, None]
    jj = jnp.arange(2 * Q_BLOCK, dtype=I32)[None, :]
    tiles = []
    for base in (Q_BLOCK, 0):
        bucket = _t5_bucket(jnp.maximum(base + i - jj, 0), nbk)
        onehot = (bucket[None, :, :] == jnp.arange(nbk, dtype=I32)[:, None, None]).astype(F32)
        tiles.append(jnp.einsum('nh,nij->hij', rel_bias, onehot,
                                precision=lax.Precision.HIGHEST))
    far = rel_bias[_t5_bucket(jnp.full((1, 1), MAX_DISTANCE, I32), nbk)]
    return jnp.stack(tiles, axis=0), jnp.moveaxis(far, -1, 0)


def _block_diag_heads(w, rows_per_head, cols_per_head):
    n = w.shape[0]
    eye = jnp.eye(n, dtype=w.dtype)
    return jnp.einsum('hrc,hg->hrgc', w, eye).reshape(n * rows_per_head, n * cols_per_head)


def kernel(x, c, ada_w, ada_b, norm_g, w_in, shift_mu, w0, w2, a0, a2, k_k, k_a, r_k, lnx_g, lnx_b,
           kv_norm_g, w_uk, w_uv, w_pa, w_pb, w_o, rel_bias, final_g):
    bsz, s, d = x.shape
    depth = w_in.shape[0]
    rw = w0.shape[-1]
    c_lat, n_heads, dh = w_uk.shape[1:]
    aw = n_heads * dh
    iw = IDX_HEADS * IDX_HEAD_DIM
    n_shift = 3 * rw + DECAY_LORA + ICLR_LORA
    topk = min(TOPK_MAX, s // 4)
    assert DECAY_LORA + ICLR_LORA == LANES and dh == HEAD_DIM and IDX_HEAD_DIM == HEAD_DIM
    assert s % (2 * Q_BLOCK) == 0 and rw % LANES == 0

    sizes = (rw, rw, rw, DECAY_LORA, ICLR_LORA, rw, aw, c_lat, aw, iw, IDX_HEAD_DIM, IDX_HEADS, d, d)
    offs = [0]
    for sz in sizes:
        offs.append(offs[-1] + sz)
    col = lambda i: slice(offs[i], offs[i + 1])
    widths = (rw, aw, c_lat, aw, iw, LANES, LANES, d, d)

    mod = _mod_call(c, ada_w, ada_b)
    bias_near, bias_far = _bias_tiles(rel_bias)
    ones_bd = _block_diag_heads(jnp.ones((rw // HEAD_DIM, HEAD_DIM, HEAD_DIM), BF16), HEAD_DIM, HEAD_DIM)
    kc = 512 if s % 512 == 0 else 2 * Q_BLOCK
    tm_in = 256
    tm_out = 512 if s % 512 == 0 else 256

    for l in range(depth):
        w = w_in[l]
        wi_pad = jnp.zeros((d, LANES - IDX_HEADS), w.dtype)
        w_packed = jnp.concatenate(
            [w[:, 0:n_shift], w[:, col(5)], w[:, col(6)], w[:, col(7)], w[:, col(8)], w[:, col(9)],
             w[:, col(10)], w[:, col(10)], w[:, col(11)], wi_pad, w[:, col(12)], w[:, col(13)]],
            axis=1).astype(BF16)
        sh, za, q, lat, zb, qi, ki2, wi, ga, gb = _inproj_call(
            x, mod[l], norm_g[l], w_packed, shift_mu[l], kv_norm_g[l],
            n_shift=n_shift, widths=widths, tm=tm_in)

        vecs = jnp.stack([w0[l], a0[l], k_k[l], k_a[l], r_k[l].reshape(-1), lnx_g[l], lnx_b[l],
                          jnp.zeros((rw,), F32)], axis=0)
        w2p = jnp.concatenate([w2[l], jnp.zeros((ICLR_LORA, rw), F32)], axis=0)
        a2p = jnp.concatenate([jnp.zeros((DECAY_LORA, rw), F32), a2[l]], axis=0)
        ya = _rwkv_call(sh, vecs, w2p, a2p, ones_bd, width=rw)

        wuk_bd = _block_diag_heads(jnp.transpose(w_uk[l], (1, 2, 0)), dh, c_lat).astype(BF16)
        wuv_bd = _block_diag_heads(jnp.transpose(w_uv[l], (1, 0, 2)), c_lat, dh).astype(BF16)
        yb = _dsa_call(q, qi, wi, ki2, lat, wuk_bd, wuv_bd, bias_near, bias_far, topk=topk, kc=kc)

        x = _out_call(ya, za, yb, zb, ga, gb, x, mod[l], w_pa[l].astype(BF16), w_pb[l].astype(BF16),
                      w_o[l].astype(BF16), final_g, final=(l == depth - 1), tm=tm_out)
    return x
```

```python
import functools
import math

import jax
import jax.numpy as jnp
from jax import lax
from jax.experimental import pallas as pl
from jax.experimental.pallas import tpu as pltpu

F32 = jnp.float32
BF16 = jnp.bfloat16
I32 = jnp.int32

DECAY_LORA = 64
ICLR_LORA = 64
GN_EPS = 64e-5
IDX_HEADS = 8
IDX_HEAD_DIM = 64
TOPK_MAX = 256
Q_BLOCK = 128
MAX_DISTANCE = 128
NORM_EPS = 1e-6

LANES = 128
VMEM_LIMIT_BYTES = 56 * 1024 * 1024

HEAD_DIM = 64
CHUNK = 64
INT_MIN = -(2 ** 31)
INT_MAX = 2 ** 31 - 1
NEG_MASK = -2e30
NEG_INIT = -1e30


def _sigmoid(x):
    return 1.0 / (1.0 + jnp.exp(-x))


def _dot(a, b, dims=(((1,), (0,)), ((), ()))):
    return lax.dot_general(a, b, dims, preferred_element_type=F32)


_NN = (((1,), (0,)), ((), ()))
_NT = (((1,), (1,)), ((), ()))
_TN = (((0,), (0,)), ((), ()))


def _split(x, n):
    parts = []
    rem = x
    for i in range(n):
        p = rem.astype(BF16)
        parts.append(p)
        if i + 1 < n:
            rem = rem - p.astype(F32)
    return parts


def _mm(a, b, passes, dims=_NN):
    if passes == 1:
        return _dot(a.astype(BF16), b.astype(BF16), dims)
    if passes == 3:
        a1, a2 = _split(a, 2)
        b1, b2 = _split(b, 2)
        return _dot(a1, b1, dims) + (_dot(a1, b2, dims) + _dot(a2, b1, dims))
    a1, a2, a3 = _split(a, 3)
    b1, b2, b3 = _split(b, 3)
    hi = _dot(a1, b1, dims)
    mid = _dot(a1, b2, dims) + _dot(a2, b1, dims)
    lo = _dot(a2, b2, dims) + (_dot(a1, b3, dims) + _dot(a3, b1, dims))
    return hi + (mid + lo)


def _mm_exact_rhs(a, b_bf16, n):
    out = None
    for p in _split(a, n):
        t = _dot(p, b_bf16)
        out = t if out is None else out + t
    return out


def _mod_kernel(c_ref, w_ref, b_ref, o_ref):
    c = c_ref[...]
    ca = c * _sigmoid(c)
    o_ref[0] = _mm(ca, w_ref[0], 6) + b_ref[0]


def _mod_call(c, ada_w, ada_b):
    depth, d, d3 = ada_w.shape
    bsz = c.shape[0]
    tn = d
    return pl.pallas_call(
        _mod_kernel,
        grid=(depth, d3 // tn),
        in_specs=[
            pl.BlockSpec((bsz, d), lambda l, n: (0, 0)),
            pl.BlockSpec((1, d, tn), lambda l, n: (l, 0, n)),
            pl.BlockSpec((1, 1, tn), lambda l, n: (l, 0, n)),
        ],
        out_specs=pl.BlockSpec((1, bsz, tn), lambda l, n: (l, 0, n)),
        out_shape=jax.ShapeDtypeStruct((depth, bsz, d3), F32),
        compiler_params=pltpu.CompilerParams(
            dimension_semantics=("arbitrary", "arbitrary"),
            vmem_limit_bytes=VMEM_LIMIT_BYTES),
        name="adaln_mod",
    )(c, ada_w, ada_b.reshape(depth, 1, d3))


def _inproj_kernel(x_ref, mod_ref, g_ref, w_ref, mu_ref, kvg_ref,
                   sh_ref, za_ref, q_ref, lat_ref, zb_ref, qi_ref, ki_ref, wi_ref, ga_ref, gb_ref,
                   carry_ref, *, d, n_shift, widths):
    j = pl.program_id(1)
    xb = x_ref[0]
    tm = xb.shape[0]
    shift = mod_ref[0, :, 0:d]
    scale = mod_ref[0, :, d:2 * d]
    ms = jnp.mean(xb * xb, axis=-1, keepdims=True)
    h = xb * lax.rsqrt(ms + NORM_EPS) * g_ref[...]
    h = h * (1.0 + scale) + shift
    hb = h.astype(BF16)

    @pl.when(j == 0)
    def _():
        carry_ref[...] = jnp.zeros_like(carry_ref)

    ps = _dot(hb, w_ref[:, 0:n_shift])
    prev = pltpu.roll(ps, 1, 0)
    row = lax.broadcasted_iota(I32, (tm, 1), 0)
    prev = jnp.where(row == 0, carry_ref[...], prev)
    carry_ref[...] = ps[tm - 1:tm, :]
    sh_ref[0] = ps + (prev - ps) * mu_ref[...]

    off = n_shift
    outs = (za_ref, q_ref, lat_ref, zb_ref, qi_ref, ki_ref, wi_ref, ga_ref, gb_ref)
    for o_ref, wd in zip(outs, widths):
        p = _dot(hb, w_ref[:, off:off + wd])
        if o_ref is lat_ref:
            p = p * lax.rsqrt(jnp.mean(p * p, axis=-1, keepdims=True) + NORM_EPS) * kvg_ref[...]
        o_ref[0] = p
        off += wd


def _inproj_call(x, mod, norm_g, w_packed, mu, kvg, *, n_shift, widths, tm):
    bsz, s, d = x.shape
    npk = w_packed.shape[1]
    d3 = mod.shape[-1]
    row_spec = lambda w: pl.BlockSpec((1, tm, w), lambda b, j: (b, j, 0))
    const = lambda shape: pl.BlockSpec(shape, lambda b, j: tuple(0 for _ in shape))
    out_widths = (n_shift,) + tuple(widths)
    return pl.pallas_call(
        functools.partial(_inproj_kernel, d=d, n_shift=n_shift, widths=tuple(widths)),
        grid=(bsz, s // tm),
        in_specs=[
            row_spec(d),
            pl.BlockSpec((1, 1, d3), lambda b, j: (b, 0, 0)),
            const((1, d)),
            pl.BlockSpec((d, npk), lambda b, j: (0, 0), pipeline_mode=pl.Buffered(1)),
            const((1, n_shift)),
            const((1, widths[2])),
        ],
        out_specs=[row_spec(w) for w in out_widths],
        out_shape=[jax.ShapeDtypeStruct((bsz, s, w), F32) for w in out_widths],
        scratch_shapes=[pltpu.VMEM((1, n_shift), F32)],
        compiler_params=pltpu.CompilerParams(
            dimension_semantics=("arbitrary", "arbitrary"),
            vmem_limit_bytes=VMEM_LIMIT_BYTES),
        name="inproj",
    )(x, mod.reshape(bsz, 1, d3), norm_g.reshape(1, d), w_packed, mu.reshape(1, n_shift),
      kvg.reshape(1, -1))


def _rwkv_kernel(sh_ref, vec_ref, w2_ref, a2_ref, ones_ref, out_ref, st_ref,
                 *, width, p_chain, p_inv, p_state):
    j = pl.program_id(1)
    L = CHUNK
    n_pairs = width // LANES

    @pl.when(j == 0)
    def _():
        st_ref[...] = jnp.zeros_like(st_ref)

    blk = sh_ref[0]
    pr = blk[:, 0:width]
    pk = blk[:, width:2 * width]
    pv = blk[:, 2 * width:3 * width]
    lora = blk[:, 3 * width:3 * width + LANES]
    lane = lax.broadcasted_iota(I32, (1, LANES), 1)
    lo_half = lane < HEAD_DIM
    t_in = jnp.where(lo_half, jnp.tanh(lora), lora)
    dec_in = _mm(t_in, w2_ref[...], 3)
    a_in = _mm(t_in, a2_ref[...], 3)

    w0 = vec_ref[0:1, :]
    a0 = vec_ref[1:2, :]
    k_k = vec_ref[2:3, :]
    k_a = vec_ref[3:4, :]
    r_k = vec_ref[4:5, :]
    lnx_g = vec_ref[5:6, :]
    lnx_b = vec_ref[6:7, :]
    ones_bd = ones_ref[...]

    z = -(w0 + dec_in)
    sp = jnp.maximum(z, 0.0) + jnp.log1p(jnp.exp(-jnp.abs(z)))
    logw = -jnp.exp(-sp - 0.5)
    a = _sigmoid(a0 + a_in)
    kk0 = pk * k_k
    ss = _mm_exact_rhs(kk0 * kk0, ones_bd, 3)
    kk = kk0 / jnp.maximum(jnp.sqrt(ss), 1e-12)
    k_eff = pk * (1.0 + (a - 1.0) * k_a)
    a_vec = -kk
    b_vec = kk * a

    tb = blk.shape[0]
    n_chunks = tb // L
    r_i = lax.broadcasted_iota(I32, (tb, tb), 0)
    c_i = lax.broadcasted_iota(I32, (tb, tb), 1)
    tril = jnp.where(((r_i // L) == (c_i // L)) & (c_i <= r_i), 1.0, 0.0).astype(BF16)
    cl = None
    for part in _split(logw, 3):
        t = _dot(tril, part)
        cl = t if cl is None else cl + t
    cl_last = jnp.concatenate(
        [jnp.broadcast_to(cl[c * L + L - 1:c * L + L, :], (L, width)) for c in range(n_chunks)], axis=0)
    rt = pr * jnp.exp(cl)
    at = a_vec * jnp.exp(cl - logw)
    e_inv = jnp.exp(-cl)
    bt = b_vec * e_inv
    kt = k_eff * e_inv
    e_l = jnp.exp(cl_last - cl)
    bl = b_vec * e_l
    kl = k_eff * e_l
    p_last = jnp.exp(cl_last)

    rr = lax.broadcasted_iota(I32, (LANES, LANES), 0)
    cc = lax.broadcasted_iota(I32, (LANES, LANES), 1)
    same = (rr // L) == (cc // L)
    strict = same & ((cc % L) < (rr % L))
    incl = same & ((cc % L) <= (rr % L))
    eye = rr == cc
    eye_f = jnp.where(eye, 1.0, 0.0)
    rr2 = lax.broadcasted_iota(I32, (LANES, 2 * LANES), 0)
    cc2 = lax.broadcasted_iota(I32, (LANES, 2 * LANES), 1)
    incl2 = ((rr2 // L) == ((cc2 % LANES) // L)) & ((cc2 % L) <= (rr2 % L))

    def stack(x):
        return jnp.concatenate([jnp.where(lo_half, x, 0.0), jnp.where(lo_half, 0.0, x)], axis=0)

    chains = [(c, p) for c in range(n_chunks) for p in range(n_pairs)]

    def piece(x, c, p):
        return x[c * L:(c + 1) * L, p * LANES:(p + 1) * LANES]

    atm, rtm, vm, blkl, a_ab, a_ak, a_rbk = {}, {}, {}, {}, {}, {}, {}
    for ch in chains:
        atm[ch], rtm[ch], vm[ch] = stack(piece(at, *ch)), stack(piece(rt, *ch)), stack(piece(pv, *ch))
        btm, ktm = stack(piece(bt, *ch)), stack(piece(kt, *ch))
        blkl[ch] = jnp.concatenate([stack(piece(bl, *ch)), stack(piece(kl, *ch))], axis=0).astype(BF16)
        g = _mm(jnp.concatenate([atm[ch], rtm[ch]], axis=0), jnp.concatenate([btm, ktm], axis=0),
                p_chain, _NT)
        a_ab[ch] = jnp.where(strict, g[0:LANES, 0:LANES], 0.0)
        a_ak[ch] = jnp.where(strict, g[0:LANES, LANES:], 0.0)
        a_rbk[ch] = jnp.where(incl2, g[LANES:, :], 0.0)
    tinv = {ch: eye_f + a_ab[ch] for ch in chains}
    qn = dict(a_ab)
    sq = 1
    while 2 * sq < L:
        for ch in chains:
            qn[ch] = _mm(qn[ch], qn[ch], p_inv)
        for ch in chains:
            tinv[ch] = tinv[ch] + _mm(tinv[ch], qn[ch], p_inv)
        sq *= 2
    akv = {ch: _mm(a_ak[ch], vm[ch], p_chain) for ch in chains}
    wz = {ch: _mm(tinv[ch], jnp.concatenate([atm[ch], akv[ch]], axis=1), p_inv) for ch in chains}
    wm = {ch: wz[ch][:, 0:LANES] for ch in chains}
    zv = {ch: jnp.concatenate([wz[ch][:, LANES:], vm[ch]], axis=0) for ch in chains}
    rqm = {ch: rtm[ch] + _mm(a_rbk[ch][:, 0:LANES], wm[ch], p_chain) for ch in chains}
    y0m = {ch: _mm(a_rbk[ch], zv[ch], p_chain) for ch in chains}
    mt = {ch: jnp.where(eye, piece(p_last, *ch)[0:1], 0.0)
          + _mm(blkl[ch][0:LANES], wm[ch], p_state, _TN) for ch in chains}
    ct = {ch: _mm(blkl[ch], zv[ch], p_state, _TN) for ch in chains}

    st = {p: st_ref[p] for p in range(n_pairs)}
    y_rows = []
    for c in range(n_chunks):
        ys = []
        for p in range(n_pairs):
            ym = _mm(rqm[(c, p)], st[p], p_state) + y0m[(c, p)]
            ys.append(ym[0:L] + ym[L:])
            st[p] = _mm(mt[(c, p)], st[p], p_state) + ct[(c, p)]
        y_rows.append(jnp.concatenate(ys, axis=1))
    for p in range(n_pairs):
        st_ref[p] = st[p]
    y = jnp.concatenate(y_rows, axis=0)

    inv_n = 1.0 / HEAD_DIM
    mu = _mm_exact_rhs(y, ones_bd, 3) * inv_n
    dlt = y - mu
    var = _mm_exact_rhs(dlt * dlt, ones_bd, 3) * inv_n
    yn = dlt * lax.rsqrt(var + GN_EPS) * lnx_g + lnx_b
    bonus = _mm_exact_rhs(pr * k_eff * r_k, ones_bd, 3) * pv
    out_ref[0] = yn + bonus


def _rwkv_call(sh, vecs, w2p, a2p, ones_bd, *, width, tb, p_chain=1, p_inv=1, p_state=1):
    bsz, s, n_shift = sh.shape
    n_pairs = width // LANES
    const = lambda shape: pl.BlockSpec(shape, lambda b, j: tuple(0 for _ in shape))
    return pl.pallas_call(
        functools.partial(_rwkv_kernel, width=width, p_chain=p_chain, p_inv=p_inv, p_state=p_state),
        grid=(bsz, s // tb),
        in_specs=[
            pl.BlockSpec((1, tb, n_shift), lambda b, j: (b, j, 0)),
            const(vecs.shape),
            const(w2p.shape),
            const(a2p.shape),
            const(ones_bd.shape),
        ],
        out_specs=pl.BlockSpec((1, tb, width), lambda b, j: (b, j, 0)),
        out_shape=jax.ShapeDtypeStruct((bsz, s, width), F32),
        scratch_shapes=[pltpu.VMEM((n_pairs, LANES, LANES), F32)],
        compiler_params=pltpu.CompilerParams(
            dimension_semantics=("arbitrary", "arbitrary"),
            vmem_limit_bytes=VMEM_LIMIT_BYTES),
        name="rwkv7_scan",
    )(sh, vecs, w2p, a2p, ones_bd)


def _dsa_kernel(q_ref, qi_ref, wi_ref, ki_ref, lat_ref, wuk_ref, wuv_ref, bias_ref, far_ref, out_ref,
                key_ref, acc_ref, m_ref, l_ref, mt_ref,
                *, topk, kc, n_heads, seq, scale):
    qb = pl.program_id(1)
    QB = Q_BLOCK
    NEAR = 2 * QB
    q_end = (qb + 1) * QB
    n_idx = (jnp.maximum(q_end, NEAR) + kc - 1) // kc
    lane = lax.broadcasted_iota(I32, (1, LANES), 1)
    lo_half = lane < HEAD_DIM
    tpos = qb * QB + lax.broadcasted_iota(I32, (QB, 1), 0)

    qi = qi_ref[0]
    wi = wi_ref[0]
    parts = []
    for h in range(n_heads):
        pair = qi[:, (h // 2) * LANES:(h // 2 + 1) * LANES]
        parts.append(jnp.where(lo_half, pair, 0.0) if h % 2 == 0 else jnp.where(lo_half, 0.0, pair))
    qis = jnp.concatenate(parts, axis=0).astype(BF16)
    wis = [wi[:, h:h + 1] for h in range(n_heads)]

    def idx_body(c, carry):
        ks = pl.multiple_of(c * kc, kc)
        kib = ki_ref[0, pl.ds(ks, kc), :].astype(BF16)
        s_all = jnp.maximum(_dot(qis, kib, _NT), 0.0)
        score = wis[0] * s_all[0:QB]
        for h in range(1, n_heads):
            score = score + wis[h] * s_all[h * QB:(h + 1) * QB]
        score = score + 0.0
        bits = pltpu.bitcast(score, I32)
        key = bits ^ ((bits >> 31) & INT_MAX)
        spos = ks + lax.broadcasted_iota(I32, (1, kc), 1)
        key_ref[:, pl.ds(ks, kc)] = jnp.where(spos <= tpos, key, INT_MIN)
        return carry

    lax.fori_loop(0, n_idx, idx_body, 0)

    def count(pred):
        def body(c, acc):
            ks = pl.multiple_of(c * kc, kc)
            kch = key_ref[:, pl.ds(ks, kc)]
            col = ks + lax.broadcasted_iota(I32, (1, kc), 1)
            hit = jnp.where(pred(kch, col), 1.0, 0.0)
            for t in range(kc // LANES):
                acc = acc + hit[:, t * LANES:(t + 1) * LANES]
            return acc
        acc = lax.fori_loop(0, n_idx, body, jnp.zeros((QB, LANES), F32))
        return jnp.sum(acc, axis=1, keepdims=True)

    kf = float(topk)

    def bit_body(i, thr_u):
        cand_u = thr_u | lax.shift_left(jnp.int32(1), 31 - i)
        cand = cand_u ^ INT_MIN
        cnt = count(lambda kch, col: kch >= cand)
        return jnp.where(cnt >= kf, cand_u, thr_u)

    thr = lax.fori_loop(0, 32, bit_body, jnp.zeros((QB, 1), I32)) ^ INT_MIN

    cnt_gt = count(lambda kch, col: kch > thr)
    cnt_ge = count(lambda kch, col: kch >= thr)
    need = kf - cnt_gt
    excess = jnp.where((cnt_ge > kf) & (thr > INT_MIN), 1.0, 0.0)
    mt_ref[...] = jnp.full((QB, 1), INT_MAX, I32)

    @pl.when(jnp.max(excess) > 0.0)
    def _():
        nbits = max(1, (seq - 1).bit_length())

        def tie_body(i, mcur):
            cand = mcur | lax.shift_left(jnp.int32(1), nbits - 1 - i)
            cnt = count(lambda kch, col: (kch == thr) & (col < cand))
            return jnp.where(cnt < need, cand, mcur)

        mt_ref[...] = lax.fori_loop(0, nbits, tie_body, jnp.zeros((QB, 1), I32))

    mtie = mt_ref[...]

    qabs = _dot(q_ref[0].astype(BF16), wuk_ref[...]) * scale
    qs = jnp.concatenate([qabs[:, h * LANES:(h + 1) * LANES] for h in range(n_heads)],
                         axis=0).astype(BF16)
    m_ref[...] = jnp.full(m_ref.shape, NEG_INIT, F32)
    l_ref[...] = jnp.zeros(l_ref.shape, F32)
    acc_ref[...] = jnp.zeros(acc_ref.shape, F32)

    def att_chunk(ks, width, bias_of_head, col_limit):
        latb = lat_ref[0, pl.ds(ks, width), :].astype(BF16)
        logits = _dot(qs, latb, _NT)
        kch = key_ref[:, pl.ds(ks, width)]
        col = ks + lax.broadcasted_iota(I32, (1, width), 1)
        sel = ((kch > thr) | ((kch == thr) & (col <= mtie))) & (kch > INT_MIN)
        if col_limit is not None:
            sel = sel & (col < col_limit)
        ps = []
        for h in range(n_heads):
            rows = slice(h * QB, (h + 1) * QB)
            lg = jnp.where(sel, logits[rows] + bias_of_head(h), NEG_MASK)
            m_old = m_ref[rows]
            m_new = jnp.maximum(m_old, jnp.max(lg, axis=1, keepdims=True))
            p = jnp.exp(lg - m_new)
            alpha = jnp.exp(m_old - m_new)
            l_ref[rows] = alpha * l_ref[rows] + jnp.sum(p, axis=1, keepdims=True)
            acc_ref[rows] = alpha * acc_ref[rows]
            m_ref[rows] = m_new
            ps.append(p.astype(BF16))
        acc_ref[...] += _dot(jnp.concatenate(ps, axis=0), latb)

    far_end = jnp.maximum(qb - 1, 0) * QB
    n_far = (far_end + kc - 1) // kc

    def far_body(c, carry):
        att_chunk(pl.multiple_of(c * kc, kc), kc, lambda h: far_ref[h], far_end)
        return carry

    lax.fori_loop(0, n_far, far_body, 0)

    near_start = pl.multiple_of(far_end, QB)
    tile_id = jnp.where(qb == 0, 1, 0)
    att_chunk(near_start, NEAR, lambda h: bias_ref[tile_id, h], None)

    o = acc_ref[...] / l_ref[...]
    o_wide = jnp.concatenate([o[h * QB:(h + 1) * QB] for h in range(n_heads)], axis=1).astype(BF16)
    out_ref[0] = _dot(o_wide, wuv_ref[...])


def _dsa_call(q, qi, wi, ki2, lat, wuk_bd, wuv_bd, bias_near, bias_far, *, topk, kc):
    bsz, s, aw = q.shape
    n_heads = aw // HEAD_DIM
    c_lat = lat.shape[-1]
    nb = s // Q_BLOCK
    const = lambda shape: pl.BlockSpec(shape, lambda b, j: tuple(0 for _ in shape))
    blk = lambda w: pl.BlockSpec((1, Q_BLOCK, w), lambda b, j: (b, j, 0))
    full = lambda w: pl.BlockSpec((1, s, w), lambda b, j: (b, 0, 0))
    return pl.pallas_call(
        functools.partial(_dsa_kernel, topk=topk, kc=kc, n_heads=n_heads, seq=s,
                          scale=float(HEAD_DIM) ** -0.5),
        grid=(bsz, nb),
        in_specs=[
            blk(aw), blk(qi.shape[-1]), blk(LANES), full(LANES), full(c_lat),
            const(wuk_bd.shape), const(wuv_bd.shape), const(bias_near.shape), const(bias_far.shape),
        ],
        out_specs=blk(aw),
        out_shape=jax.ShapeDtypeStruct((bsz, s, aw), F32),
        scratch_shapes=[
            pltpu.VMEM((Q_BLOCK, s), I32),
            pltpu.VMEM((n_heads * Q_BLOCK, c_lat), F32),
            pltpu.VMEM((n_heads * Q_BLOCK, 1), F32),
            pltpu.VMEM((n_heads * Q_BLOCK, 1), F32),
            pltpu.VMEM((Q_BLOCK, 1), I32),
        ],
        compiler_params=pltpu.CompilerParams(
            dimension_semantics=("arbitrary", "arbitrary"),
            vmem_limit_bytes=VMEM_LIMIT_BYTES),
        name="dsa_attention",
    )(q, qi, wi, ki2, lat, wuk_bd, wuv_bd, bias_near, bias_far)


def _out_kernel(ya_ref, za_ref, yb_ref, zb_ref, ga_ref, gb_ref, x_ref, mod_ref,
                wpa_ref, wpb_ref, wo_ref, fg_ref, o_ref, *, d, final):
    za = za_ref[0]
    zb = zb_ref[0]
    ua = (ya_ref[0] * (za * _sigmoid(za))).astype(BF16)
    ub = (yb_ref[0] * (zb * _sigmoid(zb))).astype(BF16)
    br_a = _dot(ua, wpa_ref[...])
    br_b = _dot(ub, wpb_ref[...])
    merged = _sigmoid(ga_ref[0]) * br_a + _sigmoid(gb_ref[0]) * br_b
    gate = mod_ref[0, :, 2 * d:3 * d]
    xo = x_ref[0] + gate * _dot(merged.astype(BF16), wo_ref[...])
    if final:
        xo = xo * lax.rsqrt(jnp.mean(xo * xo, axis=-1, keepdims=True) + NORM_EPS) * fg_ref[...]
    o_ref[0] = xo


def _out_call(ya, za, yb, zb, ga, gb, x, mod, wpa, wpb, wo, final_g, *, final, tm):
    bsz, s, d = x.shape
    d3 = mod.shape[-1]
    row = lambda w: pl.BlockSpec((1, tm, w), lambda b, j: (b, j, 0))
    const = lambda shape: pl.BlockSpec(shape, lambda b, j: tuple(0 for _ in shape))
    return pl.pallas_call(
        functools.partial(_out_kernel, d=d, final=final),
        grid=(bsz, s // tm),
        in_specs=[
            row(ya.shape[-1]), row(za.shape[-1]), row(yb.shape[-1]), row(zb.shape[-1]),
            row(d), row(d), row(d),
            pl.BlockSpec((1, 1, d3), lambda b, j: (b, 0, 0)),
            const(wpa.shape), const(wpb.shape), const(wo.shape), const((1, d)),
        ],
        out_specs=row(d),
        out_shape=jax.ShapeDtypeStruct((bsz, s, d), F32),
        compiler_params=pltpu.CompilerParams(
            dimension_semantics=("arbitrary", "arbitrary"),
            vmem_limit_bytes=VMEM_LIMIT_BYTES),
        name="gated_out",
    )(ya, za, yb, zb, ga, gb, x, mod.reshape(bsz, 1, d3), wpa, wpb, wo, final_g.reshape(1, d))


def _t5_bucket(dist, num_buckets):
    max_exact = num_buckets // 2
    is_small = dist < max_exact
    dd = jnp.maximum(dist, 1).astype(F32)
    large = max_exact + (jnp.log(dd / max_exact) / math.log(MAX_DISTANCE / max_exact)
                         * (num_buckets - max_exact)).astype(I32)
    large = jnp.minimum(large, num_buckets - 1)
    return jnp.where(is_small, dist, large)


def _bias_tiles(rel_bias):
    nbk = rel_bias.shape[0]
    i = jnp.arange(Q_BLOCK, dtype=I32)[:, None]
    jj = jnp.arange(2 * Q_BLOCK, dtype=I32)[None, :]
    tiles = []
    for base in (Q_BLOCK, 0):
        bucket = _t5_bucket(jnp.maximum(base + i - jj, 0), nbk)
        onehot = (bucket[None, :, :] == jnp.arange(nbk, dtype=I32)[:, None, None]).astype(F32)
        tiles.append(jnp.einsum('nh,nij->hij', rel_bias, onehot,
                                precision=lax.Precision.HIGHEST))
    far = rel_bias[_t5_bucket(jnp.full((1, 1), MAX_DISTANCE, I32), nbk)]
    return jnp.stack(tiles, axis=0), jnp.moveaxis(far, -1, 0)


def _block_diag_heads(w, rows_per_head, cols_per_head):
    n = w.shape[0]
    eye = jnp.eye(n, dtype=w.dtype)
    return jnp.einsum('hrc,hg->hrgc', w, eye).reshape(n * rows_per_head, n * cols_per_head)


def kernel(x, c, ada_w, ada_b, norm_g, w_in, shift_mu, w0, w2, a0, a2, k_k, k_a, r_k, lnx_g, lnx_b,
           kv_norm_g, w_uk, w_uv, w_pa, w_pb, w_o, rel_bias, final_g):
    bsz, s, d = x.shape
    depth = w_in.shape[0]
    rw = w0.shape[-1]
    c_lat, n_heads, dh = w_uk.shape[1:]
    aw = n_heads * dh
    iw = IDX_HEADS * IDX_HEAD_DIM
    n_shift = 3 * rw + DECAY_LORA + ICLR_LORA
    topk = min(TOPK_MAX, s // 4)
    assert DECAY_LORA + ICLR_LORA == LANES and dh == HEAD_DIM and IDX_HEAD_DIM == HEAD_DIM
    assert s % (2 * Q_BLOCK) == 0 and rw % LANES == 0

    sizes = (rw, rw, rw, DECAY_LORA, ICLR_LORA, rw, aw, c_lat, aw, iw, IDX_HEAD_DIM, IDX_HEADS, d, d)
    offs = [0]
    for sz in sizes:
        offs.append(offs[-1] + sz)
    col = lambda i: slice(offs[i], offs[i + 1])
    widths = (rw, aw, c_lat, aw, iw, LANES, LANES, d, d)

    mod = _mod_call(c, ada_w, ada_b)
    bias_near, bias_far = _bias_tiles(rel_bias)
    ones_bd = _block_diag_heads(jnp.ones((rw // HEAD_DIM, HEAD_DIM, HEAD_DIM), BF16), HEAD_DIM, HEAD_DIM)
    kc = 512 if s % 512 == 0 else 2 * Q_BLOCK
    tm_in = 256
    tm_out = 512 if s % 512 == 0 else 256

    for l in range(depth):
        w = w_in[l]
        wi_pad = jnp.zeros((d, LANES - IDX_HEADS), w.dtype)
        w_packed = jnp.concatenate(
            [w[:, 0:n_shift], w[:, col(5)], w[:, col(6)], w[:, col(7)], w[:, col(8)], w[:, col(9)],
             w[:, col(10)], w[:, col(10)], w[:, col(11)], wi_pad, w[:, col(12)], w[:, col(13)]],
            axis=1).astype(BF16)
        sh, za, q, lat, zb, qi, ki2, wi, ga, gb = _inproj_call(
            x, mod[l], norm_g[l], w_packed, shift_mu[l], kv_norm_g[l],
            n_shift=n_shift, widths=widths, tm=tm_in)

        vecs = jnp.stack([w0[l], a0[l], k_k[l], k_a[l], r_k[l].reshape(-1), lnx_g[l], lnx_b[l],
                          jnp.zeros((rw,), F32)], axis=0)
        w2p = jnp.concatenate([w2[l], jnp.zeros((ICLR_LORA, rw), F32)], axis=0)
        a2p = jnp.concatenate([jnp.zeros((DECAY_LORA, rw), F32), a2[l]], axis=0)
        ya = _rwkv_call(sh, vecs, w2p, a2p, ones_bd, width=rw, tb=4 * CHUNK)

        wuk_bd = _block_diag_heads(jnp.transpose(w_uk[l], (1, 2, 0)), dh, c_lat).astype(BF16)
        wuv_bd = _block_diag_heads(jnp.transpose(w_uv[l], (1, 0, 2)), c_lat, dh).astype(BF16)
        yb = _dsa_call(q, qi, wi, ki2, lat, wuk_bd, wuv_bd, bias_near, bias_far, topk=topk, kc=kc)

        x = _out_call(ya, za, yb, zb, ga, gb, x, mod[l], w_pa[l].astype(BF16), w_pb[l].astype(BF16),
                      w_o[l].astype(BF16), final_g, final=(l == depth - 1), tm=tm_out)
    return x
```

```python
import functools
import math

import jax
import jax.numpy as jnp
from jax import lax
from jax.experimental import pallas as pl
from jax.experimental.pallas import tpu as pltpu

F32 = jnp.float32
BF16 = jnp.bfloat16
I32 = jnp.int32

DECAY_LORA = 64
ICLR_LORA = 64
GN_EPS = 64e-5
IDX_HEADS = 8
IDX_HEAD_DIM = 64
TOPK_MAX = 256
Q_BLOCK = 128
MAX_DISTANCE = 128
NORM_EPS = 1e-6

LANES = 128
VMEM_LIMIT_BYTES = 56 * 1024 * 1024

HEAD_DIM = 64
CHUNK = 64
INT_MIN = -(2 ** 31)
INT_MAX = 2 ** 31 - 1
NEG_MASK = -2e30
NEG_INIT = -1e30
LOG2E = 1.4426950408889634


def _sigmoid(x):
    return 1.0 / (1.0 + jnp.exp(-x))


def _dot(a, b, dims=(((1,), (0,)), ((), ()))):
    return lax.dot_general(a, b, dims, preferred_element_type=F32)


_NN = (((1,), (0,)), ((), ()))
_NT = (((1,), (1,)), ((), ()))
_TN = (((0,), (0,)), ((), ()))


def _split(x, n):
    parts = []
    rem = x
    for i in range(n):
        p = rem.astype(BF16)
        parts.append(p)
        if i + 1 < n:
            rem = rem - p.astype(F32)
    return parts


def _mm(a, b, passes, dims=_NN):
    if passes == 1:
        return _dot(a.astype(BF16), b.astype(BF16), dims)
    if passes == 3:
        a1, a2 = _split(a, 2)
        b1, b2 = _split(b, 2)
        return _dot(a1, b1, dims) + (_dot(a1, b2, dims) + _dot(a2, b1, dims))
    a1, a2, a3 = _split(a, 3)
    b1, b2, b3 = _split(b, 3)
    hi = _dot(a1, b1, dims)
    mid = _dot(a1, b2, dims) + _dot(a2, b1, dims)
    lo = _dot(a2, b2, dims) + (_dot(a1, b3, dims) + _dot(a3, b1, dims))
    return hi + (mid + lo)


def _mm_exact_rhs(a, b_bf16, n):
    out = None
    for p in _split(a, n):
        t = _dot(p, b_bf16)
        out = t if out is None else out + t
    return out


def _mod_kernel(c_ref, w_ref, b_ref, o_ref):
    c = c_ref[...]
    ca = c * _sigmoid(c)
    o_ref[0] = _mm(ca, w_ref[0], 6) + b_ref[0]


def _mod_call(c, ada_w, ada_b):
    depth, d, d3 = ada_w.shape
    bsz = c.shape[0]
    tn = d
    return pl.pallas_call(
        _mod_kernel,
        grid=(depth, d3 // tn),
        in_specs=[
            pl.BlockSpec((bsz, d), lambda l, n: (0, 0)),
            pl.BlockSpec((1, d, tn), lambda l, n: (l, 0, n)),
            pl.BlockSpec((1, 1, tn), lambda l, n: (l, 0, n)),
        ],
        out_specs=pl.BlockSpec((1, bsz, tn), lambda l, n: (l, 0, n)),
        out_shape=jax.ShapeDtypeStruct((depth, bsz, d3), F32),
        compiler_params=pltpu.CompilerParams(
            dimension_semantics=("arbitrary", "arbitrary"),
            vmem_limit_bytes=VMEM_LIMIT_BYTES),
        name="adaln_mod",
    )(c, ada_w, ada_b.reshape(depth, 1, d3))


def _inproj_kernel(x_ref, mod_ref, g_ref, w_ref, mu_ref, kvg_ref,
                   sh_ref, za_ref, q_ref, lat_ref, zb_ref, qi_ref, ki_ref, wi_ref, ga_ref, gb_ref,
                   carry_ref, *, d, n_shift, widths):
    j = pl.program_id(1)
    xb = x_ref[0]
    tm = xb.shape[0]
    shift = mod_ref[0, :, 0:d]
    scale = mod_ref[0, :, d:2 * d]
    ms = jnp.mean(xb * xb, axis=-1, keepdims=True)
    h = xb * lax.rsqrt(ms + NORM_EPS) * g_ref[...]
    h = h * (1.0 + scale) + shift
    hb = h.astype(BF16)

    @pl.when(j == 0)
    def _():
        carry_ref[...] = jnp.zeros_like(carry_ref)

    ps = _dot(hb, w_ref[:, 0:n_shift])
    prev = pltpu.roll(ps, 1, 0)
    row = lax.broadcasted_iota(I32, (tm, 1), 0)
    prev = jnp.where(row == 0, carry_ref[...], prev)
    carry_ref[...] = ps[tm - 1:tm, :]
    sh_ref[0] = ps + (prev - ps) * mu_ref[...]

    off = n_shift
    outs = (za_ref, q_ref, lat_ref, zb_ref, qi_ref, ki_ref, wi_ref, ga_ref, gb_ref)
    for o_ref, wd in zip(outs, widths):
        p = _dot(hb, w_ref[:, off:off + wd])
        if o_ref is lat_ref:
            p = p * lax.rsqrt(jnp.mean(p * p, axis=-1, keepdims=True) + NORM_EPS) * kvg_ref[...]
        o_ref[0] = p
        off += wd


def _inproj_call(x, mod, norm_g, w_packed, mu, kvg, *, n_shift, widths, tm):
    bsz, s, d = x.shape
    npk = w_packed.shape[1]
    d3 = mod.shape[-1]
    row_spec = lambda w: pl.BlockSpec((1, tm, w), lambda b, j: (b, j, 0))
    const = lambda shape: pl.BlockSpec(shape, lambda b, j: tuple(0 for _ in shape))
    out_widths = (n_shift,) + tuple(widths)
    return pl.pallas_call(
        functools.partial(_inproj_kernel, d=d, n_shift=n_shift, widths=tuple(widths)),
        grid=(bsz, s // tm),
        in_specs=[
            row_spec(d),
            pl.BlockSpec((1, 1, d3), lambda b, j: (b, 0, 0)),
            const((1, d)),
            pl.BlockSpec((d, npk), lambda b, j: (0, 0), pipeline_mode=pl.Buffered(1)),
            const((1, n_shift)),
            const((1, widths[2])),
        ],
        out_specs=[row_spec(w) for w in out_widths],
        out_shape=[jax.ShapeDtypeStruct((bsz, s, w), F32) for w in out_widths],
        scratch_shapes=[pltpu.VMEM((1, n_shift), F32)],
        compiler_params=pltpu.CompilerParams(
            dimension_semantics=("arbitrary", "arbitrary"),
            vmem_limit_bytes=VMEM_LIMIT_BYTES),
        name="inproj",
    )(x, mod.reshape(bsz, 1, d3), norm_g.reshape(1, d), w_packed, mu.reshape(1, n_shift),
      kvg.reshape(1, -1))


def _rwkv_kernel(sh_ref, vec_ref, w2_ref, a2_ref, ones_ref, out_ref, st_ref,
                 *, width, p_chain, p_inv, p_state):
    j = pl.program_id(1)
    L = CHUNK
    n_pairs = width // LANES

    @pl.when(j == 0)
    def _():
        st_ref[...] = jnp.zeros_like(st_ref)

    blk = sh_ref[0]
    pr = blk[:, 0:width]
    pk = blk[:, width:2 * width]
    pv = blk[:, 2 * width:3 * width]
    lora = blk[:, 3 * width:3 * width + LANES]
    lane = lax.broadcasted_iota(I32, (1, LANES), 1)
    lo_half = lane < HEAD_DIM
    t_in = jnp.where(lo_half, jnp.tanh(lora), lora)
    dec_in = _mm(t_in, w2_ref[...], 3)
    a_in = _mm(t_in, a2_ref[...], 3)

    w0 = vec_ref[0:1, :]
    a0 = vec_ref[1:2, :]
    k_k = vec_ref[2:3, :]
    k_a = vec_ref[3:4, :]
    r_k = vec_ref[4:5, :]
    lnx_g = vec_ref[5:6, :]
    lnx_b = vec_ref[6:7, :]
    ones_bd = ones_ref[...]

    z = -(w0 + dec_in)
    sp = jnp.maximum(z, 0.0) + jnp.log1p(jnp.exp(-jnp.abs(z)))
    logw = -jnp.exp(-sp - 0.5)
    a = _sigmoid(a0 + a_in)
    kk0 = pk * k_k
    ss = _mm_exact_rhs(kk0 * kk0, ones_bd, 3)
    kk = kk0 / jnp.maximum(jnp.sqrt(ss), 1e-12)
    k_eff = pk * (1.0 + (a - 1.0) * k_a)
    a_vec = -kk
    b_vec = kk * a

    tb = blk.shape[0]
    n_chunks = tb // L
    r_i = lax.broadcasted_iota(I32, (tb, tb), 0)
    c_i = lax.broadcasted_iota(I32, (tb, tb), 1)
    tril = jnp.where(((r_i // L) == (c_i // L)) & (c_i <= r_i), 1.0, 0.0).astype(BF16)
    cl = None
    for part in _split(logw, 3):
        t = _dot(tril, part)
        cl = t if cl is None else cl + t
    cl_last = jnp.concatenate(
        [jnp.broadcast_to(cl[c * L + L - 1:c * L + L, :], (L, width)) for c in range(n_chunks)], axis=0)
    rt = pr * jnp.exp(cl)
    at = a_vec * jnp.exp(cl - logw)
    e_inv = jnp.exp(-cl)
    bt = b_vec * e_inv
    kt = k_eff * e_inv
    e_l = jnp.exp(cl_last - cl)
    bl = b_vec * e_l
    kl = k_eff * e_l
    p_last = jnp.exp(cl_last)

    rr = lax.broadcasted_iota(I32, (LANES, LANES), 0)
    cc = lax.broadcasted_iota(I32, (LANES, LANES), 1)
    same = (rr // L) == (cc // L)
    strict = same & ((cc % L) < (rr % L))
    incl = same & ((cc % L) <= (rr % L))
    eye = rr == cc
    eye_f = jnp.where(eye, 1.0, 0.0)
    rr2 = lax.broadcasted_iota(I32, (LANES, 2 * LANES), 0)
    cc2 = lax.broadcasted_iota(I32, (LANES, 2 * LANES), 1)
    incl2 = ((rr2 // L) == ((cc2 % LANES) // L)) & ((cc2 % L) <= (rr2 % L))

    def stack(x):
        return jnp.concatenate([jnp.where(lo_half, x, 0.0), jnp.where(lo_half, 0.0, x)], axis=0)

    chains = [(c, p) for c in range(n_chunks) for p in range(n_pairs)]

    def piece(x, c, p):
        return x[c * L:(c + 1) * L, p * LANES:(p + 1) * LANES]

    atm, rtm, vm, blkl, a_ab, a_ak, a_rbk = {}, {}, {}, {}, {}, {}, {}
    for ch in chains:
        atm[ch], rtm[ch], vm[ch] = stack(piece(at, *ch)), stack(piece(rt, *ch)), stack(piece(pv, *ch))
        btm, ktm = stack(piece(bt, *ch)), stack(piece(kt, *ch))
        blkl[ch] = jnp.concatenate([stack(piece(bl, *ch)), stack(piece(kl, *ch))], axis=0).astype(BF16)
        g = _mm(jnp.concatenate([atm[ch], rtm[ch]], axis=0), jnp.concatenate([btm, ktm], axis=0),
                p_chain, _NT)
        a_ab[ch] = jnp.where(strict, g[0:LANES, 0:LANES], 0.0)
        a_ak[ch] = jnp.where(strict, g[0:LANES, LANES:], 0.0)
        a_rbk[ch] = jnp.where(incl2, g[LANES:, :], 0.0)
    tinv = {ch: eye_f + a_ab[ch] for ch in chains}
    qn = dict(a_ab)
    sq = 1
    while 2 * sq < L:
        for ch in chains:
            qn[ch] = _mm(qn[ch], qn[ch], p_inv)
        for ch in chains:
            tinv[ch] = tinv[ch] + _mm(tinv[ch], qn[ch], p_inv)
        sq *= 2
    akv = {ch: _mm(a_ak[ch], vm[ch], p_chain) for ch in chains}
    wz = {ch: _mm(tinv[ch], jnp.concatenate([atm[ch], akv[ch]], axis=1), p_inv) for ch in chains}
    wm = {ch: wz[ch][:, 0:LANES] for ch in chains}
    zv = {ch: jnp.concatenate([wz[ch][:, LANES:], vm[ch]], axis=0) for ch in chains}
    rqm = {ch: rtm[ch] + _mm(a_rbk[ch][:, 0:LANES], wm[ch], p_chain) for ch in chains}
    y0m = {ch: _mm(a_rbk[ch], zv[ch], p_chain) for ch in chains}
    mt = {ch: jnp.where(eye, piece(p_last, *ch)[0:1], 0.0)
          + _mm(blkl[ch][0:LANES], wm[ch], p_state, _TN) for ch in chains}
    ct = {ch: _mm(blkl[ch], zv[ch], p_state, _TN) for ch in chains}

    st = {p: st_ref[p] for p in range(n_pairs)}
    y_rows = []
    for c in range(n_chunks):
        ys = []
        for p in range(n_pairs):
            ym = _mm(rqm[(c, p)], st[p], p_state) + y0m[(c, p)]
            ys.append(ym[0:L] + ym[L:])
            st[p] = _mm(mt[(c, p)], st[p], p_state) + ct[(c, p)]
        y_rows.append(jnp.concatenate(ys, axis=1))
    for p in range(n_pairs):
        st_ref[p] = st[p]
    y = jnp.concatenate(y_rows, axis=0)

    inv_n = 1.0 / HEAD_DIM
    mu = _mm_exact_rhs(y, ones_bd, 3) * inv_n
    dlt = y - mu
    var = _mm_exact_rhs(dlt * dlt, ones_bd, 3) * inv_n
    yn = dlt * lax.rsqrt(var + GN_EPS) * lnx_g + lnx_b
    bonus = _mm_exact_rhs(pr * k_eff * r_k, ones_bd, 3) * pv
    out_ref[0] = yn + bonus


def _rwkv_call(sh, vecs, w2p, a2p, ones_bd, *, width, tb, p_chain=1, p_inv=1, p_state=1):
    bsz, s, n_shift = sh.shape
    n_pairs = width // LANES
    const = lambda shape: pl.BlockSpec(shape, lambda b, j: tuple(0 for _ in shape))
    return pl.pallas_call(
        functools.partial(_rwkv_kernel, width=width, p_chain=p_chain, p_inv=p_inv, p_state=p_state),
        grid=(bsz, s // tb),
        in_specs=[
            pl.BlockSpec((1, tb, n_shift), lambda b, j: (b, j, 0)),
            const(vecs.shape),
            const(w2p.shape),
            const(a2p.shape),
            const(ones_bd.shape),
        ],
        out_specs=pl.BlockSpec((1, tb, width), lambda b, j: (b, j, 0)),
        out_shape=jax.ShapeDtypeStruct((bsz, s, width), F32),
        scratch_shapes=[pltpu.VMEM((n_pairs, LANES, LANES), F32)],
        compiler_params=pltpu.CompilerParams(
            dimension_semantics=("arbitrary", "arbitrary"),
            vmem_limit_bytes=VMEM_LIMIT_BYTES),
        name="rwkv7_scan",
    )(sh, vecs, w2p, a2p, ones_bd)


def _dsa_kernel(q_ref, qi_ref, wi_ref, ki_ref, lat_ref, wukt_ref, wuv_ref, bias_ref, far_ref, out_ref,
                key_ref, acc_ref, lg_ref, p_ref, w_ref, madd_ref, mt_ref,
                *, topk, kc, n_heads, seq, scale):
    qb = pl.program_id(1)
    QB = Q_BLOCK
    NEAR = 2 * QB
    HQ = n_heads * QB
    c_lat = lat_ref.shape[-1]
    q_end = (qb + 1) * QB
    n_idx = (jnp.maximum(q_end, NEAR) + kc - 1) // kc
    tpos = qb * QB + lax.broadcasted_iota(I32, (1, QB), 1)

    def head(x, h):
        return x[:, h * QB:(h + 1) * QB]

    qi_t = qi_ref[0].T
    qi_w = jnp.concatenate([qi_t[h * HEAD_DIM:(h + 1) * HEAD_DIM, :] for h in range(n_heads)], axis=1)
    w_ref[...] = jnp.concatenate([qi_w, jnp.zeros_like(qi_w)], axis=0).astype(BF16)
    wi_t = wi_ref[0].T

    def idx_body(c, carry):
        ks = pl.multiple_of(c * kc, kc)
        kib = ki_ref[0, pl.ds(ks, kc), :].astype(BF16)
        score = None
        for g in range(n_heads // 2):
            s_g = jnp.maximum(_dot(kib, w_ref[:, 2 * g * QB:(2 * g + 2) * QB]), 0.0)
            part = wi_t[2 * g:2 * g + 1, :] * s_g[:, 0:QB] + wi_t[2 * g + 1:2 * g + 2, :] * s_g[:, QB:]
            score = part if score is None else score + part
        score = score + 0.0
        bits = pltpu.bitcast(score, I32)
        key = bits ^ ((bits >> 31) & INT_MAX)
        spos = ks + lax.broadcasted_iota(I32, (kc, 1), 0)
        key_ref[pl.ds(ks, kc), :] = jnp.where(spos <= tpos, key, INT_MIN)
        return carry

    lax.fori_loop(0, n_idx, idx_body, 0)

    def count(pred):
        def body(c, acc):
            ks = pl.multiple_of(c * kc, kc)
            kch = key_ref[pl.ds(ks, kc), :]
            row = ks + lax.broadcasted_iota(I32, (kc, 1), 0)
            hit = jnp.where(pred(kch, row), 1.0, 0.0).reshape(kc // 8, 8, QB)
            parts = [hit[i] for i in range(kc // 8)]
            while len(parts) > 1:
                parts = [parts[i] + parts[i + 1] for i in range(0, len(parts), 2)]
            return acc + parts[0]
        acc = lax.fori_loop(0, n_idx, body, jnp.zeros((8, QB), F32))
        return jnp.sum(acc, axis=0, keepdims=True)

    kf = float(topk)

    def bit_body(i, thr_u):
        cand_u = thr_u | lax.shift_left(jnp.int32(1), 31 - i)
        cand = cand_u ^ INT_MIN
        cnt = count(lambda kch, row: kch >= cand)
        return jnp.where(cnt >= kf, cand_u, thr_u)

    thr = lax.fori_loop(0, 32, bit_body, jnp.zeros((1, QB), I32)) ^ INT_MIN

    cnt_gt = count(lambda kch, row: kch > thr)
    cnt_ge = count(lambda kch, row: kch >= thr)
    need = kf - cnt_gt
    excess = jnp.where((cnt_ge > kf) & (thr > INT_MIN), 1.0, 0.0)
    mt_ref[...] = jnp.full((1, QB), INT_MAX, I32)

    @pl.when(jnp.max(excess) > 0.0)
    def _():
        nbits = max(1, (seq - 1).bit_length())

        def tie_body(i, mcur):
            cand = mcur | lax.shift_left(jnp.int32(1), nbits - 1 - i)
            cnt = count(lambda kch, row: (kch == thr) & (row < cand))
            return jnp.where(cnt < need, cand, mcur)

        mt_ref[...] = lax.fori_loop(0, nbits, tie_body, jnp.zeros((1, QB), I32))

    mtie = mt_ref[...]

    def selected(ks, width, hi_limit):
        kch = key_ref[pl.ds(ks, width), :]
        row = ks + lax.broadcasted_iota(I32, (width, 1), 0)
        sel = ((kch > thr) | ((kch == thr) & (row <= mtie))) & (kch > INT_MIN)
        if hi_limit is not None:
            sel = sel & (row < hi_limit)
        return sel

    q_t = q_ref[0].T.astype(BF16)
    qabs_t = _dot(wukt_ref[...], q_t) * (scale * LOG2E)
    w_ref[...] = jnp.concatenate([qabs_t[h * c_lat:(h + 1) * c_lat, :] for h in range(n_heads)],
                                 axis=1).astype(BF16)
    far_end = jnp.maximum(qb - 1, 0) * QB
    n_far = (far_end + kc - 1) // kc
    pair_cols = [slice(2 * g * QB, (2 * g + 2) * QB) for g in range(n_heads // 2)]

    def lat_chunk(ks, width):
        return lat_ref[0, pl.ds(ks, width), :].astype(BF16)

    acc_ref[...] = jnp.zeros(acc_ref.shape, F32)
    p_ref[...] = jnp.zeros(p_ref.shape, BF16)

    @pl.when(n_far > 0)
    def _():
        lg_ref[...] = _dot(lat_chunk(0, kc), w_ref[...])

    def far_body(c, carry):
        ms, ls = carry
        ksp = pl.multiple_of(jnp.maximum(c - 1, 0) * kc, kc)
        latp_t = lat_ref[0, pl.ds(ksp, kc), :].T.astype(BF16)
        ksn = pl.multiple_of(jnp.minimum(c + 1, n_far - 1) * kc, kc)
        latn = lat_chunk(ksn, kc)
        ks = pl.multiple_of(c * kc, kc)
        madd_ref[...] = jnp.where(selected(ks, kc, far_end), 0.0, NEG_MASK)
        new_ms, new_ls = list(ms), list(ls)
        for g, cols in enumerate(pair_cols):
            pv_g = _dot(latp_t, p_ref[:, cols])
            lg_next = _dot(latn, w_ref[:, cols])
            for h in (2 * g, 2 * g + 1):
                hc = slice(h * QB, (h + 1) * QB)
                bfar = far_ref[h]
                lgm = lg_ref[:, hc] + madd_ref[...]
                m_new = jnp.maximum(ms[h], jnp.max(lgm, axis=0, keepdims=True) + bfar)
                p = jnp.exp2(lgm - (m_new - bfar))
                alpha = jnp.exp2(ms[h] - m_new)
                new_ls[h] = alpha * ls[h] + jnp.sum(p, axis=0, keepdims=True)
                new_ms[h] = m_new
                p_ref[:, hc] = p.astype(BF16)
                acc_ref[:, hc] = (acc_ref[:, hc] + pv_g[:, (h - 2 * g) * QB:(h - 2 * g + 1) * QB]) * alpha
            lg_ref[:, cols] = lg_next
        return tuple(new_ms), tuple(new_ls)

    init = (tuple(jnp.full((1, QB), NEG_INIT, F32) for _ in range(n_heads)),
            tuple(jnp.zeros((1, QB), F32) for _ in range(n_heads)))
    ms, ls = lax.fori_loop(0, n_far, far_body, init)
    ks_last = pl.multiple_of(jnp.maximum(n_far - 1, 0) * kc, kc)
    pv = _dot(lat_chunk(ks_last, kc), p_ref[...], _TN)

    near_start = pl.multiple_of(far_end, QB)
    tile_id = jnp.where(qb == 0, 1, 0)
    latn = lat_chunk(near_start, NEAR)
    lgn = _dot(latn, w_ref[...])
    sel = selected(near_start, NEAR, None)
    ps, alphas, l_fin = [], [], []
    for h in range(n_heads):
        lgm = jnp.where(sel, head(lgn, h) + bias_ref[tile_id, h], NEG_MASK)
        m_new = jnp.maximum(ms[h], jnp.max(lgm, axis=0, keepdims=True))
        p = jnp.exp2(lgm - m_new)
        alpha = jnp.exp2(ms[h] - m_new)
        l_fin.append(alpha * ls[h] + jnp.sum(p, axis=0, keepdims=True))
        alphas.append(alpha)
        ps.append(p.astype(BF16))
    pvn = _dot(latn, jnp.concatenate(ps, axis=1), _TN)
    o_parts = []
    for h in range(n_heads):
        o_h = ((acc_ref[:, h * QB:(h + 1) * QB] + head(pv, h)) * alphas[h] + head(pvn, h)) / l_fin[h]
        o_parts.append(o_h.astype(BF16))
    out_ref[0] = _dot(jnp.concatenate(o_parts, axis=0), wuv_ref[...], _TN)


def _dsa_call(q, qi, wi, ki2, lat, wuk_t, wuv_bd, bias_near, bias_far, *, topk, kc):
    bsz, s, aw = q.shape
    n_heads = aw // HEAD_DIM
    c_lat = lat.shape[-1]
    nb = s // Q_BLOCK
    hq = n_heads * Q_BLOCK
    const = lambda shape: pl.BlockSpec(shape, lambda b, j: tuple(0 for _ in shape))
    blk = lambda w: pl.BlockSpec((1, Q_BLOCK, w), lambda b, j: (b, j, 0))
    full = lambda w: pl.BlockSpec((1, s, w), lambda b, j: (b, 0, 0))
    return pl.pallas_call(
        functools.partial(_dsa_kernel, topk=topk, kc=kc, n_heads=n_heads, seq=s,
                          scale=float(HEAD_DIM) ** -0.5),
        grid=(bsz, nb),
        in_specs=[
            blk(aw), blk(qi.shape[-1]), blk(LANES), full(LANES), full(c_lat),
            const(wuk_t.shape), const(wuv_bd.shape), const(bias_near.shape), const(bias_far.shape),
        ],
        out_specs=blk(aw),
        out_shape=jax.ShapeDtypeStruct((bsz, s, aw), F32),
        scratch_shapes=[
            pltpu.VMEM((s, Q_BLOCK), I32),
            pltpu.VMEM((c_lat, hq), F32),
            pltpu.VMEM((kc, hq), F32),
            pltpu.VMEM((kc, hq), BF16),
            pltpu.VMEM((c_lat, hq), BF16),
            pltpu.VMEM((kc, Q_BLOCK), F32),
            pltpu.VMEM((1, Q_BLOCK), I32),
        ],
        compiler_params=pltpu.CompilerParams(
            dimension_semantics=("arbitrary", "arbitrary"),
            vmem_limit_bytes=VMEM_LIMIT_BYTES),
        name="dsa_attention",
    )(q, qi, wi, ki2, lat, wuk_t, wuv_bd, bias_near, bias_far)


def _out_kernel(ya_ref, za_ref, yb_ref, zb_ref, ga_ref, gb_ref, x_ref, mod_ref,
                wpa_ref, wpb_ref, wo_ref, fg_ref, o_ref, *, d, final):
    za = za_ref[0]
    zb = zb_ref[0]
    ua = (ya_ref[0] * (za * _sigmoid(za))).astype(BF16)
    ub = (yb_ref[0] * (zb * _sigmoid(zb))).astype(BF16)
    br_a = _dot(ua, wpa_ref[...])
    br_b = _dot(ub, wpb_ref[...])
    merged = _sigmoid(ga_ref[0]) * br_a + _sigmoid(gb_ref[0]) * br_b
    gate = mod_ref[0, :, 2 * d:3 * d]
    xo = x_ref[0] + gate * _dot(merged.astype(BF16), wo_ref[...])
    if final:
        xo = xo * lax.rsqrt(jnp.mean(xo * xo, axis=-1, keepdims=True) + NORM_EPS) * fg_ref[...]
    o_ref[0] = xo


def _out_call(ya, za, yb, zb, ga, gb, x, mod, wpa, wpb, wo, final_g, *, final, tm):
    bsz, s, d = x.shape
    d3 = mod.shape[-1]
    row = lambda w: pl.BlockSpec((1, tm, w), lambda b, j: (b, j, 0))
    const = lambda shape: pl.BlockSpec(shape, lambda b, j: tuple(0 for _ in shape))
    return pl.pallas_call(
        functools.partial(_out_kernel, d=d, final=final),
        grid=(bsz, s // tm),
        in_specs=[
            row(ya.shape[-1]), row(za.shape[-1]), row(yb.shape[-1]), row(zb.shape[-1]),
            row(d), row(d), row(d),
            pl.BlockSpec((1, 1, d3), lambda b, j: (b, 0, 0)),
            const(wpa.shape), const(wpb.shape), const(wo.shape), const((1, d)),
        ],
        out_specs=row(d),
        out_shape=jax.ShapeDtypeStruct((bsz, s, d), F32),
        compiler_params=pltpu.CompilerParams(
            dimension_semantics=("arbitrary", "arbitrary"),
            vmem_limit_bytes=VMEM_LIMIT_BYTES),
        name="gated_out",
    )(ya, za, yb, zb, ga, gb, x, mod.reshape(bsz, 1, d3), wpa, wpb, wo, final_g.reshape(1, d))


def _t5_bucket(dist, num_buckets):
    max_exact = num_buckets // 2
    is_small = dist < max_exact
    dd = jnp.maximum(dist, 1).astype(F32)
    large = max_exact + (jnp.log(dd / max_exact) / math.log(MAX_DISTANCE / max_exact)
                         * (num_buckets - max_exact)).astype(I32)
    large = jnp.minimum(large, num_buckets - 1)
    return jnp.where(is_small, dist, large)


def _bias_tiles(rel_bias):
    nbk = rel_bias.shape[0]
    i = jnp.arange(Q_BLOCK, dtype=I32)[:, None]
    jj = jnp.arange(2 * Q_BLOCK, dtype=I32)[None, :]
    tiles = []
    for base in (Q_BLOCK, 0):
        bucket = _t5_bucket(jnp.maximum(base + i - jj, 0), nbk)
        onehot = (bucket[None, :, :] == jnp.arange(nbk, dtype=I32)[:, None, None]).astype(F32)
        tiles.append(jnp.einsum('nh,nij->hji', rel_bias, onehot,
                                precision=lax.Precision.HIGHEST))
    far = rel_bias[_t5_bucket(jnp.full((1, 1), MAX_DISTANCE, I32), nbk)]
    return jnp.stack(tiles, axis=0) * LOG2E, jnp.moveaxis(far, -1, 0) * LOG2E


def _block_diag_heads(w, rows_per_head, cols_per_head):
    n = w.shape[0]
    eye = jnp.eye(n, dtype=w.dtype)
    return jnp.einsum('hrc,hg->hrgc', w, eye).reshape(n * rows_per_head, n * cols_per_head)


def kernel(x, c, ada_w, ada_b, norm_g, w_in, shift_mu, w0, w2, a0, a2, k_k, k_a, r_k, lnx_g, lnx_b,
           kv_norm_g, w_uk, w_uv, w_pa, w_pb, w_o, rel_bias, final_g):
    bsz, s, d = x.shape
    depth = w_in.shape[0]
    rw = w0.shape[-1]
    c_lat, n_heads, dh = w_uk.shape[1:]
    aw = n_heads * dh
    iw = IDX_HEADS * IDX_HEAD_DIM
    n_shift = 3 * rw + DECAY_LORA + ICLR_LORA
    topk = min(TOPK_MAX, s // 4)
    assert DECAY_LORA + ICLR_LORA == LANES and dh == HEAD_DIM and IDX_HEAD_DIM == HEAD_DIM
    assert s % (2 * Q_BLOCK) == 0 and rw % LANES == 0

    sizes = (rw, rw, rw, DECAY_LORA, ICLR_LORA, rw, aw, c_lat, aw, iw, IDX_HEAD_DIM, IDX_HEADS, d, d)
    offs = [0]
    for sz in sizes:
        offs.append(offs[-1] + sz)
    col = lambda i: slice(offs[i], offs[i + 1])
    widths = (rw, aw, c_lat, aw, iw, LANES, LANES, d, d)

    mod = _mod_call(c, ada_w, ada_b)
    bias_near, bias_far = _bias_tiles(rel_bias)
    ones_bd = _block_diag_heads(jnp.ones((rw // HEAD_DIM, HEAD_DIM, HEAD_DIM), BF16), HEAD_DIM, HEAD_DIM)
    kc = 512 if s % 512 == 0 else 2 * Q_BLOCK
    tm_in = 256
    tm_out = 512 if s % 512 == 0 else 256

    for l in range(depth):
        w = w_in[l]
        wi_pad = jnp.zeros((d, LANES - IDX_HEADS), w.dtype)
        w_packed = jnp.concatenate(
            [w[:, 0:n_shift], w[:, col(5)], w[:, col(6)], w[:, col(7)], w[:, col(8)], w[:, col(9)],
             w[:, col(10)], w[:, col(10)], w[:, col(11)], wi_pad, w[:, col(12)], w[:, col(13)]],
            axis=1).astype(BF16)
        sh, za, q, lat, zb, qi, ki2, wi, ga, gb = _inproj_call(
            x, mod[l], norm_g[l], w_packed, shift_mu[l], kv_norm_g[l],
            n_shift=n_shift, widths=widths, tm=tm_in)

        vecs = jnp.stack([w0[l], a0[l], k_k[l], k_a[l], r_k[l].reshape(-1), lnx_g[l], lnx_b[l],
                          jnp.zeros((rw,), F32)], axis=0)
        w2p = jnp.concatenate([w2[l], jnp.zeros((ICLR_LORA, rw), F32)], axis=0)
        a2p = jnp.concatenate([jnp.zeros((DECAY_LORA, rw), F32), a2[l]], axis=0)
        ya = _rwkv_call(sh, vecs, w2p, a2p, ones_bd, width=rw, tb=4 * CHUNK)

        wuk_t = _block_diag_heads(jnp.transpose(w_uk[l], (1, 0, 2)), c_lat, dh).astype(BF16)
        wuv_bd = _block_diag_heads(jnp.transpose(w_uv[l], (1, 0, 2)), c_lat, dh).astype(BF16)
        yb = _dsa_call(q, qi, wi, ki2, lat, wuk_t, wuv_bd, bias_near, bias_far, topk=topk, kc=kc)

        x = _out_call(ya, za, yb, zb, ga, gb, x, mod[l], w_pa[l].astype(BF16), w_pb[l].astype(BF16),
                      w_o[l].astype(BF16), final_g, final=(l == depth - 1), tm=tm_out)
    return x
```

```python
import functools
import math

import jax
import jax.numpy as jnp
from jax import lax
from jax.experimental import pallas as pl
from jax.experimental.pallas import tpu as pltpu

F32 = jnp.float32
BF16 = jnp.bfloat16
I32 = jnp.int32

DECAY_LORA = 64
ICLR_LORA = 64
GN_EPS = 64e-5
IDX_HEADS = 8
IDX_HEAD_DIM = 64
TOPK_MAX = 256
Q_BLOCK = 128
MAX_DISTANCE = 128
NORM_EPS = 1e-6

LANES = 128
VMEM_LIMIT_BYTES = 56 * 1024 * 1024

HEAD_DIM = 64
CHUNK = 64
INT_MIN = -(2 ** 31)
INT_MAX = 2 ** 31 - 1
NEG_MASK = -2e30
NEG_INIT = -1e30
LOG2E = 1.4426950408889634
BIT_GROUP = 256


def _sigmoid(x):
    return 1.0 / (1.0 + jnp.exp(-x))


def _dot(a, b, dims=(((1,), (0,)), ((), ()))):
    return lax.dot_general(a, b, dims, preferred_element_type=F32)


_NN = (((1,), (0,)), ((), ()))
_NT = (((1,), (1,)), ((), ()))
_TN = (((0,), (0,)), ((), ()))


def _split(x, n):
    parts = []
    rem = x
    for i in range(n):
        p = rem.astype(BF16)
        parts.append(p)
        if i + 1 < n:
            rem = rem - p.astype(F32)
    return parts


def _mm(a, b, passes, dims=_NN):
    if passes == 1:
        return _dot(a.astype(BF16), b.astype(BF16), dims)
    if passes == 3:
        a1, a2 = _split(a, 2)
        b1, b2 = _split(b, 2)
        return _dot(a1, b1, dims) + (_dot(a1, b2, dims) + _dot(a2, b1, dims))
    a1, a2, a3 = _split(a, 3)
    b1, b2, b3 = _split(b, 3)
    hi = _dot(a1, b1, dims)
    mid = _dot(a1, b2, dims) + _dot(a2, b1, dims)
    lo = _dot(a2, b2, dims) + (_dot(a1, b3, dims) + _dot(a3, b1, dims))
    return hi + (mid + lo)


def _mm_exact_rhs(a, b_bf16, n):
    out = None
    for p in _split(a, n):
        t = _dot(p, b_bf16)
        out = t if out is None else out + t
    return out


def _mod_kernel(c_ref, w_ref, b_ref, o_ref):
    c = c_ref[...]
    ca = c * _sigmoid(c)
    o_ref[0] = _mm(ca, w_ref[0], 6) + b_ref[0]


def _mod_call(c, ada_w, ada_b):
    depth, d, d3 = ada_w.shape
    bsz = c.shape[0]
    tn = d
    return pl.pallas_call(
        _mod_kernel,
        grid=(depth, d3 // tn),
        in_specs=[
            pl.BlockSpec((bsz, d), lambda l, n: (0, 0)),
            pl.BlockSpec((1, d, tn), lambda l, n: (l, 0, n)),
            pl.BlockSpec((1, 1, tn), lambda l, n: (l, 0, n)),
        ],
        out_specs=pl.BlockSpec((1, bsz, tn), lambda l, n: (l, 0, n)),
        out_shape=jax.ShapeDtypeStruct((depth, bsz, d3), F32),
        compiler_params=pltpu.CompilerParams(
            dimension_semantics=("arbitrary", "arbitrary"),
            vmem_limit_bytes=VMEM_LIMIT_BYTES),
        name="adaln_mod",
    )(c, ada_w, ada_b.reshape(depth, 1, d3))


def _inproj_kernel(x_ref, mod_ref, g_ref, w_ref, mu_ref, kvg_ref,
                   sh_ref, za_ref, q_ref, lat_ref, zb_ref, qi_ref, ki_ref, wi_ref, ga_ref, gb_ref,
                   carry_ref, *, d, n_shift, widths):
    j = pl.program_id(1)
    xb = x_ref[0]
    tm = xb.shape[0]
    shift = mod_ref[0, :, 0:d]
    scale = mod_ref[0, :, d:2 * d]
    ms = jnp.mean(xb * xb, axis=-1, keepdims=True)
    h = xb * lax.rsqrt(ms + NORM_EPS) * g_ref[...]
    h = h * (1.0 + scale) + shift
    hb = h.astype(BF16)

    @pl.when(j == 0)
    def _():
        carry_ref[...] = jnp.zeros_like(carry_ref)

    ps = _dot(hb, w_ref[:, 0:n_shift])
    prev = pltpu.roll(ps, 1, 0)
    row = lax.broadcasted_iota(I32, (tm, 1), 0)
    prev = jnp.where(row == 0, carry_ref[...], prev)
    carry_ref[...] = ps[tm - 1:tm, :]
    sh_ref[0] = ps + (prev - ps) * mu_ref[...]

    off = n_shift
    outs = (za_ref, q_ref, lat_ref, zb_ref, qi_ref, ki_ref, wi_ref, ga_ref, gb_ref)
    for o_ref, wd in zip(outs, widths):
        p = _dot(hb, w_ref[:, off:off + wd])
        if o_ref is lat_ref:
            p = p * lax.rsqrt(jnp.mean(p * p, axis=-1, keepdims=True) + NORM_EPS) * kvg_ref[...]
        o_ref[0] = p
        off += wd


def _inproj_call(x, mod, norm_g, w_packed, mu, kvg, *, n_shift, widths, tm):
    bsz, s, d = x.shape
    npk = w_packed.shape[1]
    d3 = mod.shape[-1]
    row_spec = lambda w: pl.BlockSpec((1, tm, w), lambda b, j: (b, j, 0))
    const = lambda shape: pl.BlockSpec(shape, lambda b, j: tuple(0 for _ in shape))
    out_widths = (n_shift,) + tuple(widths)
    return pl.pallas_call(
        functools.partial(_inproj_kernel, d=d, n_shift=n_shift, widths=tuple(widths)),
        grid=(bsz, s // tm),
        in_specs=[
            row_spec(d),
            pl.BlockSpec((1, 1, d3), lambda b, j: (b, 0, 0)),
            const((1, d)),
            pl.BlockSpec((d, npk), lambda b, j: (0, 0), pipeline_mode=pl.Buffered(1)),
            const((1, n_shift)),
            const((1, widths[2])),
        ],
        out_specs=[row_spec(w) for w in out_widths],
        out_shape=[jax.ShapeDtypeStruct((bsz, s, w), F32) for w in out_widths],
        scratch_shapes=[pltpu.VMEM((1, n_shift), F32)],
        compiler_params=pltpu.CompilerParams(
            dimension_semantics=("arbitrary", "arbitrary"),
            vmem_limit_bytes=VMEM_LIMIT_BYTES),
        name="inproj",
    )(x, mod.reshape(bsz, 1, d3), norm_g.reshape(1, d), w_packed, mu.reshape(1, n_shift),
      kvg.reshape(1, -1))


def _rwkv_kernel(sh_ref, vec_ref, w2_ref, a2_ref, ones_ref, out_ref, st_ref,
                 *, width, p_chain, p_inv, p_state):
    j = pl.program_id(1)
    L = CHUNK
    n_pairs = width // LANES

    @pl.when(j == 0)
    def _():
        st_ref[...] = jnp.zeros_like(st_ref)

    blk = sh_ref[0]
    pr = blk[:, 0:width]
    pk = blk[:, width:2 * width]
    pv = blk[:, 2 * width:3 * width]
    lora = blk[:, 3 * width:3 * width + LANES]
    lane = lax.broadcasted_iota(I32, (1, LANES), 1)
    lo_half = lane < HEAD_DIM
    t_in = jnp.where(lo_half, jnp.tanh(lora), lora)
    dec_in = _mm(t_in, w2_ref[...], 3)
    a_in = _mm(t_in, a2_ref[...], 3)

    w0 = vec_ref[0:1, :]
    a0 = vec_ref[1:2, :]
    k_k = vec_ref[2:3, :]
    k_a = vec_ref[3:4, :]
    r_k = vec_ref[4:5, :]
    lnx_g = vec_ref[5:6, :]
    lnx_b = vec_ref[6:7, :]
    ones_bd = ones_ref[...]

    z = -(w0 + dec_in)
    sp = jnp.maximum(z, 0.0) + jnp.log1p(jnp.exp(-jnp.abs(z)))
    logw = -jnp.exp(-sp - 0.5)
    a = _sigmoid(a0 + a_in)
    kk0 = pk * k_k
    ss = _mm_exact_rhs(kk0 * kk0, ones_bd, 3)
    kk = kk0 / jnp.maximum(jnp.sqrt(ss), 1e-12)
    k_eff = pk * (1.0 + (a - 1.0) * k_a)
    a_vec = -kk
    b_vec = kk * a

    tb = blk.shape[0]
    n_chunks = tb // L
    r_i = lax.broadcasted_iota(I32, (tb, tb), 0)
    c_i = lax.broadcasted_iota(I32, (tb, tb), 1)
    tril = jnp.where(((r_i // L) == (c_i // L)) & (c_i <= r_i), 1.0, 0.0).astype(BF16)
    cl = None
    for part in _split(logw, 3):
        t = _dot(tril, part)
        cl = t if cl is None else cl + t
    cl_last = jnp.concatenate(
        [jnp.broadcast_to(cl[c * L + L - 1:c * L + L, :], (L, width)) for c in range(n_chunks)], axis=0)
    rt = pr * jnp.exp(cl)
    at = a_vec * jnp.exp(cl - logw)
    e_inv = jnp.exp(-cl)
    bt = b_vec * e_inv
    kt = k_eff * e_inv
    e_l = jnp.exp(cl_last - cl)
    bl = b_vec * e_l
    kl = k_eff * e_l
    p_last = jnp.exp(cl_last)

    rr = lax.broadcasted_iota(I32, (LANES, LANES), 0)
    cc = lax.broadcasted_iota(I32, (LANES, LANES), 1)
    same = (rr // L) == (cc // L)
    strict = same & ((cc % L) < (rr % L))
    incl = same & ((cc % L) <= (rr % L))
    eye = rr == cc
    eye_f = jnp.where(eye, 1.0, 0.0)
    rr2 = lax.broadcasted_iota(I32, (LANES, 2 * LANES), 0)
    cc2 = lax.broadcasted_iota(I32, (LANES, 2 * LANES), 1)
    incl2 = ((rr2 // L) == ((cc2 % LANES) // L)) & ((cc2 % L) <= (rr2 % L))

    def stack(x):
        return jnp.concatenate([jnp.where(lo_half, x, 0.0), jnp.where(lo_half, 0.0, x)], axis=0)

    chains = [(c, p) for c in range(n_chunks) for p in range(n_pairs)]

    def piece(x, c, p):
        return x[c * L:(c + 1) * L, p * LANES:(p + 1) * LANES]

    atm, rtm, vm, blkl, a_ab, a_ak, a_rbk = {}, {}, {}, {}, {}, {}, {}
    for ch in chains:
        atm[ch], rtm[ch], vm[ch] = stack(piece(at, *ch)), stack(piece(rt, *ch)), stack(piece(pv, *ch))
        btm, ktm = stack(piece(bt, *ch)), stack(piece(kt, *ch))
        blkl[ch] = jnp.concatenate([stack(piece(bl, *ch)), stack(piece(kl, *ch))], axis=0).astype(BF16)
        g = _mm(jnp.concatenate([atm[ch], rtm[ch]], axis=0), jnp.concatenate([btm, ktm], axis=0),
                p_chain, _NT)
        a_ab[ch] = jnp.where(strict, g[0:LANES, 0:LANES], 0.0)
        a_ak[ch] = jnp.where(strict, g[0:LANES, LANES:], 0.0)
        a_rbk[ch] = jnp.where(incl2, g[LANES:, :], 0.0)
    tinv = {ch: eye_f + a_ab[ch] for ch in chains}
    qn = dict(a_ab)
    sq = 1
    while 2 * sq < L:
        for ch in chains:
            qn[ch] = _mm(qn[ch], qn[ch], p_inv)
        for ch in chains:
            tinv[ch] = tinv[ch] + _mm(tinv[ch], qn[ch], p_inv)
        sq *= 2
    akv = {ch: _mm(a_ak[ch], vm[ch], p_chain) for ch in chains}
    wz = {ch: _mm(tinv[ch], jnp.concatenate([atm[ch], akv[ch]], axis=1), p_inv) for ch in chains}
    wm = {ch: wz[ch][:, 0:LANES] for ch in chains}
    zv = {ch: jnp.concatenate([wz[ch][:, LANES:], vm[ch]], axis=0) for ch in chains}
    rqm = {ch: rtm[ch] + _mm(a_rbk[ch][:, 0:LANES], wm[ch], p_chain) for ch in chains}
    y0m = {ch: _mm(a_rbk[ch], zv[ch], p_chain) for ch in chains}
    mt = {ch: jnp.where(eye, piece(p_last, *ch)[0:1], 0.0)
          + _mm(blkl[ch][0:LANES], wm[ch], p_state, _TN) for ch in chains}
    ct = {ch: _mm(blkl[ch], zv[ch], p_state, _TN) for ch in chains}

    st = {p: st_ref[p] for p in range(n_pairs)}
    y_rows = []
    for c in range(n_chunks):
        ys = []
        for p in range(n_pairs):
            ym = _mm(rqm[(c, p)], st[p], p_state) + y0m[(c, p)]
            ys.append(ym[0:L] + ym[L:])
            st[p] = _mm(mt[(c, p)], st[p], p_state) + ct[(c, p)]
        y_rows.append(jnp.concatenate(ys, axis=1))
    for p in range(n_pairs):
        st_ref[p] = st[p]
    y = jnp.concatenate(y_rows, axis=0)

    inv_n = 1.0 / HEAD_DIM
    mu = _mm_exact_rhs(y, ones_bd, 3) * inv_n
    dlt = y - mu
    var = _mm_exact_rhs(dlt * dlt, ones_bd, 3) * inv_n
    yn = dlt * lax.rsqrt(var + GN_EPS) * lnx_g + lnx_b
    bonus = _mm_exact_rhs(pr * k_eff * r_k, ones_bd, 3) * pv
    out_ref[0] = yn + bonus


def _rwkv_call(sh, vecs, w2p, a2p, ones_bd, *, width, tb, p_chain=1, p_inv=1, p_state=1):
    bsz, s, n_shift = sh.shape
    n_pairs = width // LANES
    const = lambda shape: pl.BlockSpec(shape, lambda b, j: tuple(0 for _ in shape))
    return pl.pallas_call(
        functools.partial(_rwkv_kernel, width=width, p_chain=p_chain, p_inv=p_inv, p_state=p_state),
        grid=(bsz, s // tb),
        in_specs=[
            pl.BlockSpec((1, tb, n_shift), lambda b, j: (b, j, 0)),
            const(vecs.shape),
            const(w2p.shape),
            const(a2p.shape),
            const(ones_bd.shape),
        ],
        out_specs=pl.BlockSpec((1, tb, width), lambda b, j: (b, j, 0)),
        out_shape=jax.ShapeDtypeStruct((bsz, s, width), F32),
        scratch_shapes=[pltpu.VMEM((n_pairs, LANES, LANES), F32)],
        compiler_params=pltpu.CompilerParams(
            dimension_semantics=("arbitrary", "arbitrary"),
            vmem_limit_bytes=VMEM_LIMIT_BYTES),
        name="rwkv7_scan",
    )(sh, vecs, w2p, a2p, ones_bd)


def _bit_transpose32(a):
    a = list(a)
    m, j = 0x0000FFFF, 16
    while j:
        k = 0
        while k < 32:
            t = (a[k] ^ lax.shift_right_logical(a[k + j], jnp.int32(j))) & m
            a[k] = a[k] ^ t
            a[k + j] = a[k + j] ^ lax.shift_left(t, jnp.int32(j))
            k = (k + j + 1) & ~j
        j >>= 1
        m = (m ^ (m << j)) & 0xFFFFFFFF
    return a


def _dsa_kernel(q_ref, qi_ref, wi_ref, ki_ref, lat_ref, wukt_ref, wuv_ref, bias_ref, far_ref, out_ref,
                key_ref, acc_ref, lg_ref, p_ref, w_ref, madd_ref, mt_ref, planes_ref,
                *, topk, kc, n_heads, seq, scale):
    qb = pl.program_id(1)
    QB = Q_BLOCK
    NEAR = 2 * QB
    HQ = n_heads * QB
    c_lat = lat_ref.shape[-1]
    q_end = (qb + 1) * QB
    n_idx = (jnp.maximum(q_end, NEAR) + kc - 1) // kc
    tpos = qb * QB + lax.broadcasted_iota(I32, (1, QB), 1)

    def head(x, h):
        return x[:, h * QB:(h + 1) * QB]

    qi_t = qi_ref[0].T
    qi_w = jnp.concatenate([qi_t[h * HEAD_DIM:(h + 1) * HEAD_DIM, :] for h in range(n_heads)], axis=1)
    w_ref[...] = jnp.concatenate([qi_w, jnp.zeros_like(qi_w)], axis=0).astype(BF16)
    wi_t = wi_ref[0].T

    def idx_body(c, carry):
        ks = pl.multiple_of(c * kc, kc)
        kib = ki_ref[0, pl.ds(ks, kc), :].astype(BF16)
        score = None
        for g in range(n_heads // 2):
            s_g = jnp.maximum(_dot(kib, w_ref[:, 2 * g * QB:(2 * g + 2) * QB]), 0.0)
            part = wi_t[2 * g:2 * g + 1, :] * s_g[:, 0:QB] + wi_t[2 * g + 1:2 * g + 2, :] * s_g[:, QB:]
            score = part if score is None else score + part
        score = score + 0.0
        bits = pltpu.bitcast(score, I32)
        key = bits ^ ((bits >> 31) & INT_MAX)
        spos = ks + lax.broadcasted_iota(I32, (kc, 1), 0)
        key = jnp.where(spos <= tpos, key, INT_MIN)
        key_ref[pl.ds(ks, kc), :] = key
        ukey = key ^ INT_MIN
        for g in range(kc // BIT_GROUP):
            regs = _bit_transpose32([ukey[g * BIT_GROUP + 8 * i:g * BIT_GROUP + 8 * i + 8, :]
                                     for i in range(32)])
            row0 = pl.multiple_of(c * (kc // 32) + 8 * g, 8)
            for b in range(32):
                planes_ref[b, pl.ds(row0, 8), :] = regs[31 - b]
        return carry

    @pl.when((pl.program_id(0) == 0) & (qb == 0))
    def _():
        planes_ref[...] = jnp.zeros(planes_ref.shape, I32)

    lax.fori_loop(0, n_idx, idx_body, 0)

    n_rows = seq // 32
    valid = lax.broadcasted_iota(I32, (n_rows, 1), 0) < n_idx * (kc // 32)
    kf = float(topk)

    def bit_body(i, carry):
        alive, c_gt, thr_u = carry
        b = 31 - i
        w = alive & planes_ref[b]
        cnt = jnp.sum(lax.population_count(w).astype(F32), axis=0, keepdims=True)
        tot = c_gt + cnt
        take = tot >= kf
        alive = jnp.where(take, w, alive ^ w)
        c_gt = jnp.where(take, c_gt, tot)
        thr_u = jnp.where(take, thr_u | lax.shift_left(jnp.int32(1), b), thr_u)
        return alive, c_gt, thr_u

    alive, cnt_gt, thr_u = lax.fori_loop(
        0, 32, bit_body,
        (jnp.broadcast_to(jnp.where(valid, -1, 0), (n_rows, QB)).astype(I32),
         jnp.zeros((1, QB), F32), jnp.zeros((1, QB), I32)))
    thr = thr_u ^ INT_MIN
    cnt_ge = cnt_gt + jnp.sum(lax.population_count(alive).astype(F32), axis=0, keepdims=True)

    def count(pred):
        def body(c, acc):
            ks = pl.multiple_of(c * kc, kc)
            kch = key_ref[pl.ds(ks, kc), :]
            row = ks + lax.broadcasted_iota(I32, (kc, 1), 0)
            hit = jnp.where(pred(kch, row), 1.0, 0.0).reshape(kc // 8, 8, QB)
            parts = [hit[i] for i in range(kc // 8)]
            while len(parts) > 1:
                parts = [parts[i] + parts[i + 1] for i in range(0, len(parts), 2)]
            return acc + parts[0]
        acc = lax.fori_loop(0, n_idx, body, jnp.zeros((8, QB), F32))
        return jnp.sum(acc, axis=0, keepdims=True)

    need = kf - cnt_gt
    excess = jnp.where((cnt_ge > kf) & (thr > INT_MIN), 1.0, 0.0)
    mt_ref[...] = jnp.full((1, QB), INT_MAX, I32)

    @pl.when(jnp.max(excess) > 0.0)
    def _():
        nbits = max(1, (seq - 1).bit_length())

        def tie_body(i, mcur):
            cand = mcur | lax.shift_left(jnp.int32(1), nbits - 1 - i)
            cnt = count(lambda kch, row: (kch == thr) & (row < cand))
            return jnp.where(cnt < need, cand, mcur)

        mt_ref[...] = lax.fori_loop(0, nbits, tie_body, jnp.zeros((1, QB), I32))

    mtie = mt_ref[...]

    def selected(ks, width, hi_limit):
        kch = key_ref[pl.ds(ks, width), :]
        row = ks + lax.broadcasted_iota(I32, (width, 1), 0)
        sel = ((kch > thr) | ((kch == thr) & (row <= mtie))) & (kch > INT_MIN)
        if hi_limit is not None:
            sel = sel & (row < hi_limit)
        return sel

    q_t = q_ref[0].T.astype(BF16)
    qabs_t = _dot(wukt_ref[...], q_t) * (scale * LOG2E)
    w_ref[...] = jnp.concatenate([qabs_t[h * c_lat:(h + 1) * c_lat, :] for h in range(n_heads)],
                                 axis=1).astype(BF16)
    far_end = jnp.maximum(qb - 1, 0) * QB
    n_far = (far_end + kc - 1) // kc
    pair_cols = [slice(2 * g * QB, (2 * g + 2) * QB) for g in range(n_heads // 2)]

    def lat_chunk(ks, width):
        return lat_ref[0, pl.ds(ks, width), :].astype(BF16)

    acc_ref[...] = jnp.zeros(acc_ref.shape, F32)
    p_ref[...] = jnp.zeros(p_ref.shape, BF16)

    @pl.when(n_far > 0)
    def _():
        lg_ref[...] = _dot(lat_chunk(0, kc), w_ref[...])

    def far_body(c, carry):
        ms, ls = carry
        ksp = pl.multiple_of(jnp.maximum(c - 1, 0) * kc, kc)
        latp_t = lat_ref[0, pl.ds(ksp, kc), :].T.astype(BF16)
        ksn = pl.multiple_of(jnp.minimum(c + 1, n_far - 1) * kc, kc)
        latn = lat_chunk(ksn, kc)
        ks = pl.multiple_of(c * kc, kc)
        madd_ref[...] = jnp.where(selected(ks, kc, far_end), 0.0, NEG_MASK)
        new_ms, new_ls = list(ms), list(ls)
        for g, cols in enumerate(pair_cols):
            pv_g = _dot(latp_t, p_ref[:, cols])
            lg_next = _dot(latn, w_ref[:, cols])
            for h in (2 * g, 2 * g + 1):
                hc = slice(h * QB, (h + 1) * QB)
                bfar = far_ref[h]
                lgm = lg_ref[:, hc] + madd_ref[...]
                m_new = jnp.maximum(ms[h], jnp.max(lgm, axis=0, keepdims=True) + bfar)
                p = jnp.exp2(lgm - (m_new - bfar))
                alpha = jnp.exp2(ms[h] - m_new)
                new_ls[h] = alpha * ls[h] + jnp.sum(p, axis=0, keepdims=True)
                new_ms[h] = m_new
                p_ref[:, hc] = p.astype(BF16)
                acc_ref[:, hc] = (acc_ref[:, hc] + pv_g[:, (h - 2 * g) * QB:(h - 2 * g + 1) * QB]) * alpha
            lg_ref[:, cols] = lg_next
        return tuple(new_ms), tuple(new_ls)

    init = (tuple(jnp.full((1, QB), NEG_INIT, F32) for _ in range(n_heads)),
            tuple(jnp.zeros((1, QB), F32) for _ in range(n_heads)))
    ms, ls = lax.fori_loop(0, n_far, far_body, init)
    ks_last = pl.multiple_of(jnp.maximum(n_far - 1, 0) * kc, kc)
    pv = _dot(lat_chunk(ks_last, kc), p_ref[...], _TN)

    near_start = pl.multiple_of(far_end, QB)
    tile_id = jnp.where(qb == 0, 1, 0)
    latn = lat_chunk(near_start, NEAR)
    lgn = _dot(latn, w_ref[...])
    sel = selected(near_start, NEAR, None)
    ps, alphas, l_fin = [], [], []
    for h in range(n_heads):
        lgm = jnp.where(sel, head(lgn, h) + bias_ref[tile_id, h], NEG_MASK)
        m_new = jnp.maximum(ms[h], jnp.max(lgm, axis=0, keepdims=True))
        p = jnp.exp2(lgm - m_new)
        alpha = jnp.exp2(ms[h] - m_new)
        l_fin.append(alpha * ls[h] + jnp.sum(p, axis=0, keepdims=True))
        alphas.append(alpha)
        ps.append(p.astype(BF16))
    pvn = _dot(latn, jnp.concatenate(ps, axis=1), _TN)
    o_parts = []
    for h in range(n_heads):
        o_h = ((acc_ref[:, h * QB:(h + 1) * QB] + head(pv, h)) * alphas[h] + head(pvn, h)) / l_fin[h]
        o_parts.append(o_h.astype(BF16))
    out_ref[0] = _dot(jnp.concatenate(o_parts, axis=0), wuv_ref[...], _TN)


def _dsa_call(q, qi, wi, ki2, lat, wuk_t, wuv_bd, bias_near, bias_far, *, topk, kc):
    bsz, s, aw = q.shape
    n_heads = aw // HEAD_DIM
    c_lat = lat.shape[-1]
    nb = s // Q_BLOCK
    hq = n_heads * Q_BLOCK
    const = lambda shape: pl.BlockSpec(shape, lambda b, j: tuple(0 for _ in shape))
    blk = lambda w: pl.BlockSpec((1, Q_BLOCK, w), lambda b, j: (b, j, 0))
    full = lambda w: pl.BlockSpec((1, s, w), lambda b, j: (b, 0, 0))
    return pl.pallas_call(
        functools.partial(_dsa_kernel, topk=topk, kc=kc, n_heads=n_heads, seq=s,
                          scale=float(HEAD_DIM) ** -0.5),
        grid=(bsz, nb),
        in_specs=[
            blk(aw), blk(qi.shape[-1]), blk(LANES), full(LANES), full(c_lat),
            const(wuk_t.shape), const(wuv_bd.shape), const(bias_near.shape), const(bias_far.shape),
        ],
        out_specs=blk(aw),
        out_shape=jax.ShapeDtypeStruct((bsz, s, aw), F32),
        scratch_shapes=[
            pltpu.VMEM((s, Q_BLOCK), I32),
            pltpu.VMEM((c_lat, hq), F32),
            pltpu.VMEM((kc, hq), F32),
            pltpu.VMEM((kc, hq), BF16),
            pltpu.VMEM((c_lat, hq), BF16),
            pltpu.VMEM((kc, Q_BLOCK), F32),
            pltpu.VMEM((1, Q_BLOCK), I32),
            pltpu.VMEM((32, s // 32, Q_BLOCK), I32),
        ],
        compiler_params=pltpu.CompilerParams(
            dimension_semantics=("arbitrary", "arbitrary"),
            vmem_limit_bytes=VMEM_LIMIT_BYTES),
        name="dsa_attention",
    )(q, qi, wi, ki2, lat, wuk_t, wuv_bd, bias_near, bias_far)


def _out_kernel(ya_ref, za_ref, yb_ref, zb_ref, ga_ref, gb_ref, x_ref, mod_ref,
                wpa_ref, wpb_ref, wo_ref, fg_ref, o_ref, *, d, final):
    za = za_ref[0]
    zb = zb_ref[0]
    ua = (ya_ref[0] * (za * _sigmoid(za))).astype(BF16)
    ub = (yb_ref[0] * (zb * _sigmoid(zb))).astype(BF16)
    br_a = _dot(ua, wpa_ref[...])
    br_b = _dot(ub, wpb_ref[...])
    merged = _sigmoid(ga_ref[0]) * br_a + _sigmoid(gb_ref[0]) * br_b
    gate = mod_ref[0, :, 2 * d:3 * d]
    xo = x_ref[0] + gate * _dot(merged.astype(BF16), wo_ref[...])
    if final:
        xo = xo * lax.rsqrt(jnp.mean(xo * xo, axis=-1, keepdims=True) + NORM_EPS) * fg_ref[...]
    o_ref[0] = xo


def _out_call(ya, za, yb, zb, ga, gb, x, mod, wpa, wpb, wo, final_g, *, final, tm):
    bsz, s, d = x.shape
    d3 = mod.shape[-1]
    row = lambda w: pl.BlockSpec((1, tm, w), lambda b, j: (b, j, 0))
    const = lambda shape: pl.BlockSpec(shape, lambda b, j: tuple(0 for _ in shape))
    return pl.pallas_call(
        functools.partial(_out_kernel, d=d, final=final),
        grid=(bsz, s // tm),
        in_specs=[
            row(ya.shape[-1]), row(za.shape[-1]), row(yb.shape[-1]), row(zb.shape[-1]),
            row(d), row(d), row(d),
            pl.BlockSpec((1, 1, d3), lambda b, j: (b, 0, 0)),
            const(wpa.shape), const(wpb.shape), const(wo.shape), const((1, d)),
        ],
        out_specs=row(d),
        out_shape=jax.ShapeDtypeStruct((bsz, s, d), F32),
        compiler_params=pltpu.CompilerParams(
            dimension_semantics=("arbitrary", "arbitrary"),
            vmem_limit_bytes=VMEM_LIMIT_BYTES),
        name="gated_out",
    )(ya, za, yb, zb, ga, gb, x, mod.reshape(bsz, 1, d3), wpa, wpb, wo, final_g.reshape(1, d))


def _t5_bucket(dist, num_buckets):
    max_exact = num_buckets // 2
    is_small = dist < max_exact
    dd = jnp.maximum(dist, 1).astype(F32)
    large = max_exact + (jnp.log(dd / max_exact) / math.log(MAX_DISTANCE / max_exact)
                         * (num_buckets - max_exact)).astype(I32)
    large = jnp.minimum(large, num_buckets - 1)
    return jnp.where(is_small, dist, large)


def _bias_tiles(rel_bias):
    nbk = rel_bias.shape[0]
    i = jnp.arange(Q_BLOCK, dtype=I32)[:, None]
    jj = jnp.arange(2 * Q_BLOCK, dtype=I32)[None, :]
    tiles = []
    for base in (Q_BLOCK, 0):
        bucket = _t5_bucket(jnp.maximum(base + i - jj, 0), nbk)
        onehot = (bucket[None, :, :] == jnp.arange(nbk, dtype=I32)[:, None, None]).astype(F32)
        tiles.append(jnp.einsum('nh,nij->hji', rel_bias, onehot,
                                precision=lax.Precision.HIGHEST))
    far = rel_bias[_t5_bucket(jnp.full((1, 1), MAX_DISTANCE, I32), nbk)]
    return jnp.stack(tiles, axis=0) * LOG2E, jnp.moveaxis(far, -1, 0) * LOG2E


def _block_diag_heads(w, rows_per_head, cols_per_head):
    n = w.shape[0]
    eye = jnp.eye(n, dtype=w.dtype)
    return jnp.einsum('hrc,hg->hrgc', w, eye).reshape(n * rows_per_head, n * cols_per_head)


def kernel(x, c, ada_w, ada_b, norm_g, w_in, shift_mu, w0, w2, a0, a2, k_k, k_a, r_k, lnx_g, lnx_b,
           kv_norm_g, w_uk, w_uv, w_pa, w_pb, w_o, rel_bias, final_g):
    bsz, s, d = x.shape
    depth = w_in.shape[0]
    rw = w0.shape[-1]
    c_lat, n_heads, dh = w_uk.shape[1:]
    aw = n_heads * dh
    iw = IDX_HEADS * IDX_HEAD_DIM
    n_shift = 3 * rw + DECAY_LORA + ICLR_LORA
    topk = min(TOPK_MAX, s // 4)
    assert DECAY_LORA + ICLR_LORA == LANES and dh == HEAD_DIM and IDX_HEAD_DIM == HEAD_DIM
    assert s % (2 * Q_BLOCK) == 0 and rw % LANES == 0

    sizes = (rw, rw, rw, DECAY_LORA, ICLR_LORA, rw, aw, c_lat, aw, iw, IDX_HEAD_DIM, IDX_HEADS, d, d)
    offs = [0]
    for sz in sizes:
        offs.append(offs[-1] + sz)
    col = lambda i: slice(offs[i], offs[i + 1])
    widths = (rw, aw, c_lat, aw, iw, LANES, LANES, d, d)

    mod = _mod_call(c, ada_w, ada_b)
    bias_near, bias_far = _bias_tiles(rel_bias)
    ones_bd = _block_diag_heads(jnp.ones((rw // HEAD_DIM, HEAD_DIM, HEAD_DIM), BF16), HEAD_DIM, HEAD_DIM)
    kc = 512 if s % 512 == 0 else 2 * Q_BLOCK
    tm_in = 256
    tm_out = 512 if s % 512 == 0 else 256

    for l in range(depth):
        w = w_in[l]
        wi_pad = jnp.zeros((d, LANES - IDX_HEADS), w.dtype)
        w_packed = jnp.concatenate(
            [w[:, 0:n_shift], w[:, col(5)], w[:, col(6)], w[:, col(7)], w[:, col(8)], w[:, col(9)],
             w[:, col(10)], w[:, col(10)], w[:, col(11)], wi_pad, w[:, col(12)], w[:, col(13)]],
            axis=1).astype(BF16)
        sh, za, q, lat, zb, qi, ki2, wi, ga, gb = _inproj_call(
            x, mod[l], norm_g[l], w_packed, shift_mu[l], kv_norm_g[l],
            n_shift=n_shift, widths=widths, tm=tm_in)

        vecs = jnp.stack([w0[l], a0[l], k_k[l], k_a[l], r_k[l].reshape(-1), lnx_g[l], lnx_b[l],
                          jnp.zeros((rw,), F32)], axis=0)
        w2p = jnp.concatenate([w2[l], jnp.zeros((ICLR_LORA, rw), F32)], axis=0)
        a2p = jnp.concatenate([jnp.zeros((DECAY_LORA, rw), F32), a2[l]], axis=0)
        ya = _rwkv_call(sh, vecs, w2p, a2p, ones_bd, width=rw, tb=4 * CHUNK)

        wuk_t = _block_diag_heads(jnp.transpose(w_uk[l], (1, 0, 2)), c_lat, dh).astype(BF16)
        wuv_bd = _block_diag_heads(jnp.transpose(w_uv[l], (1, 0, 2)), c_lat, dh).astype(BF16)
        yb = _dsa_call(q, qi, wi, ki2, lat, wuk_t, wuv_bd, bias_near, bias_far, topk=topk, kc=kc)

        x = _out_call(ya, za, yb, zb, ga, gb, x, mod[l], w_pa[l].astype(BF16), w_pb[l].astype(BF16),
                      w_o[l].astype(BF16), final_g, final=(l == depth - 1), tm=tm_out)
    return x
```

```python
import functools
import math

import jax
import jax.numpy as jnp
from jax import lax
from jax.experimental import pallas as pl
from jax.experimental.pallas import tpu as pltpu

F32 = jnp.float32
BF16 = jnp.bfloat16
I32 = jnp.int32

DECAY_LORA = 64
ICLR_LORA = 64
GN_EPS = 64e-5
IDX_HEADS = 8
IDX_HEAD_DIM = 64
TOPK_MAX = 256
Q_BLOCK = 128
MAX_DISTANCE = 128
NORM_EPS = 1e-6

LANES = 128
VMEM_LIMIT_BYTES = 56 * 1024 * 1024

HEAD_DIM = 64
CHUNK = 64
INT_MIN = -(2 ** 31)
INT_MAX = 2 ** 31 - 1
NEG_MASK = -2e30
NEG_INIT = -1e30
LOG2E = 1.4426950408889634
BIT_GROUP = 256


def _sigmoid(x):
    return 1.0 / (1.0 + jnp.exp(-x))


def _dot(a, b, dims=(((1,), (0,)), ((), ()))):
    return lax.dot_general(a, b, dims, preferred_element_type=F32)


_NN = (((1,), (0,)), ((), ()))
_NT = (((1,), (1,)), ((), ()))
_TN = (((0,), (0,)), ((), ()))


def _split(x, n):
    parts = []
    rem = x
    for i in range(n):
        p = rem.astype(BF16)
        parts.append(p)
        if i + 1 < n:
            rem = rem - p.astype(F32)
    return parts


def _mm(a, b, passes, dims=_NN):
    if passes == 1:
        return _dot(a.astype(BF16), b.astype(BF16), dims)
    if passes == 3:
        a1, a2 = _split(a, 2)
        b1, b2 = _split(b, 2)
        return _dot(a1, b1, dims) + (_dot(a1, b2, dims) + _dot(a2, b1, dims))
    a1, a2, a3 = _split(a, 3)
    b1, b2, b3 = _split(b, 3)
    hi = _dot(a1, b1, dims)
    mid = _dot(a1, b2, dims) + _dot(a2, b1, dims)
    lo = _dot(a2, b2, dims) + (_dot(a1, b3, dims) + _dot(a3, b1, dims))
    return hi + (mid + lo)


def _mm_exact_rhs(a, b_bf16, n):
    out = None
    for p in _split(a, n):
        t = _dot(p, b_bf16)
        out = t if out is None else out + t
    return out


def _mod_kernel(c_ref, w_ref, b_ref, o_ref):
    c = c_ref[...]
    ca = c * _sigmoid(c)
    o_ref[0] = _mm(ca, w_ref[0], 6) + b_ref[0]


def _mod_call(c, ada_w, ada_b):
    depth, d, d3 = ada_w.shape
    bsz = c.shape[0]
    tn = d
    return pl.pallas_call(
        _mod_kernel,
        grid=(depth, d3 // tn),
        in_specs=[
            pl.BlockSpec((bsz, d), lambda l, n: (0, 0)),
            pl.BlockSpec((1, d, tn), lambda l, n: (l, 0, n)),
            pl.BlockSpec((1, 1, tn), lambda l, n: (l, 0, n)),
        ],
        out_specs=pl.BlockSpec((1, bsz, tn), lambda l, n: (l, 0, n)),
        out_shape=jax.ShapeDtypeStruct((depth, bsz, d3), F32),
        compiler_params=pltpu.CompilerParams(
            dimension_semantics=("arbitrary", "arbitrary"),
            vmem_limit_bytes=VMEM_LIMIT_BYTES),
        name="adaln_mod",
    )(c, ada_w, ada_b.reshape(depth, 1, d3))


def _inproj_kernel(x_ref, mod_ref, g_ref, w_ref, mu_ref, kvg_ref,
                   sh_ref, za_ref, q_ref, lat_ref, zb_ref, qi_ref, ki_ref, wi_ref, ga_ref, gb_ref,
                   carry_ref, *, d, n_shift, widths):
    j = pl.program_id(1)
    xb = x_ref[0]
    tm = xb.shape[0]
    shift = mod_ref[0, :, 0:d]
    scale = mod_ref[0, :, d:2 * d]
    ms = jnp.mean(xb * xb, axis=-1, keepdims=True)
    h = xb * lax.rsqrt(ms + NORM_EPS) * g_ref[...]
    h = h * (1.0 + scale) + shift
    hb = h.astype(BF16)

    @pl.when(j == 0)
    def _():
        carry_ref[...] = jnp.zeros_like(carry_ref)

    ps = _dot(hb, w_ref[:, 0:n_shift])
    prev = pltpu.roll(ps, 1, 0)
    row = lax.broadcasted_iota(I32, (tm, 1), 0)
    prev = jnp.where(row == 0, carry_ref[...], prev)
    carry_ref[...] = ps[tm - 1:tm, :]
    sh_ref[0] = ps + (prev - ps) * mu_ref[...]

    off = n_shift
    outs = (za_ref, q_ref, lat_ref, zb_ref, qi_ref, ki_ref, wi_ref, ga_ref, gb_ref)
    for o_ref, wd in zip(outs, widths):
        p = _dot(hb, w_ref[:, off:off + wd])
        if o_ref is lat_ref:
            p = p * lax.rsqrt(jnp.mean(p * p, axis=-1, keepdims=True) + NORM_EPS) * kvg_ref[...]
        o_ref[0] = p.astype(o_ref.dtype)
        off += wd


def _inproj_call(x, mod, norm_g, w_packed, mu, kvg, *, n_shift, widths, tm):
    bsz, s, d = x.shape
    npk = w_packed.shape[1]
    d3 = mod.shape[-1]
    row_spec = lambda w: pl.BlockSpec((1, tm, w), lambda b, j: (b, j, 0))
    const = lambda shape: pl.BlockSpec(shape, lambda b, j: tuple(0 for _ in shape))
    out_widths = (n_shift,) + tuple(widths)
    out_dtypes = (F32, BF16, BF16, BF16, BF16, BF16, BF16, F32, BF16, BF16)
    return pl.pallas_call(
        functools.partial(_inproj_kernel, d=d, n_shift=n_shift, widths=tuple(widths)),
        grid=(bsz, s // tm),
        in_specs=[
            row_spec(d),
            pl.BlockSpec((1, 1, d3), lambda b, j: (b, 0, 0)),
            const((1, d)),
            pl.BlockSpec((d, npk), lambda b, j: (0, 0), pipeline_mode=pl.Buffered(1)),
            const((1, n_shift)),
            const((1, widths[2])),
        ],
        out_specs=[row_spec(w) for w in out_widths],
        out_shape=[jax.ShapeDtypeStruct((bsz, s, w), dt) for w, dt in zip(out_widths, out_dtypes)],
        scratch_shapes=[pltpu.VMEM((1, n_shift), F32)],
        compiler_params=pltpu.CompilerParams(
            dimension_semantics=("arbitrary", "arbitrary"),
            vmem_limit_bytes=VMEM_LIMIT_BYTES),
        name="inproj",
    )(x, mod.reshape(bsz, 1, d3), norm_g.reshape(1, d), w_packed, mu.reshape(1, n_shift),
      kvg.reshape(1, -1))


def _rwkv_kernel(sh_ref, vec_ref, w2_ref, a2_ref, ones_ref, out_ref, st_ref,
                 *, width, p_chain, p_inv, p_state):
    j = pl.program_id(1)
    L = CHUNK
    n_pairs = width // LANES

    @pl.when(j == 0)
    def _():
        st_ref[...] = jnp.zeros_like(st_ref)

    blk = sh_ref[0]
    pr = blk[:, 0:width]
    pk = blk[:, width:2 * width]
    pv = blk[:, 2 * width:3 * width]
    lora = blk[:, 3 * width:3 * width + LANES]
    lane = lax.broadcasted_iota(I32, (1, LANES), 1)
    lo_half = lane < HEAD_DIM
    t_in = jnp.where(lo_half, jnp.tanh(lora), lora)
    dec_in = _mm(t_in, w2_ref[...], 3)
    a_in = _mm(t_in, a2_ref[...], 3)

    w0 = vec_ref[0:1, :]
    a0 = vec_ref[1:2, :]
    k_k = vec_ref[2:3, :]
    k_a = vec_ref[3:4, :]
    r_k = vec_ref[4:5, :]
    lnx_g = vec_ref[5:6, :]
    lnx_b = vec_ref[6:7, :]
    ones_bd = ones_ref[...]

    z = -(w0 + dec_in)
    sp = jnp.maximum(z, 0.0) + jnp.log1p(jnp.exp(-jnp.abs(z)))
    logw = -jnp.exp(-sp - 0.5)
    a = _sigmoid(a0 + a_in)
    kk0 = pk * k_k
    ss = _mm_exact_rhs(kk0 * kk0, ones_bd, 3)
    kk = kk0 / jnp.maximum(jnp.sqrt(ss), 1e-12)
    k_eff = pk * (1.0 + (a - 1.0) * k_a)
    a_vec = -kk
    b_vec = kk * a

    tb = blk.shape[0]
    n_chunks = tb // L
    r_i = lax.broadcasted_iota(I32, (tb, tb), 0)
    c_i = lax.broadcasted_iota(I32, (tb, tb), 1)
    tril = jnp.where(((r_i // L) == (c_i // L)) & (c_i <= r_i), 1.0, 0.0).astype(BF16)
    cl = None
    for part in _split(logw, 3):
        t = _dot(tril, part)
        cl = t if cl is None else cl + t
    cl_last = jnp.concatenate(
        [jnp.broadcast_to(cl[c * L + L - 1:c * L + L, :], (L, width)) for c in range(n_chunks)], axis=0)
    rt = pr * jnp.exp(cl)
    at = a_vec * jnp.exp(cl - logw)
    e_inv = jnp.exp(-cl)
    bt = b_vec * e_inv
    kt = k_eff * e_inv
    e_l = jnp.exp(cl_last - cl)
    bl = b_vec * e_l
    kl = k_eff * e_l
    p_last = jnp.exp(cl_last)

    rr = lax.broadcasted_iota(I32, (LANES, LANES), 0)
    cc = lax.broadcasted_iota(I32, (LANES, LANES), 1)
    same = (rr // L) == (cc // L)
    strict = same & ((cc % L) < (rr % L))
    incl = same & ((cc % L) <= (rr % L))
    eye = rr == cc
    eye_f = jnp.where(eye, 1.0, 0.0)
    rr2 = lax.broadcasted_iota(I32, (LANES, 2 * LANES), 0)
    cc2 = lax.broadcasted_iota(I32, (LANES, 2 * LANES), 1)
    incl2 = ((rr2 // L) == ((cc2 % LANES) // L)) & ((cc2 % L) <= (rr2 % L))

    def stack(x):
        return jnp.concatenate([jnp.where(lo_half, x, 0.0), jnp.where(lo_half, 0.0, x)], axis=0)

    chains = [(c, p) for c in range(n_chunks) for p in range(n_pairs)]

    def piece(x, c, p):
        return x[c * L:(c + 1) * L, p * LANES:(p + 1) * LANES]

    atm, rtm, vm, blkl, a_ab, a_ak, a_rbk = {}, {}, {}, {}, {}, {}, {}
    for ch in chains:
        atm[ch], rtm[ch], vm[ch] = stack(piece(at, *ch)), stack(piece(rt, *ch)), stack(piece(pv, *ch))
        btm, ktm = stack(piece(bt, *ch)), stack(piece(kt, *ch))
        blkl[ch] = jnp.concatenate([stack(piece(bl, *ch)), stack(piece(kl, *ch))], axis=0).astype(BF16)
        g = _mm(jnp.concatenate([atm[ch], rtm[ch]], axis=0), jnp.concatenate([btm, ktm], axis=0),
                p_chain, _NT)
        a_ab[ch] = jnp.where(strict, g[0:LANES, 0:LANES], 0.0)
        a_ak[ch] = jnp.where(strict, g[0:LANES, LANES:], 0.0)
        a_rbk[ch] = jnp.where(incl2, g[LANES:, :], 0.0)
    tinv = {ch: eye_f + a_ab[ch] for ch in chains}
    qn = dict(a_ab)
    sq = 1
    while 2 * sq < L:
        for ch in chains:
            qn[ch] = _mm(qn[ch], qn[ch], p_inv)
        for ch in chains:
            tinv[ch] = tinv[ch] + _mm(tinv[ch], qn[ch], p_inv)
        sq *= 2
    akv = {ch: _mm(a_ak[ch], vm[ch], p_chain) for ch in chains}
    wz = {ch: _mm(tinv[ch], jnp.concatenate([atm[ch], akv[ch]], axis=1), p_inv) for ch in chains}
    wm = {ch: wz[ch][:, 0:LANES] for ch in chains}
    zv = {ch: jnp.concatenate([wz[ch][:, LANES:], vm[ch]], axis=0) for ch in chains}
    rqm = {ch: rtm[ch] + _mm(a_rbk[ch][:, 0:LANES], wm[ch], p_chain) for ch in chains}
    y0m = {ch: _mm(a_rbk[ch], zv[ch], p_chain) for ch in chains}
    mt = {ch: jnp.where(eye, piece(p_last, *ch)[0:1], 0.0)
          + _mm(blkl[ch][0:LANES], wm[ch], p_state, _TN) for ch in chains}
    ct = {ch: _mm(blkl[ch], zv[ch], p_state, _TN) for ch in chains}

    st = {p: st_ref[p] for p in range(n_pairs)}
    y_rows = []
    for c in range(n_chunks):
        ys = []
        for p in range(n_pairs):
            ym = _mm(rqm[(c, p)], st[p], p_state) + y0m[(c, p)]
            ys.append(ym[0:L] + ym[L:])
            st[p] = _mm(mt[(c, p)], st[p], p_state) + ct[(c, p)]
        y_rows.append(jnp.concatenate(ys, axis=1))
    for p in range(n_pairs):
        st_ref[p] = st[p]
    y = jnp.concatenate(y_rows, axis=0)

    inv_n = 1.0 / HEAD_DIM
    mu = _mm_exact_rhs(y, ones_bd, 3) * inv_n
    dlt = y - mu
    var = _mm_exact_rhs(dlt * dlt, ones_bd, 3) * inv_n
    yn = dlt * lax.rsqrt(var + GN_EPS) * lnx_g + lnx_b
    bonus = _mm_exact_rhs(pr * k_eff * r_k, ones_bd, 3) * pv
    out_ref[0] = yn + bonus


def _rwkv_call(sh, vecs, w2p, a2p, ones_bd, *, width, tb, p_chain=1, p_inv=1, p_state=1):
    bsz, s, n_shift = sh.shape
    n_pairs = width // LANES
    const = lambda shape: pl.BlockSpec(shape, lambda b, j: tuple(0 for _ in shape))
    return pl.pallas_call(
        functools.partial(_rwkv_kernel, width=width, p_chain=p_chain, p_inv=p_inv, p_state=p_state),
        grid=(bsz, s // tb),
        in_specs=[
            pl.BlockSpec((1, tb, n_shift), lambda b, j: (b, j, 0)),
            const(vecs.shape),
            const(w2p.shape),
            const(a2p.shape),
            const(ones_bd.shape),
        ],
        out_specs=pl.BlockSpec((1, tb, width), lambda b, j: (b, j, 0)),
        out_shape=jax.ShapeDtypeStruct((bsz, s, width), F32),
        scratch_shapes=[pltpu.VMEM((n_pairs, LANES, LANES), F32)],
        compiler_params=pltpu.CompilerParams(
            dimension_semantics=("arbitrary", "arbitrary"),
            vmem_limit_bytes=VMEM_LIMIT_BYTES),
        name="rwkv7_scan",
    )(sh, vecs, w2p, a2p, ones_bd)


def _bit_transpose32(a):
    a = list(a)
    m, j = 0x0000FFFF, 16
    while j:
        k = 0
        while k < 32:
            t = (a[k] ^ lax.shift_right_logical(a[k + j], jnp.int32(j))) & m
            a[k] = a[k] ^ t
            a[k + j] = a[k + j] ^ lax.shift_left(t, jnp.int32(j))
            k = (k + j + 1) & ~j
        j >>= 1
        m = (m ^ (m << j)) & 0xFFFFFFFF
    return a


def _dsa_kernel(q_ref, qi_ref, wi_ref, ki_ref, lat_ref, wukt_ref, wuv_ref, bias_ref, far_ref, out_ref,
                key_ref, acc_ref, lg_ref, p_ref, w_ref, wqk_ref, madd_ref, mt_ref, planes_ref,
                *, topk, kc, n_heads, seq, scale):
    qb = pl.program_id(1)
    QB = Q_BLOCK
    NEAR = 2 * QB
    HQ = n_heads * QB
    c_lat = lat_ref.shape[-1]
    q_end = (qb + 1) * QB
    n_idx = (jnp.maximum(q_end, NEAR) + kc - 1) // kc
    tpos = qb * QB + lax.broadcasted_iota(I32, (1, QB), 1)

    def head(x, h):
        return x[:, h * QB:(h + 1) * QB]

    qi_t = qi_ref[0].T
    qi_w = jnp.concatenate([qi_t[h * HEAD_DIM:(h + 1) * HEAD_DIM, :] for h in range(n_heads)], axis=1)
    w_ref[...] = jnp.concatenate([qi_w, jnp.zeros_like(qi_w)], axis=0).astype(BF16)
    wi_t = wi_ref[0].T

    def idx_body(c, carry):
        ks = pl.multiple_of(c * kc, kc)
        kib = ki_ref[0, pl.ds(ks, kc), :].astype(BF16)
        score = None
        for g in range(n_heads // 2):
            s_g = jnp.maximum(_dot(kib, w_ref[:, 2 * g * QB:(2 * g + 2) * QB]), 0.0)
            part = wi_t[2 * g:2 * g + 1, :] * s_g[:, 0:QB] + wi_t[2 * g + 1:2 * g + 2, :] * s_g[:, QB:]
            score = part if score is None else score + part
        score = score + 0.0
        bits = pltpu.bitcast(score, I32)
        key = bits ^ ((bits >> 31) & INT_MAX)
        spos = ks + lax.broadcasted_iota(I32, (kc, 1), 0)
        key = jnp.where(spos <= tpos, key, INT_MIN)
        key_ref[pl.ds(ks, kc), :] = key
        ukey = key ^ INT_MIN
        for g in range(kc // BIT_GROUP):
            regs = _bit_transpose32([ukey[g * BIT_GROUP + 8 * i:g * BIT_GROUP + 8 * i + 8, :]
                                     for i in range(32)])
            row0 = pl.multiple_of(c * (kc // 32) + 8 * g, 8)
            for b in range(32):
                planes_ref[b, pl.ds(row0, 8), :] = regs[31 - b]
        return carry

    @pl.when((pl.program_id(0) == 0) & (qb == 0))
    def _():
        planes_ref[...] = jnp.zeros(planes_ref.shape, I32)

    lax.fori_loop(0, n_idx, idx_body, 0)

    q_t = q_ref[0].T.astype(BF16)
    for h in range(n_heads):
        qa = _dot(wukt_ref[h], q_t[h * HEAD_DIM:(h + 1) * HEAD_DIM, :])
        wqk_ref[:, h * QB:(h + 1) * QB] = (qa * (scale * LOG2E)).astype(BF16)
    far_end = jnp.maximum(qb - 1, 0) * QB
    n_far = (far_end + kc - 1) // kc
    near_start = pl.multiple_of(far_end, QB)
    near_ld = pl.multiple_of(jnp.minimum(near_start, seq - kc), QB)
    near_off = pl.multiple_of(near_start - near_ld, QB)
    ks0 = pl.multiple_of(jnp.where(n_far > 0, 0, near_ld), QB)
    lat0 = lat_ref[0, pl.ds(ks0, kc), :].astype(BF16)
    for g in range(n_heads // 2):
        cols = slice(2 * g * QB, (2 * g + 2) * QB)
        lg_ref[:, cols] = _dot(lat0, wqk_ref[:, cols])

    n_rows = seq // 32
    valid = lax.broadcasted_iota(I32, (n_rows, 1), 0) < n_idx * (kc // 32)
    kf = float(topk)

    def bit_body(i, carry):
        alive, c_gt, thr_u = carry
        b = 31 - i
        w = alive & planes_ref[b]
        cnt = jnp.sum(lax.population_count(w).astype(F32), axis=0, keepdims=True)
        tot = c_gt + cnt
        take = tot >= kf
        alive = jnp.where(take, w, alive ^ w)
        c_gt = jnp.where(take, c_gt, tot)
        thr_u = jnp.where(take, thr_u | lax.shift_left(jnp.int32(1), b), thr_u)
        return alive, c_gt, thr_u

    alive, cnt_gt, thr_u = lax.fori_loop(
        0, 32, bit_body,
        (jnp.broadcast_to(jnp.where(valid, -1, 0), (n_rows, QB)).astype(I32),
         jnp.zeros((1, QB), F32), jnp.zeros((1, QB), I32)))
    thr = thr_u ^ INT_MIN
    cnt_ge = cnt_gt + jnp.sum(lax.population_count(alive).astype(F32), axis=0, keepdims=True)

    def count(pred):
        def body(c, acc):
            ks = pl.multiple_of(c * kc, kc)
            kch = key_ref[pl.ds(ks, kc), :]
            row = ks + lax.broadcasted_iota(I32, (kc, 1), 0)
            hit = jnp.where(pred(kch, row), 1.0, 0.0).reshape(kc // 8, 8, QB)
            parts = [hit[i] for i in range(kc // 8)]
            while len(parts) > 1:
                parts = [parts[i] + parts[i + 1] for i in range(0, len(parts), 2)]
            return acc + parts[0]
        acc = lax.fori_loop(0, n_idx, body, jnp.zeros((8, QB), F32))
        return jnp.sum(acc, axis=0, keepdims=True)

    need = kf - cnt_gt
    excess = jnp.where((cnt_ge > kf) & (thr > INT_MIN), 1.0, 0.0)
    mt_ref[...] = jnp.full((1, QB), INT_MAX, I32)

    @pl.when(jnp.max(excess) > 0.0)
    def _():
        nbits = max(1, (seq - 1).bit_length())

        def tie_body(i, mcur):
            cand = mcur | lax.shift_left(jnp.int32(1), nbits - 1 - i)
            cnt = count(lambda kch, row: (kch == thr) & (row < cand))
            return jnp.where(cnt < need, cand, mcur)

        mt_ref[...] = lax.fori_loop(0, nbits, tie_body, jnp.zeros((1, QB), I32))

    mtie = jnp.where(thr == INT_MIN, -1, mt_ref[...])

    def selected(ks, width, hi_limit):
        kch = key_ref[pl.ds(ks, width), :]
        row = ks + lax.broadcasted_iota(I32, (width, 1), 0)
        sel = (kch > thr) | ((kch == thr) & (row <= mtie))
        if hi_limit is not None:
            sel = sel & (row < hi_limit)
        return sel

    pair_cols = [slice(2 * g * QB, (2 * g + 2) * QB) for g in range(n_heads // 2)]

    def lat_chunk(ks, width):
        return lat_ref[0, pl.ds(ks, width), :].astype(BF16)

    acc_ref[...] = jnp.zeros(acc_ref.shape, F32)
    p_ref[...] = jnp.zeros(p_ref.shape, BF16)


    def far_body(c, carry):
        ms, ls = carry
        ksp = pl.multiple_of(jnp.maximum(c - 1, 0) * kc, kc)
        latp_t = lat_ref[0, pl.ds(ksp, kc), :].T.astype(BF16)
        ksn = pl.multiple_of(jnp.where(c + 1 < n_far, (c + 1) * kc, near_ld), QB)
        latn = lat_chunk(ksn, kc)
        ks = pl.multiple_of(c * kc, kc)
        madd_ref[...] = jnp.where(selected(ks, kc, far_end), 0.0, NEG_MASK)
        new_ms, new_ls = list(ms), list(ls)
        for g, cols in enumerate(pair_cols):
            pv_g = _dot(latp_t, p_ref[:, cols])
            lg_next = _dot(latn, wqk_ref[:, cols])
            for h in (2 * g, 2 * g + 1):
                hc = slice(h * QB, (h + 1) * QB)
                bfar = far_ref[h]
                lgm = lg_ref[:, hc] + madd_ref[...]
                m_new = jnp.maximum(ms[h], jnp.max(lgm, axis=0, keepdims=True) + bfar)
                p = jnp.exp2(lgm - (m_new - bfar))
                alpha = jnp.exp2(ms[h] - m_new)
                new_ls[h] = alpha * ls[h] + jnp.sum(p, axis=0, keepdims=True)
                new_ms[h] = m_new
                p_ref[:, hc] = p.astype(BF16)
                acc_ref[:, hc] = (acc_ref[:, hc] + pv_g[:, (h - 2 * g) * QB:(h - 2 * g + 1) * QB]) * alpha
            lg_ref[:, cols] = lg_next
        return tuple(new_ms), tuple(new_ls)

    init = (tuple(jnp.full((1, QB), NEG_INIT, F32) for _ in range(n_heads)),
            tuple(jnp.zeros((1, QB), F32) for _ in range(n_heads)))
    ms, ls = lax.fori_loop(0, n_far, far_body, init)
    ks_last = pl.multiple_of(jnp.maximum(n_far - 1, 0) * kc, kc)
    pv = _dot(lat_chunk(ks_last, kc), p_ref[...], _TN)

    tile_id = jnp.where(qb == 0, 1, 0)
    latn = lat_chunk(near_start, NEAR)
    lgn = lg_ref[pl.ds(near_off, NEAR), :]
    sel = selected(near_start, NEAR, None)
    ps, alphas, l_fin = [], [], []
    for h in range(n_heads):
        lgm = jnp.where(sel, head(lgn, h) + bias_ref[tile_id, h], NEG_MASK)
        m_new = jnp.maximum(ms[h], jnp.max(lgm, axis=0, keepdims=True))
        p = jnp.exp2(lgm - m_new)
        alpha = jnp.exp2(ms[h] - m_new)
        l_fin.append(alpha * ls[h] + jnp.sum(p, axis=0, keepdims=True))
        alphas.append(alpha)
        ps.append(p.astype(BF16))
    pvn = _dot(latn, jnp.concatenate(ps, axis=1), _TN)
    o_parts = []
    for h in range(n_heads):
        o_h = ((acc_ref[:, h * QB:(h + 1) * QB] + head(pv, h)) * alphas[h] + head(pvn, h)) / l_fin[h]
        o_parts.append(o_h.astype(BF16))
    out_ref[0] = _dot(jnp.concatenate(o_parts, axis=0), wuv_ref[...], _TN)


def _dsa_call(q, qi, wi, ki2, lat, wuk_t, wuv_bd, bias_near, bias_far, *, topk, kc):
    bsz, s, aw = q.shape
    n_heads = aw // HEAD_DIM
    c_lat = lat.shape[-1]
    nb = s // Q_BLOCK
    hq = n_heads * Q_BLOCK
    const = lambda shape: pl.BlockSpec(shape, lambda b, j: tuple(0 for _ in shape))
    blk = lambda w: pl.BlockSpec((1, Q_BLOCK, w), lambda b, j: (b, j, 0))
    full = lambda w: pl.BlockSpec((1, s, w), lambda b, j: (b, 0, 0))
    return pl.pallas_call(
        functools.partial(_dsa_kernel, topk=topk, kc=kc, n_heads=n_heads, seq=s,
                          scale=float(HEAD_DIM) ** -0.5),
        grid=(bsz, nb),
        in_specs=[
            blk(aw), blk(qi.shape[-1]), blk(LANES), full(LANES), full(c_lat),
            const(wuk_t.shape), const(wuv_bd.shape), const(bias_near.shape), const(bias_far.shape),
        ],
        out_specs=blk(aw),
        out_shape=jax.ShapeDtypeStruct((bsz, s, aw), F32),
        scratch_shapes=[
            pltpu.VMEM((s, Q_BLOCK), I32),
            pltpu.VMEM((c_lat, hq), F32),
            pltpu.VMEM((kc, hq), F32),
            pltpu.VMEM((kc, hq), BF16),
            pltpu.VMEM((c_lat, hq), BF16),
            pltpu.VMEM((c_lat, hq), BF16),
            pltpu.VMEM((kc, Q_BLOCK), F32),
            pltpu.VMEM((1, Q_BLOCK), I32),
            pltpu.VMEM((32, s // 32, Q_BLOCK), I32),
        ],
        compiler_params=pltpu.CompilerParams(
            dimension_semantics=("arbitrary", "arbitrary"),
            vmem_limit_bytes=VMEM_LIMIT_BYTES),
        name="dsa_attention",
    )(q, qi, wi, ki2, lat, wuk_t, wuv_bd, bias_near, bias_far)


def _out_kernel(ya_ref, za_ref, yb_ref, zb_ref, ga_ref, gb_ref, x_ref, mod_ref,
                wpa_ref, wpb_ref, wo_ref, fg_ref, o_ref, *, d, final):
    za = za_ref[0].astype(F32)
    zb = zb_ref[0].astype(F32)
    ua = (ya_ref[0] * (za * _sigmoid(za))).astype(BF16)
    ub = (yb_ref[0] * (zb * _sigmoid(zb))).astype(BF16)
    br_a = _dot(ua, wpa_ref[...])
    br_b = _dot(ub, wpb_ref[...])
    merged = _sigmoid(ga_ref[0].astype(F32)) * br_a + _sigmoid(gb_ref[0].astype(F32)) * br_b
    gate = mod_ref[0, :, 2 * d:3 * d]
    xo = x_ref[0] + gate * _dot(merged.astype(BF16), wo_ref[...])
    if final:
        xo = xo * lax.rsqrt(jnp.mean(xo * xo, axis=-1, keepdims=True) + NORM_EPS) * fg_ref[...]
    o_ref[0] = xo


def _out_call(ya, za, yb, zb, ga, gb, x, mod, wpa, wpb, wo, final_g, *, final, tm):
    bsz, s, d = x.shape
    d3 = mod.shape[-1]
    row = lambda w: pl.BlockSpec((1, tm, w), lambda b, j: (b, j, 0))
    const = lambda shape: pl.BlockSpec(shape, lambda b, j: tuple(0 for _ in shape))
    return pl.pallas_call(
        functools.partial(_out_kernel, d=d, final=final),
        grid=(bsz, s // tm),
        in_specs=[
            row(ya.shape[-1]), row(za.shape[-1]), row(yb.shape[-1]), row(zb.shape[-1]),
            row(d), row(d), row(d),
            pl.BlockSpec((1, 1, d3), lambda b, j: (b, 0, 0)),
            const(wpa.shape), const(wpb.shape), const(wo.shape), const((1, d)),
        ],
        out_specs=row(d),
        out_shape=jax.ShapeDtypeStruct((bsz, s, d), F32),
        compiler_params=pltpu.CompilerParams(
            dimension_semantics=("arbitrary", "arbitrary"),
            vmem_limit_bytes=VMEM_LIMIT_BYTES),
        name="gated_out",
    )(ya, za, yb, zb, ga, gb, x, mod.reshape(bsz, 1, d3), wpa, wpb, wo, final_g.reshape(1, d))


def _t5_bucket(dist, num_buckets):
    max_exact = num_buckets // 2
    is_small = dist < max_exact
    dd = jnp.maximum(dist, 1).astype(F32)
    large = max_exact + (jnp.log(dd / max_exact) / math.log(MAX_DISTANCE / max_exact)
                         * (num_buckets - max_exact)).astype(I32)
    large = jnp.minimum(large, num_buckets - 1)
    return jnp.where(is_small, dist, large)


def _bias_tiles(rel_bias):
    nbk = rel_bias.shape[0]
    i = jnp.arange(Q_BLOCK, dtype=I32)[:, None]
    jj = jnp.arange(2 * Q_BLOCK, dtype=I32)[None, :]
    tiles = []
    for base in (Q_BLOCK, 0):
        bucket = _t5_bucket(jnp.maximum(base + i - jj, 0), nbk)
        onehot = (bucket[None, :, :] == jnp.arange(nbk, dtype=I32)[:, None, None]).astype(F32)
        tiles.append(jnp.einsum('nh,nij->hji', rel_bias, onehot,
                                precision=lax.Precision.HIGHEST))
    far = rel_bias[_t5_bucket(jnp.full((1, 1), MAX_DISTANCE, I32), nbk)]
    return jnp.stack(tiles, axis=0) * LOG2E, jnp.moveaxis(far, -1, 0) * LOG2E


def _block_diag_heads(w, rows_per_head, cols_per_head):
    n = w.shape[0]
    eye = jnp.eye(n, dtype=w.dtype)
    return jnp.einsum('hrc,hg->hrgc', w, eye).reshape(n * rows_per_head, n * cols_per_head)


def kernel(x, c, ada_w, ada_b, norm_g, w_in, shift_mu, w0, w2, a0, a2, k_k, k_a, r_k, lnx_g, lnx_b,
           kv_norm_g, w_uk, w_uv, w_pa, w_pb, w_o, rel_bias, final_g):
    bsz, s, d = x.shape
    depth = w_in.shape[0]
    rw = w0.shape[-1]
    c_lat, n_heads, dh = w_uk.shape[1:]
    aw = n_heads * dh
    iw = IDX_HEADS * IDX_HEAD_DIM
    n_shift = 3 * rw + DECAY_LORA + ICLR_LORA
    topk = min(TOPK_MAX, s // 4)
    assert DECAY_LORA + ICLR_LORA == LANES and dh == HEAD_DIM and IDX_HEAD_DIM == HEAD_DIM
    assert s % (2 * Q_BLOCK) == 0 and rw % LANES == 0

    sizes = (rw, rw, rw, DECAY_LORA, ICLR_LORA, rw, aw, c_lat, aw, iw, IDX_HEAD_DIM, IDX_HEADS, d, d)
    offs = [0]
    for sz in sizes:
        offs.append(offs[-1] + sz)
    col = lambda i: slice(offs[i], offs[i + 1])
    widths = (rw, aw, c_lat, aw, iw, LANES, LANES, d, d)

    mod = _mod_call(c, ada_w, ada_b)
    bias_near, bias_far = _bias_tiles(rel_bias)
    ones_bd = _block_diag_heads(jnp.ones((rw // HEAD_DIM, HEAD_DIM, HEAD_DIM), BF16), HEAD_DIM, HEAD_DIM)
    kc = 512 if s % 512 == 0 else 2 * Q_BLOCK
    tm_in = 256
    tm_out = 512 if s % 512 == 0 else 256

    for l in range(depth):
        w = w_in[l]
        wi_pad = jnp.zeros((d, LANES - IDX_HEADS), w.dtype)
        w_packed = jnp.concatenate(
            [w[:, 0:n_shift], w[:, col(5)], w[:, col(6)], w[:, col(7)], w[:, col(8)], w[:, col(9)],
             w[:, col(10)], w[:, col(10)], w[:, col(11)], wi_pad, w[:, col(12)], w[:, col(13)]],
            axis=1).astype(BF16)
        sh, za, q, lat, zb, qi, ki2, wi, ga, gb = _inproj_call(
            x, mod[l], norm_g[l], w_packed, shift_mu[l], kv_norm_g[l],
            n_shift=n_shift, widths=widths, tm=tm_in)

        vecs = jnp.stack([w0[l], a0[l], k_k[l], k_a[l], r_k[l].reshape(-1), lnx_g[l], lnx_b[l],
                          jnp.zeros((rw,), F32)], axis=0)
        w2p = jnp.concatenate([w2[l], jnp.zeros((ICLR_LORA, rw), F32)], axis=0)
        a2p = jnp.concatenate([jnp.zeros((DECAY_LORA, rw), F32), a2[l]], axis=0)
        ya = _rwkv_call(sh, vecs, w2p, a2p, ones_bd, width=rw, tb=4 * CHUNK)

        wuk_t = jnp.transpose(w_uk[l], (1, 0, 2)).astype(BF16)
        wuv_bd = _block_diag_heads(jnp.transpose(w_uv[l], (1, 0, 2)), c_lat, dh).astype(BF16)
        yb = _dsa_call(q, qi, wi, ki2, lat, wuk_t, wuv_bd, bias_near, bias_far, topk=topk, kc=kc)

        x = _out_call(ya, za, yb, zb, ga, gb, x, mod[l], w_pa[l].astype(BF16), w_pb[l].astype(BF16),
                      w_o[l].astype(BF16), final_g, final=(l == depth - 1), tm=tm_out)
    return x
```

```python
import functools
import math

import jax
import jax.numpy as jnp
from jax import lax
from jax.experimental import pallas as pl
from jax.experimental.pallas import tpu as pltpu

F32 = jnp.float32
BF16 = jnp.bfloat16
I32 = jnp.int32

DECAY_LORA = 64
ICLR_LORA = 64
GN_EPS = 64e-5
IDX_HEADS = 8
IDX_HEAD_DIM = 64
TOPK_MAX = 256
Q_BLOCK = 128
MAX_DISTANCE = 128
NORM_EPS = 1e-6

LANES = 128
VMEM_LIMIT_BYTES = 56 * 1024 * 1024

HEAD_DIM = 64
CHUNK = 64
INT_MIN = -(2 ** 31)
INT_MAX = 2 ** 31 - 1
NEG_MASK = -2e30
NEG_INIT = -1e30
LOG2E = 1.4426950408889634
BIT_GROUP = 256
ONES_ROWS = 16


def _sigmoid(x):
    return 1.0 / (1.0 + jnp.exp(-x))


def _dot(a, b, dims=(((1,), (0,)), ((), ()))):
    return lax.dot_general(a, b, dims, preferred_element_type=F32)


_NN = (((1,), (0,)), ((), ()))
_NT = (((1,), (1,)), ((), ()))
_TN = (((0,), (0,)), ((), ()))


def _split(x, n):
    parts = []
    rem = x
    for i in range(n):
        p = rem.astype(BF16)
        parts.append(p)
        if i + 1 < n:
            rem = rem - p.astype(F32)
    return parts


def _mm(a, b, passes, dims=_NN):
    if passes == 1:
        return _dot(a.astype(BF16), b.astype(BF16), dims)
    if passes == 3:
        a1, a2 = _split(a, 2)
        b1, b2 = _split(b, 2)
        return _dot(a1, b1, dims) + (_dot(a1, b2, dims) + _dot(a2, b1, dims))
    a1, a2, a3 = _split(a, 3)
    b1, b2, b3 = _split(b, 3)
    hi = _dot(a1, b1, dims)
    mid = _dot(a1, b2, dims) + _dot(a2, b1, dims)
    lo = _dot(a2, b2, dims) + (_dot(a1, b3, dims) + _dot(a3, b1, dims))
    return hi + (mid + lo)


def _mm_exact_rhs(a, b_bf16, n):
    out = None
    for p in _split(a, n):
        t = _dot(p, b_bf16)
        out = t if out is None else out + t
    return out


def _mod_kernel(c_ref, w_ref, b_ref, o_ref):
    c = c_ref[...]
    ca = c * _sigmoid(c)
    o_ref[0] = _mm(ca, w_ref[0], 6) + b_ref[0]


def _mod_call(c, ada_w, ada_b):
    depth, d, d3 = ada_w.shape
    bsz = c.shape[0]
    tn = d
    return pl.pallas_call(
        _mod_kernel,
        grid=(depth, d3 // tn),
        in_specs=[
            pl.BlockSpec((bsz, d), lambda l, n: (0, 0)),
            pl.BlockSpec((1, d, tn), lambda l, n: (l, 0, n)),
            pl.BlockSpec((1, 1, tn), lambda l, n: (l, 0, n)),
        ],
        out_specs=pl.BlockSpec((1, bsz, tn), lambda l, n: (l, 0, n)),
        out_shape=jax.ShapeDtypeStruct((depth, bsz, d3), F32),
        compiler_params=pltpu.CompilerParams(
            dimension_semantics=("arbitrary", "arbitrary"),
            vmem_limit_bytes=VMEM_LIMIT_BYTES),
        name="adaln_mod",
    )(c, ada_w, ada_b.reshape(depth, 1, d3))


def _inproj_kernel(x_ref, mod_ref, g_ref, w_ref, mu_ref, kvg_ref,
                   sh_ref, za_ref, q_ref, lat_ref, zb_ref, qi_ref, ki_ref, wi_ref, ga_ref, gb_ref,
                   carry_ref, *, d, n_shift, widths):
    j = pl.program_id(1)
    xb = x_ref[0]
    tm = xb.shape[0]
    shift = mod_ref[0, :, 0:d]
    scale = mod_ref[0, :, d:2 * d]
    ms = jnp.mean(xb * xb, axis=-1, keepdims=True)
    h = xb * lax.rsqrt(ms + NORM_EPS) * g_ref[...]
    h = h * (1.0 + scale) + shift
    hb = h.astype(BF16)

    @pl.when(j == 0)
    def _():
        carry_ref[...] = jnp.zeros_like(carry_ref)

    ps = _dot(hb, w_ref[:, 0:n_shift])
    prev = pltpu.roll(ps, 1, 0)
    row = lax.broadcasted_iota(I32, (tm, 1), 0)
    prev = jnp.where(row == 0, carry_ref[...], prev)
    carry_ref[...] = ps[tm - 1:tm, :]
    sh_ref[0] = ps + (prev - ps) * mu_ref[...]

    off = n_shift
    outs = (za_ref, q_ref, lat_ref, zb_ref, qi_ref, ki_ref, wi_ref, ga_ref, gb_ref)
    for o_ref, wd in zip(outs, widths):
        p = _dot(hb, w_ref[:, off:off + wd])
        if o_ref is lat_ref:
            p = p * lax.rsqrt(jnp.mean(p * p, axis=-1, keepdims=True) + NORM_EPS) * kvg_ref[...]
        o_ref[0] = p.astype(o_ref.dtype)
        off += wd


def _inproj_call(x, mod, norm_g, w_packed, mu, kvg, *, n_shift, widths, tm):
    bsz, s, d = x.shape
    npk = w_packed.shape[1]
    d3 = mod.shape[-1]
    row_spec = lambda w: pl.BlockSpec((1, tm, w), lambda b, j: (b, j, 0))
    const = lambda shape: pl.BlockSpec(shape, lambda b, j: tuple(0 for _ in shape))
    out_widths = (n_shift,) + tuple(widths)
    out_dtypes = (F32, BF16, BF16, BF16, BF16, BF16, BF16, F32, BF16, BF16)
    return pl.pallas_call(
        functools.partial(_inproj_kernel, d=d, n_shift=n_shift, widths=tuple(widths)),
        grid=(bsz, s // tm),
        in_specs=[
            row_spec(d),
            pl.BlockSpec((1, 1, d3), lambda b, j: (b, 0, 0)),
            const((1, d)),
            pl.BlockSpec((d, npk), lambda b, j: (0, 0), pipeline_mode=pl.Buffered(1)),
            const((1, n_shift)),
            const((1, widths[2])),
        ],
        out_specs=[row_spec(w) for w in out_widths],
        out_shape=[jax.ShapeDtypeStruct((bsz, s, w), dt) for w, dt in zip(out_widths, out_dtypes)],
        scratch_shapes=[pltpu.VMEM((1, n_shift), F32)],
        compiler_params=pltpu.CompilerParams(
            dimension_semantics=("arbitrary", "arbitrary"),
            vmem_limit_bytes=VMEM_LIMIT_BYTES),
        name="inproj",
    )(x, mod.reshape(bsz, 1, d3), norm_g.reshape(1, d), w_packed, mu.reshape(1, n_shift),
      kvg.reshape(1, -1))


def _rwkv_kernel(sh_ref, vec_ref, w2_ref, a2_ref, ones_ref, out_ref, st_ref,
                 *, width, p_chain, p_inv, p_state):
    j = pl.program_id(1)
    L = CHUNK
    n_pairs = width // LANES

    @pl.when(j == 0)
    def _():
        st_ref[...] = jnp.zeros_like(st_ref)

    blk = sh_ref[0]
    pr = blk[:, 0:width]
    pk = blk[:, width:2 * width]
    pv = blk[:, 2 * width:3 * width]
    lora = blk[:, 3 * width:3 * width + LANES]
    lane = lax.broadcasted_iota(I32, (1, LANES), 1)
    lo_half = lane < HEAD_DIM
    t_in = jnp.where(lo_half, jnp.tanh(lora), lora)
    dec_in = _mm(t_in, w2_ref[...], 3)
    a_in = _mm(t_in, a2_ref[...], 3)

    w0 = vec_ref[0:1, :]
    a0 = vec_ref[1:2, :]
    k_k = vec_ref[2:3, :]
    k_a = vec_ref[3:4, :]
    r_k = vec_ref[4:5, :]
    lnx_g = vec_ref[5:6, :]
    lnx_b = vec_ref[6:7, :]
    ones_bd = ones_ref[...]

    z = -(w0 + dec_in)
    sp = jnp.maximum(z, 0.0) + jnp.log(1.0 + jnp.exp(-jnp.abs(z)))
    logw = -jnp.exp(-sp - 0.5)
    a = _sigmoid(a0 + a_in)
    kk0 = pk * k_k
    ss = _mm_exact_rhs(kk0 * kk0, ones_bd, 2)
    kk = kk0 / jnp.maximum(jnp.sqrt(ss), 1e-12)
    k_eff = pk * (1.0 + (a - 1.0) * k_a)
    a_vec = -kk
    b_vec = kk * a

    tb = blk.shape[0]
    n_chunks = tb // L
    r_i = lax.broadcasted_iota(I32, (tb, tb), 0)
    c_i = lax.broadcasted_iota(I32, (tb, tb), 1)
    tril = jnp.where(((r_i // L) == (c_i // L)) & (c_i <= r_i), 1.0, 0.0).astype(BF16)
    cl = None
    for part in _split(logw, 3):
        t = _dot(tril, part)
        cl = t if cl is None else cl + t
    cl_last = jnp.concatenate(
        [jnp.broadcast_to(cl[c * L + L - 1:c * L + L, :], (L, width)) for c in range(n_chunks)], axis=0)
    rt = pr * jnp.exp(cl)
    at = a_vec * jnp.exp(cl - logw)
    e_inv = jnp.exp(-cl)
    bt = b_vec * e_inv
    kt = k_eff * e_inv
    e_l = jnp.exp(cl_last - cl)
    bl = b_vec * e_l
    kl = k_eff * e_l
    p_last = jnp.exp(cl_last)

    rr = lax.broadcasted_iota(I32, (LANES, LANES), 0)
    cc = lax.broadcasted_iota(I32, (LANES, LANES), 1)
    same = (rr // L) == (cc // L)
    strict = same & ((cc % L) < (rr % L))
    incl = same & ((cc % L) <= (rr % L))
    eye = rr == cc
    eye_f = jnp.where(eye, 1.0, 0.0)
    rr2 = lax.broadcasted_iota(I32, (LANES, 2 * LANES), 0)
    cc2 = lax.broadcasted_iota(I32, (LANES, 2 * LANES), 1)
    incl2 = ((rr2 // L) == ((cc2 % LANES) // L)) & ((cc2 % L) <= (rr2 % L))

    def stack(x):
        return jnp.concatenate([jnp.where(lo_half, x, 0.0), jnp.where(lo_half, 0.0, x)], axis=0)

    chains = [(c, p) for c in range(n_chunks) for p in range(n_pairs)]

    def piece(x, c, p):
        return x[c * L:(c + 1) * L, p * LANES:(p + 1) * LANES]

    atm, rtm, vm, blkl, a_ab, a_ak, a_rbk = {}, {}, {}, {}, {}, {}, {}
    for ch in chains:
        atm[ch], rtm[ch] = stack(piece(at, *ch)), stack(piece(rt, *ch))
        vm[ch] = stack(piece(pv, *ch)).astype(BF16)
        btm, ktm = stack(piece(bt, *ch)), stack(piece(kt, *ch))
        blkl[ch] = jnp.concatenate([stack(piece(bl, *ch)), stack(piece(kl, *ch))], axis=0).astype(BF16)
        g = _mm(jnp.concatenate([atm[ch], rtm[ch]], axis=0), jnp.concatenate([btm, ktm], axis=0),
                p_chain, _NT)
        a_ab[ch] = jnp.where(strict, g[0:LANES, 0:LANES], 0.0)
        a_ak[ch] = jnp.where(strict, g[0:LANES, LANES:], 0.0)
        a_rbk[ch] = jnp.where(incl2, g[LANES:, :], 0.0).astype(BF16)
    tinv = {ch: eye_f + a_ab[ch] for ch in chains}
    qn = dict(a_ab)
    sq = 1
    while 2 * sq < L:
        for ch in chains:
            qn[ch] = _mm(qn[ch], qn[ch], p_inv)
        for ch in chains:
            tinv[ch] = tinv[ch] + _mm(tinv[ch], qn[ch], p_inv)
        sq *= 2
    akv = {ch: _mm(a_ak[ch], vm[ch], p_chain) for ch in chains}
    wz = {ch: _mm(tinv[ch], jnp.concatenate([atm[ch], akv[ch]], axis=1), p_inv) for ch in chains}
    wm = {ch: wz[ch][:, 0:LANES].astype(BF16) for ch in chains}
    zv = {ch: jnp.concatenate([wz[ch][:, LANES:].astype(BF16), vm[ch]], axis=0) for ch in chains}
    rqm = {ch: rtm[ch] + _mm(a_rbk[ch][:, 0:LANES], wm[ch], p_chain) for ch in chains}
    y0m = {ch: _mm(a_rbk[ch], zv[ch], p_chain) for ch in chains}
    mt = {ch: jnp.where(eye, piece(p_last, *ch)[0:1], 0.0)
          + _mm(blkl[ch][0:LANES], wm[ch], p_state, _TN) for ch in chains}
    ct = {ch: _mm(blkl[ch], zv[ch], p_state, _TN) for ch in chains}

    st = {p: st_ref[p] for p in range(n_pairs)}
    y_rows = []
    for c in range(n_chunks):
        ys = []
        for p in range(n_pairs):
            ym = _mm(rqm[(c, p)], st[p], p_state) + y0m[(c, p)]
            ys.append(ym[0:L] + ym[L:])
            st[p] = _mm(mt[(c, p)], st[p], p_state) + ct[(c, p)]
        y_rows.append(jnp.concatenate(ys, axis=1))
    for p in range(n_pairs):
        st_ref[p] = st[p]
    y = jnp.concatenate(y_rows, axis=0)

    inv_n = 1.0 / HEAD_DIM
    mu = _mm_exact_rhs(y, ones_bd, 2) * inv_n
    dlt = y - mu
    var = _mm_exact_rhs(dlt * dlt, ones_bd, 2) * inv_n
    yn = dlt * lax.rsqrt(var + GN_EPS) * lnx_g + lnx_b
    bonus = _mm_exact_rhs(pr * k_eff * r_k, ones_bd, 2) * pv
    out_ref[0] = yn + bonus


def _rwkv_call(sh, vecs, w2p, a2p, ones_bd, *, width, tb, p_chain=1, p_inv=1, p_state=1):
    bsz, s, n_shift = sh.shape
    n_pairs = width // LANES
    const = lambda shape: pl.BlockSpec(shape, lambda b, j: tuple(0 for _ in shape))
    return pl.pallas_call(
        functools.partial(_rwkv_kernel, width=width, p_chain=p_chain, p_inv=p_inv, p_state=p_state),
        grid=(bsz, s // tb),
        in_specs=[
            pl.BlockSpec((1, tb, n_shift), lambda b, j: (b, j, 0)),
            const(vecs.shape),
            const(w2p.shape),
            const(a2p.shape),
            const(ones_bd.shape),
        ],
        out_specs=pl.BlockSpec((1, tb, width), lambda b, j: (b, j, 0)),
        out_shape=jax.ShapeDtypeStruct((bsz, s, width), F32),
        scratch_shapes=[pltpu.VMEM((n_pairs, LANES, LANES), F32)],
        compiler_params=pltpu.CompilerParams(
            dimension_semantics=("arbitrary", "arbitrary"),
            vmem_limit_bytes=VMEM_LIMIT_BYTES),
        name="rwkv7_scan",
    )(sh, vecs, w2p, a2p, ones_bd)


def _bit_transpose32(a):
    a = list(a)
    m, j = 0x0000FFFF, 16
    while j:
        k = 0
        while k < 32:
            t = (a[k] ^ lax.shift_right_logical(a[k + j], jnp.int32(j))) & m
            a[k] = a[k] ^ t
            a[k + j] = a[k + j] ^ lax.shift_left(t, jnp.int32(j))
            k = (k + j + 1) & ~j
        j >>= 1
        m = (m ^ (m << j)) & 0xFFFFFFFF
    return a


def _dsa_kernel(q_ref, qi_ref, wi_ref, ki_ref, lat_ref, wukt_ref, wuv_ref, bias_ref, far_ref, out_ref,
                key_ref, acc_ref, lg_ref, p_ref, w_ref, wqk_ref, madd_ref, mt_ref, planes_ref,
                *, topk, kc, n_heads, seq, scale):
    qb = pl.program_id(1)
    QB = Q_BLOCK
    NEAR = 2 * QB
    HQ = n_heads * QB
    c_lat = lat_ref.shape[-1]
    q_end = (qb + 1) * QB
    n_idx = (jnp.maximum(q_end, NEAR) + kc - 1) // kc
    tpos = qb * QB + lax.broadcasted_iota(I32, (1, QB), 1)

    def head(x, h):
        return x[:, h * QB:(h + 1) * QB]

    qi_t = qi_ref[0].T
    qi_w = jnp.concatenate([qi_t[h * HEAD_DIM:(h + 1) * HEAD_DIM, :] for h in range(n_heads)], axis=1)
    w_ref[...] = jnp.concatenate([qi_w, jnp.zeros_like(qi_w)], axis=0).astype(BF16)
    wi_t = wi_ref[0].T

    def idx_body(c, carry):
        ks = pl.multiple_of(c * kc, kc)
        kib = ki_ref[0, pl.ds(ks, kc), :].astype(BF16)
        score = None
        for g in range(n_heads // 2):
            s_g = jnp.maximum(_dot(kib, w_ref[:, 2 * g * QB:(2 * g + 2) * QB]), 0.0)
            part = wi_t[2 * g:2 * g + 1, :] * s_g[:, 0:QB] + wi_t[2 * g + 1:2 * g + 2, :] * s_g[:, QB:]
            score = part if score is None else score + part
        score = score + 0.0
        bits = pltpu.bitcast(score, I32)
        key = bits ^ ((bits >> 31) & INT_MAX)
        spos = ks + lax.broadcasted_iota(I32, (kc, 1), 0)
        key = jnp.where(spos <= tpos, key, INT_MIN)
        key_ref[pl.ds(ks, kc), :] = key
        ukey = key ^ INT_MIN
        for g in range(kc // BIT_GROUP):
            regs = _bit_transpose32([ukey[g * BIT_GROUP + 8 * i:g * BIT_GROUP + 8 * i + 8, :]
                                     for i in range(32)])
            row0 = pl.multiple_of(c * (kc // 32) + 8 * g, 8)
            for b in range(32):
                planes_ref[b, pl.ds(row0, 8), :] = regs[31 - b]
        return carry

    @pl.when((pl.program_id(0) == 0) & (qb == 0))
    def _():
        planes_ref[...] = jnp.zeros(planes_ref.shape, I32)

    lax.fori_loop(0, n_idx, idx_body, 0)

    q_t = q_ref[0].T.astype(BF16)
    for h in range(n_heads):
        qa = _dot(wukt_ref[h], q_t[h * HEAD_DIM:(h + 1) * HEAD_DIM, :])
        wqk_ref[:, h * QB:(h + 1) * QB] = (qa * (scale * LOG2E)).astype(BF16)
    far_end = jnp.maximum(qb - 1, 0) * QB
    n_far = (far_end + kc - 1) // kc
    near_start = pl.multiple_of(far_end, QB)
    near_ld = pl.multiple_of(jnp.minimum(near_start, seq - kc), QB)
    near_off = pl.multiple_of(near_start - near_ld, QB)
    ks0 = pl.multiple_of(jnp.where(n_far > 0, 0, near_ld), QB)
    lat0 = lat_ref[0, pl.ds(ks0, kc), :].astype(BF16)
    for g in range(n_heads // 2):
        cols = slice(2 * g * QB, (2 * g + 2) * QB)
        lg_ref[:, cols] = _dot(lat0, wqk_ref[:, cols])

    n_rows = seq // 32
    valid = lax.broadcasted_iota(I32, (n_rows, 1), 0) < n_idx * (kc // 32)
    kf = float(topk)

    def bit_body(i, carry):
        alive, c_gt, thr_u = carry
        b = 31 - i
        w = alive & planes_ref[b]
        cnt = jnp.sum(lax.population_count(w).astype(F32), axis=0, keepdims=True)
        tot = c_gt + cnt
        take = tot >= kf
        alive = jnp.where(take, w, alive ^ w)
        c_gt = jnp.where(take, c_gt, tot)
        thr_u = jnp.where(take, thr_u | lax.shift_left(jnp.int32(1), b), thr_u)
        return alive, c_gt, thr_u

    alive, cnt_gt, thr_u = lax.fori_loop(
        0, 32, bit_body,
        (jnp.broadcast_to(jnp.where(valid, -1, 0), (n_rows, QB)).astype(I32),
         jnp.zeros((1, QB), F32), jnp.zeros((1, QB), I32)))
    thr = thr_u ^ INT_MIN
    cnt_ge = cnt_gt + jnp.sum(lax.population_count(alive).astype(F32), axis=0, keepdims=True)

    def count(pred):
        def body(c, acc):
            ks = pl.multiple_of(c * kc, kc)
            kch = key_ref[pl.ds(ks, kc), :]
            row = ks + lax.broadcasted_iota(I32, (kc, 1), 0)
            hit = jnp.where(pred(kch, row), 1.0, 0.0).reshape(kc // 8, 8, QB)
            parts = [hit[i] for i in range(kc // 8)]
            while len(parts) > 1:
                parts = [parts[i] + parts[i + 1] for i in range(0, len(parts), 2)]
            return acc + parts[0]
        acc = lax.fori_loop(0, n_idx, body, jnp.zeros((8, QB), F32))
        return jnp.sum(acc, axis=0, keepdims=True)

    need = kf - cnt_gt
    excess = jnp.where((cnt_ge > kf) & (thr > INT_MIN), 1.0, 0.0)
    mt_ref[...] = jnp.full((1, QB), INT_MAX, I32)

    @pl.when(jnp.max(excess) > 0.0)
    def _():
        nbits = max(1, (seq - 1).bit_length())

        def tie_body(i, mcur):
            cand = mcur | lax.shift_left(jnp.int32(1), nbits - 1 - i)
            cnt = count(lambda kch, row: (kch == thr) & (row < cand))
            return jnp.where(cnt < need, cand, mcur)

        mt_ref[...] = lax.fori_loop(0, nbits, tie_body, jnp.zeros((1, QB), I32))

    mtie = jnp.where(thr == INT_MIN, -1, mt_ref[...])

    def selected(ks, width, hi_limit):
        kch = key_ref[pl.ds(ks, width), :]
        row = ks + lax.broadcasted_iota(I32, (width, 1), 0)
        sel = (kch > thr) | ((kch == thr) & (row <= mtie))
        if hi_limit is not None:
            sel = sel & (row < hi_limit)
        return sel

    pair_cols = [slice(2 * g * QB, (2 * g + 2) * QB) for g in range(n_heads // 2)]

    def lat_chunk(ks, width):
        return lat_ref[0, pl.ds(ks, width), :].astype(BF16)

    acc_ref[...] = jnp.zeros(acc_ref.shape, F32)
    p_ref[...] = jnp.zeros(p_ref.shape, BF16)


    def lat_t_ones(ks, width):
        lat_t = lat_ref[0, pl.ds(ks, width), :].T.astype(BF16)
        return jnp.concatenate([lat_t, jnp.ones((ONES_ROWS, width), BF16)], axis=0)

    def far_body(c, ms):
        ksp = pl.multiple_of(jnp.maximum(c - 1, 0) * kc, kc)
        latp_t = lat_t_ones(ksp, kc)
        ksn = pl.multiple_of(jnp.where(c + 1 < n_far, (c + 1) * kc, near_ld), QB)
        latn = lat_chunk(ksn, kc)
        ks = pl.multiple_of(c * kc, kc)
        madd_ref[...] = jnp.where(selected(ks, kc, far_end), 0.0, NEG_MASK)
        new_ms = list(ms)
        for g, cols in enumerate(pair_cols):
            pv_g = _dot(latp_t, p_ref[:, cols])
            lg_next = _dot(latn, wqk_ref[:, cols])
            for h in (2 * g, 2 * g + 1):
                hc = slice(h * QB, (h + 1) * QB)
                bfar = far_ref[h]
                lgm = lg_ref[:, hc] + madd_ref[...]
                m_new = jnp.maximum(ms[h], jnp.max(lgm, axis=0, keepdims=True) + bfar)
                p = jnp.exp2(lgm - (m_new - bfar))
                alpha = jnp.exp2(ms[h] - m_new)
                new_ms[h] = m_new
                p_ref[:, hc] = p.astype(BF16)
                acc_ref[:, hc] = (acc_ref[:, hc] + pv_g[:, (h - 2 * g) * QB:(h - 2 * g + 1) * QB]) * alpha
            lg_ref[:, cols] = lg_next
        return tuple(new_ms)

    ms = lax.fori_loop(0, n_far, far_body,
                       tuple(jnp.full((1, QB), NEG_INIT, F32) for _ in range(n_heads)))
    ks_last = pl.multiple_of(jnp.maximum(n_far - 1, 0) * kc, kc)
    pv = _dot(lat_t_ones(ks_last, kc), p_ref[...])

    tile_id = jnp.where(qb == 0, 1, 0)
    lgn = lg_ref[pl.ds(near_off, NEAR), :]
    sel = selected(near_start, NEAR, None)
    ps, alphas = [], []
    for h in range(n_heads):
        lgm = jnp.where(sel, head(lgn, h) + bias_ref[tile_id, h], NEG_MASK)
        m_new = jnp.maximum(ms[h], jnp.max(lgm, axis=0, keepdims=True))
        alphas.append(jnp.exp2(ms[h] - m_new))
        ps.append(jnp.exp2(lgm - m_new).astype(BF16))
    pvn = _dot(lat_t_ones(near_start, NEAR), jnp.concatenate(ps, axis=1))
    o_parts = []
    for h in range(n_heads):
        tot = (acc_ref[:, h * QB:(h + 1) * QB] + head(pv, h)) * alphas[h] + head(pvn, h)
        o_h = tot[0:c_lat] / tot[c_lat:c_lat + 1]
        o_parts.append(o_h.astype(BF16))
    out_ref[0] = _dot(jnp.concatenate(o_parts, axis=0), wuv_ref[...], _TN)


def _dsa_call(q, qi, wi, ki2, lat, wuk_t, wuv_bd, bias_near, bias_far, *, topk, kc):
    bsz, s, aw = q.shape
    n_heads = aw // HEAD_DIM
    c_lat = lat.shape[-1]
    nb = s // Q_BLOCK
    hq = n_heads * Q_BLOCK
    const = lambda shape: pl.BlockSpec(shape, lambda b, j: tuple(0 for _ in shape))
    blk = lambda w: pl.BlockSpec((1, Q_BLOCK, w), lambda b, j: (b, j, 0))
    full = lambda w: pl.BlockSpec((1, s, w), lambda b, j: (b, 0, 0))
    return pl.pallas_call(
        functools.partial(_dsa_kernel, topk=topk, kc=kc, n_heads=n_heads, seq=s,
                          scale=float(HEAD_DIM) ** -0.5),
        grid=(bsz, nb),
        in_specs=[
            blk(aw), blk(qi.shape[-1]), blk(LANES), full(LANES), full(c_lat),
            const(wuk_t.shape), const(wuv_bd.shape), const(bias_near.shape), const(bias_far.shape),
        ],
        out_specs=blk(aw),
        out_shape=jax.ShapeDtypeStruct((bsz, s, aw), F32),
        scratch_shapes=[
            pltpu.VMEM((s, Q_BLOCK), I32),
            pltpu.VMEM((c_lat + ONES_ROWS, hq), F32),
            pltpu.VMEM((kc, hq), F32),
            pltpu.VMEM((kc, hq), BF16),
            pltpu.VMEM((c_lat, hq), BF16),
            pltpu.VMEM((c_lat, hq), BF16),
            pltpu.VMEM((kc, Q_BLOCK), F32),
            pltpu.VMEM((1, Q_BLOCK), I32),
            pltpu.VMEM((32, s // 32, Q_BLOCK), I32),
        ],
        compiler_params=pltpu.CompilerParams(
            dimension_semantics=("arbitrary", "arbitrary"),
            vmem_limit_bytes=VMEM_LIMIT_BYTES),
        name="dsa_attention",
    )(q, qi, wi, ki2, lat, wuk_t, wuv_bd, bias_near, bias_far)


def _out_kernel(ya_ref, za_ref, yb_ref, zb_ref, ga_ref, gb_ref, x_ref, mod_ref,
                wpa_ref, wpb_ref, wo_ref, fg_ref, o_ref, *, d, final):
    za = za_ref[0].astype(F32)
    zb = zb_ref[0].astype(F32)
    ua = (ya_ref[0] * (za * _sigmoid(za))).astype(BF16)
    ub = (yb_ref[0] * (zb * _sigmoid(zb))).astype(BF16)
    br_a = _dot(ua, wpa_ref[...])
    br_b = _dot(ub, wpb_ref[...])
    merged = _sigmoid(ga_ref[0].astype(F32)) * br_a + _sigmoid(gb_ref[0].astype(F32)) * br_b
    gate = mod_ref[0, :, 2 * d:3 * d]
    xo = x_ref[0] + gate * _dot(merged.astype(BF16), wo_ref[...])
    if final:
        xo = xo * lax.rsqrt(jnp.mean(xo * xo, axis=-1, keepdims=True) + NORM_EPS) * fg_ref[...]
    o_ref[0] = xo


def _out_call(ya, za, yb, zb, ga, gb, x, mod, wpa, wpb, wo, final_g, *, final, tm):
    bsz, s, d = x.shape
    d3 = mod.shape[-1]
    row = lambda w: pl.BlockSpec((1, tm, w), lambda b, j: (b, j, 0))
    const = lambda shape: pl.BlockSpec(shape, lambda b, j: tuple(0 for _ in shape))
    return pl.pallas_call(
        functools.partial(_out_kernel, d=d, final=final),
        grid=(bsz, s // tm),
        in_specs=[
            row(ya.shape[-1]), row(za.shape[-1]), row(yb.shape[-1]), row(zb.shape[-1]),
            row(d), row(d), row(d),
            pl.BlockSpec((1, 1, d3), lambda b, j: (b, 0, 0)),
            const(wpa.shape), const(wpb.shape), const(wo.shape), const((1, d)),
        ],
        out_specs=row(d),
        out_shape=jax.ShapeDtypeStruct((bsz, s, d), F32),
        compiler_params=pltpu.CompilerParams(
            dimension_semantics=("arbitrary", "arbitrary"),
            vmem_limit_bytes=VMEM_LIMIT_BYTES),
        name="gated_out",
    )(ya, za, yb, zb, ga, gb, x, mod.reshape(bsz, 1, d3), wpa, wpb, wo, final_g.reshape(1, d))


def _t5_bucket(dist, num_buckets):
    max_exact = num_buckets // 2
    is_small = dist < max_exact
    dd = jnp.maximum(dist, 1).astype(F32)
    large = max_exact + (jnp.log(dd / max_exact) / math.log(MAX_DISTANCE / max_exact)
                         * (num_buckets - max_exact)).astype(I32)
    large = jnp.minimum(large, num_buckets - 1)
    return jnp.where(is_small, dist, large)


def _bias_tiles(rel_bias):
    nbk = rel_bias.shape[0]
    i = jnp.arange(Q_BLOCK, dtype=I32)[:, None]
    jj = jnp.arange(2 * Q_BLOCK, dtype=I32)[None, :]
    tiles = []
    for base in (Q_BLOCK, 0):
        bucket = _t5_bucket(jnp.maximum(base + i - jj, 0), nbk)
        onehot = (bucket[None, :, :] == jnp.arange(nbk, dtype=I32)[:, None, None]).astype(F32)
        tiles.append(jnp.einsum('nh,nij->hji', rel_bias, onehot,
                                precision=lax.Precision.HIGHEST))
    far = rel_bias[_t5_bucket(jnp.full((1, 1), MAX_DISTANCE, I32), nbk)]
    return jnp.stack(tiles, axis=0) * LOG2E, jnp.moveaxis(far, -1, 0) * LOG2E


def _block_diag_heads(w, rows_per_head, cols_per_head):
    n = w.shape[0]
    eye = jnp.eye(n, dtype=w.dtype)
    return jnp.einsum('hrc,hg->hrgc', w, eye).reshape(n * rows_per_head, n * cols_per_head)


def kernel(x, c, ada_w, ada_b, norm_g, w_in, shift_mu, w0, w2, a0, a2, k_k, k_a, r_k, lnx_g, lnx_b,
           kv_norm_g, w_uk, w_uv, w_pa, w_pb, w_o, rel_bias, final_g):
    bsz, s, d = x.shape
    depth = w_in.shape[0]
    rw = w0.shape[-1]
    c_lat, n_heads, dh = w_uk.shape[1:]
    aw = n_heads * dh
    iw = IDX_HEADS * IDX_HEAD_DIM
    n_shift = 3 * rw + DECAY_LORA + ICLR_LORA
    topk = min(TOPK_MAX, s // 4)
    assert DECAY_LORA + ICLR_LORA == LANES and dh == HEAD_DIM and IDX_HEAD_DIM == HEAD_DIM
    assert s % (2 * Q_BLOCK) == 0 and rw % LANES == 0

    sizes = (rw, rw, rw, DECAY_LORA, ICLR_LORA, rw, aw, c_lat, aw, iw, IDX_HEAD_DIM, IDX_HEADS, d, d)
    offs = [0]
    for sz in sizes:
        offs.append(offs[-1] + sz)
    col = lambda i: slice(offs[i], offs[i + 1])
    widths = (rw, aw, c_lat, aw, iw, LANES, LANES, d, d)

    mod = _mod_call(c, ada_w, ada_b)
    bias_near, bias_far = _bias_tiles(rel_bias)
    ones_bd = _block_diag_heads(jnp.ones((rw // HEAD_DIM, HEAD_DIM, HEAD_DIM), BF16), HEAD_DIM, HEAD_DIM)
    kc = 512 if s % 512 == 0 else 2 * Q_BLOCK
    tm_in = 256
    tm_out = 512 if s % 512 == 0 else 256

    for l in range(depth):
        w = w_in[l]
        wi_pad = jnp.zeros((d, LANES - IDX_HEADS), w.dtype)
        w_packed = jnp.concatenate(
            [w[:, 0:n_shift], w[:, col(5)], w[:, col(6)], w[:, col(7)], w[:, col(8)], w[:, col(9)],
             w[:, col(10)], w[:, col(10)], w[:, col(11)], wi_pad, w[:, col(12)], w[:, col(13)]],
            axis=1).astype(BF16)
        sh, za, q, lat, zb, qi, ki2, wi, ga, gb = _inproj_call(
            x, mod[l], norm_g[l], w_packed, shift_mu[l], kv_norm_g[l],
            n_shift=n_shift, widths=widths, tm=tm_in)

        vecs = jnp.stack([w0[l], a0[l], k_k[l], k_a[l], r_k[l].reshape(-1), lnx_g[l], lnx_b[l],
                          jnp.zeros((rw,), F32)], axis=0)
        w2p = jnp.concatenate([w2[l], jnp.zeros((ICLR_LORA, rw), F32)], axis=0)
        a2p = jnp.concatenate([jnp.zeros((DECAY_LORA, rw), F32), a2[l]], axis=0)
        ya = _rwkv_call(sh, vecs, w2p, a2p, ones_bd, width=rw, tb=4 * CHUNK)

        wuk_t = jnp.transpose(w_uk[l], (1, 0, 2)).astype(BF16)
        wuv_bd = _block_diag_heads(jnp.transpose(w_uv[l], (1, 0, 2)), c_lat, dh).astype(BF16)
        yb = _dsa_call(q, qi, wi, ki2, lat, wuk_t, wuv_bd, bias_near, bias_far, topk=topk, kc=kc)

        x = _out_call(ya, za, yb, zb, ga, gb, x, mod[l], w_pa[l].astype(BF16), w_pb[l].astype(BF16),
                      w_o[l].astype(BF16), final_g, final=(l == depth - 1), tm=tm_out)
    return x
```

```python
import functools
import math

import jax
import jax.numpy as jnp
from jax import lax
from jax.experimental import pallas as pl
from jax.experimental.pallas import tpu as pltpu

F32 = jnp.float32
BF16 = jnp.bfloat16
I32 = jnp.int32

DECAY_LORA = 64
ICLR_LORA = 64
GN_EPS = 64e-5
IDX_HEADS = 8
IDX_HEAD_DIM = 64
TOPK_MAX = 256
Q_BLOCK = 128
MAX_DISTANCE = 128
NORM_EPS = 1e-6

LANES = 128
VMEM_LIMIT_BYTES = 56 * 1024 * 1024

HEAD_DIM = 64
CHUNK = 64
INT_MIN = -(2 ** 31)
INT_MAX = 2 ** 31 - 1
NEG_MASK = -2e30
NEG_INIT = -1e30
LOG2E = 1.4426950408889634
BIT_GROUP = 256
ONES_ROWS = 16


def _sigmoid(x):
    return 1.0 / (1.0 + jnp.exp(-x))


def _dot(a, b, dims=(((1,), (0,)), ((), ()))):
    return lax.dot_general(a, b, dims, preferred_element_type=F32)


_NN = (((1,), (0,)), ((), ()))
_NT = (((1,), (1,)), ((), ()))
_TN = (((0,), (0,)), ((), ()))


def _split(x, n):
    parts = []
    rem = x
    for i in range(n):
        p = rem.astype(BF16)
        parts.append(p)
        if i + 1 < n:
            rem = rem - p.astype(F32)
    return parts


def _mm(a, b, passes, dims=_NN):
    if passes == 1:
        return _dot(a.astype(BF16), b.astype(BF16), dims)
    if passes == 3:
        a1, a2 = _split(a, 2)
        b1, b2 = _split(b, 2)
        return _dot(a1, b1, dims) + (_dot(a1, b2, dims) + _dot(a2, b1, dims))
    a1, a2, a3 = _split(a, 3)
    b1, b2, b3 = _split(b, 3)
    hi = _dot(a1, b1, dims)
    mid = _dot(a1, b2, dims) + _dot(a2, b1, dims)
    lo = _dot(a2, b2, dims) + (_dot(a1, b3, dims) + _dot(a3, b1, dims))
    return hi + (mid + lo)


def _mm_exact_rhs(a, b_bf16, n):
    out = None
    for p in _split(a, n):
        t = _dot(p, b_bf16)
        out = t if out is None else out + t
    return out


def _mod_kernel(c_ref, w_ref, b_ref, o_ref):
    c = c_ref[...]
    ca = c * _sigmoid(c)
    o_ref[0] = _mm(ca, w_ref[0], 6) + b_ref[0]


def _mod_call(c, ada_w, ada_b):
    depth, d, d3 = ada_w.shape
    bsz = c.shape[0]
    tn = d
    return pl.pallas_call(
        _mod_kernel,
        grid=(depth, d3 // tn),
        in_specs=[
            pl.BlockSpec((bsz, d), lambda l, n: (0, 0)),
            pl.BlockSpec((1, d, tn), lambda l, n: (l, 0, n)),
            pl.BlockSpec((1, 1, tn), lambda l, n: (l, 0, n)),
        ],
        out_specs=pl.BlockSpec((1, bsz, tn), lambda l, n: (l, 0, n)),
        out_shape=jax.ShapeDtypeStruct((depth, bsz, d3), F32),
        compiler_params=pltpu.CompilerParams(
            dimension_semantics=("arbitrary", "arbitrary"),
            vmem_limit_bytes=VMEM_LIMIT_BYTES),
        name="adaln_mod",
    )(c, ada_w, ada_b.reshape(depth, 1, d3))


def _inproj_kernel(x_ref, mod_ref, g_ref, w_ref, mu_ref, kvg_ref,
                   sh_ref, za_ref, q_ref, lat_ref, zb_ref, qi_ref, ki_ref, wi_ref, ga_ref, gb_ref,
                   carry_ref, *, d, n_shift, widths):
    j = pl.program_id(1)
    xb = x_ref[0]
    tm = xb.shape[0]
    shift = mod_ref[0, :, 0:d]
    scale = mod_ref[0, :, d:2 * d]
    ms = jnp.mean(xb * xb, axis=-1, keepdims=True)
    h = xb * lax.rsqrt(ms + NORM_EPS) * g_ref[...]
    h = h * (1.0 + scale) + shift
    hb = h.astype(BF16)

    @pl.when(j == 0)
    def _():
        carry_ref[...] = jnp.zeros_like(carry_ref)

    ps = _dot(hb, w_ref[:, 0:n_shift])
    prev = pltpu.roll(ps, 1, 0)
    row = lax.broadcasted_iota(I32, (tm, 1), 0)
    prev = jnp.where(row == 0, carry_ref[...], prev)
    carry_ref[...] = ps[tm - 1:tm, :]
    sh_ref[0] = ps + (prev - ps) * mu_ref[...]

    off = n_shift
    outs = (za_ref, q_ref, lat_ref, zb_ref, qi_ref, ki_ref, wi_ref, ga_ref, gb_ref)
    for o_ref, wd in zip(outs, widths):
        p = _dot(hb, w_ref[:, off:off + wd])
        if o_ref is lat_ref:
            p = p * lax.rsqrt(jnp.mean(p * p, axis=-1, keepdims=True) + NORM_EPS) * kvg_ref[...]
        o_ref[0] = p.astype(o_ref.dtype)
        off += wd


def _inproj_call(x, mod, norm_g, w_packed, mu, kvg, *, n_shift, widths, tm):
    bsz, s, d = x.shape
    npk = w_packed.shape[1]
    d3 = mod.shape[-1]
    row_spec = lambda w: pl.BlockSpec((1, tm, w), lambda b, j: (b, j, 0))
    const = lambda shape: pl.BlockSpec(shape, lambda b, j: tuple(0 for _ in shape))
    out_widths = (n_shift,) + tuple(widths)
    out_dtypes = (F32, BF16, BF16, BF16, BF16, BF16, BF16, F32, BF16, BF16)
    return pl.pallas_call(
        functools.partial(_inproj_kernel, d=d, n_shift=n_shift, widths=tuple(widths)),
        grid=(bsz, s // tm),
        in_specs=[
            row_spec(d),
            pl.BlockSpec((1, 1, d3), lambda b, j: (b, 0, 0)),
            const((1, d)),
            pl.BlockSpec((d, npk), lambda b, j: (0, 0), pipeline_mode=pl.Buffered(1)),
            const((1, n_shift)),
            const((1, widths[2])),
        ],
        out_specs=[row_spec(w) for w in out_widths],
        out_shape=[jax.ShapeDtypeStruct((bsz, s, w), dt) for w, dt in zip(out_widths, out_dtypes)],
        scratch_shapes=[pltpu.VMEM((1, n_shift), F32)],
        compiler_params=pltpu.CompilerParams(
            dimension_semantics=("arbitrary", "arbitrary"),
            vmem_limit_bytes=VMEM_LIMIT_BYTES),
        name="inproj",
    )(x, mod.reshape(bsz, 1, d3), norm_g.reshape(1, d), w_packed, mu.reshape(1, n_shift),
      kvg.reshape(1, -1))


def _rwkv_kernel(sh_ref, vec_ref, w2_ref, a2_ref, ones_ref, out_ref, st_ref,
                 *, width, p_chain, p_inv, p_state):
    j = pl.program_id(1)
    L = CHUNK
    n_pairs = width // LANES

    @pl.when(j == 0)
    def _():
        st_ref[...] = jnp.zeros_like(st_ref)

    blk = sh_ref[0]
    pr = blk[:, 0:width]
    pk = blk[:, width:2 * width]
    pv = blk[:, 2 * width:3 * width]
    lora = blk[:, 3 * width:3 * width + LANES]
    lane = lax.broadcasted_iota(I32, (1, LANES), 1)
    lo_half = lane < HEAD_DIM
    t_in = jnp.where(lo_half, jnp.tanh(lora), lora)
    dec_in = _mm(t_in, w2_ref[...], 3)
    a_in = _mm(t_in, a2_ref[...], 3)

    w0 = vec_ref[0:1, :]
    a0 = vec_ref[1:2, :]
    k_k = vec_ref[2:3, :]
    k_a = vec_ref[3:4, :]
    r_k = vec_ref[4:5, :]
    lnx_g = vec_ref[5:6, :]
    lnx_b = vec_ref[6:7, :]
    ones_bd = ones_ref[...]

    z = -(w0 + dec_in)
    sp = jnp.maximum(z, 0.0) + jnp.log(1.0 + jnp.exp(-jnp.abs(z)))
    logw = -jnp.exp(-sp - 0.5)
    a = _sigmoid(a0 + a_in)
    kk0 = pk * k_k
    ss = _mm_exact_rhs(kk0 * kk0, ones_bd, 2)
    kk = kk0 / jnp.maximum(jnp.sqrt(ss), 1e-12)
    k_eff = pk * (1.0 + (a - 1.0) * k_a)
    a_vec = -kk
    b_vec = kk * a

    tb = blk.shape[0]
    n_chunks = tb // L
    r_i = lax.broadcasted_iota(I32, (tb, tb), 0)
    c_i = lax.broadcasted_iota(I32, (tb, tb), 1)
    tril = jnp.where(((r_i // L) == (c_i // L)) & (c_i <= r_i), 1.0, 0.0).astype(BF16)
    cl = None
    for part in _split(logw, 3):
        t = _dot(tril, part)
        cl = t if cl is None else cl + t
    cl_last = jnp.concatenate(
        [jnp.broadcast_to(cl[c * L + L - 1:c * L + L, :], (L, width)) for c in range(n_chunks)], axis=0)
    rt = pr * jnp.exp(cl)
    at = a_vec * jnp.exp(cl - logw)
    e_inv = jnp.exp(-cl)
    bt = b_vec * e_inv
    kt = k_eff * e_inv
    e_l = jnp.exp(cl_last - cl)
    bl = b_vec * e_l
    kl = k_eff * e_l
    p_last = jnp.exp(cl_last)

    rr = lax.broadcasted_iota(I32, (LANES, LANES), 0)
    cc = lax.broadcasted_iota(I32, (LANES, LANES), 1)
    same = (rr // L) == (cc // L)
    strict = same & ((cc % L) < (rr % L))
    incl = same & ((cc % L) <= (rr % L))
    eye = rr == cc
    eye_f = jnp.where(eye, 1.0, 0.0)
    rr2 = lax.broadcasted_iota(I32, (LANES, 2 * LANES), 0)
    cc2 = lax.broadcasted_iota(I32, (LANES, 2 * LANES), 1)
    incl2 = ((rr2 // L) == ((cc2 % LANES) // L)) & ((cc2 % L) <= (rr2 % L))

    def stack(x):
        return jnp.concatenate([jnp.where(lo_half, x, 0.0), jnp.where(lo_half, 0.0, x)], axis=0)

    chains = [(c, p) for c in range(n_chunks) for p in range(n_pairs)]

    def piece(x, c, p):
        return x[c * L:(c + 1) * L, p * LANES:(p + 1) * LANES]

    atm, rtm, vm, blkl, a_ab, a_ak, a_rbk = {}, {}, {}, {}, {}, {}, {}
    for ch in chains:
        atm[ch], rtm[ch] = stack(piece(at, *ch)), stack(piece(rt, *ch))
        vm[ch] = stack(piece(pv, *ch)).astype(BF16)
        btm, ktm = stack(piece(bt, *ch)), stack(piece(kt, *ch))
        blkl[ch] = jnp.concatenate([stack(piece(bl, *ch)), stack(piece(kl, *ch))], axis=0).astype(BF16)
        g = _mm(jnp.concatenate([atm[ch], rtm[ch]], axis=0), jnp.concatenate([btm, ktm], axis=0),
                p_chain, _NT)
        a_ab[ch] = jnp.where(strict, g[0:LANES, 0:LANES], 0.0)
        a_ak[ch] = jnp.where(strict, g[0:LANES, LANES:], 0.0)
        a_rbk[ch] = jnp.where(incl2, g[LANES:, :], 0.0).astype(BF16)
    tinv = {ch: eye_f + a_ab[ch] for ch in chains}
    qn = {ch: _mm(a_ab[ch], a_ab[ch], p_inv) for ch in chains}
    sq = 2
    while 2 * sq < L:
        r = {ch: _mm(qn[ch], jnp.concatenate([qn[ch], tinv[ch]], axis=1), p_inv) for ch in chains}
        for ch in chains:
            qn[ch] = r[ch][:, 0:LANES]
            tinv[ch] = tinv[ch] + r[ch][:, LANES:]
        sq *= 2
    tinv = {ch: tinv[ch] + _mm(qn[ch], tinv[ch], p_inv) for ch in chains}
    akv = {ch: _mm(a_ak[ch], vm[ch], p_chain) for ch in chains}
    wz = {ch: _mm(tinv[ch], jnp.concatenate([atm[ch], akv[ch]], axis=1), p_inv) for ch in chains}
    wm = {ch: wz[ch][:, 0:LANES].astype(BF16) for ch in chains}
    zv = {ch: jnp.concatenate([wz[ch][:, LANES:].astype(BF16), vm[ch]], axis=0) for ch in chains}
    rqm = {ch: rtm[ch] + _mm(a_rbk[ch][:, 0:LANES], wm[ch], p_chain) for ch in chains}
    y0m = {ch: _mm(a_rbk[ch], zv[ch], p_chain) for ch in chains}
    mt = {ch: jnp.where(eye, piece(p_last, *ch)[0:1], 0.0)
          + _mm(blkl[ch][0:LANES], wm[ch], p_state, _TN) for ch in chains}
    ct = {ch: _mm(blkl[ch], zv[ch], p_state, _TN) for ch in chains}

    st = {p: st_ref[p] for p in range(n_pairs)}
    y_rows = []
    for c in range(n_chunks):
        ys = []
        for p in range(n_pairs):
            ym = _mm(rqm[(c, p)], st[p], p_state) + y0m[(c, p)]
            ys.append(ym[0:L] + ym[L:])
            st[p] = _mm(mt[(c, p)], st[p], p_state) + ct[(c, p)]
        y_rows.append(jnp.concatenate(ys, axis=1))
    for p in range(n_pairs):
        st_ref[p] = st[p]
    y = jnp.concatenate(y_rows, axis=0)

    inv_n = 1.0 / HEAD_DIM
    mu = _mm_exact_rhs(y, ones_bd, 2) * inv_n
    dlt = y - mu
    var = _mm_exact_rhs(dlt * dlt, ones_bd, 2) * inv_n
    yn = dlt * lax.rsqrt(var + GN_EPS) * lnx_g + lnx_b
    bonus = _mm_exact_rhs(pr * k_eff * r_k, ones_bd, 2) * pv
    out_ref[0] = yn + bonus


def _rwkv_call(sh, vecs, w2p, a2p, ones_bd, *, width, tb, p_chain=1, p_inv=1, p_state=1):
    bsz, s, n_shift = sh.shape
    n_pairs = width // LANES
    const = lambda shape: pl.BlockSpec(shape, lambda b, j: tuple(0 for _ in shape))
    return pl.pallas_call(
        functools.partial(_rwkv_kernel, width=width, p_chain=p_chain, p_inv=p_inv, p_state=p_state),
        grid=(bsz, s // tb),
        in_specs=[
            pl.BlockSpec((1, tb, n_shift), lambda b, j: (b, j, 0)),
            const(vecs.shape),
            const(w2p.shape),
            const(a2p.shape),
            const(ones_bd.shape),
        ],
        out_specs=pl.BlockSpec((1, tb, width), lambda b, j: (b, j, 0)),
        out_shape=jax.ShapeDtypeStruct((bsz, s, width), F32),
        scratch_shapes=[pltpu.VMEM((n_pairs, LANES, LANES), F32)],
        compiler_params=pltpu.CompilerParams(
            dimension_semantics=("arbitrary", "arbitrary"),
            vmem_limit_bytes=VMEM_LIMIT_BYTES),
        name="rwkv7_scan",
    )(sh, vecs, w2p, a2p, ones_bd)


def _bit_transpose32(a):
    a = list(a)
    m, j = 0x0000FFFF, 16
    while j:
        k = 0
        while k < 32:
            t = (a[k] ^ lax.shift_right_logical(a[k + j], jnp.int32(j))) & m
            a[k] = a[k] ^ t
            a[k + j] = a[k + j] ^ lax.shift_left(t, jnp.int32(j))
            k = (k + j + 1) & ~j
        j >>= 1
        m = (m ^ (m << j)) & 0xFFFFFFFF
    return a


def _dsa_kernel(q_ref, qi_ref, wi_ref, ki_ref, lat_ref, wukt_ref, wuv_ref, bias_ref, far_ref, out_ref,
                key_ref, acc_ref, lg_ref, p_ref, w_ref, wqk_ref, madd_ref, mt_ref, planes_ref, latt_ref,
                *, topk, kc, n_heads, seq, scale):
    qb = pl.program_id(1)
    QB = Q_BLOCK
    NEAR = 2 * QB
    HQ = n_heads * QB
    c_lat = lat_ref.shape[-1]
    q_end = (qb + 1) * QB
    n_idx = (jnp.maximum(q_end, NEAR) + kc - 1) // kc
    tpos = qb * QB + lax.broadcasted_iota(I32, (1, QB), 1)

    def head(x, h):
        return x[:, h * QB:(h + 1) * QB]

    qi_t = qi_ref[0].T
    qi_w = jnp.concatenate([qi_t[h * HEAD_DIM:(h + 1) * HEAD_DIM, :] for h in range(n_heads)], axis=1)
    w_ref[...] = jnp.concatenate([qi_w, jnp.zeros_like(qi_w)], axis=0).astype(BF16)
    wi_t = wi_ref[0].T

    def idx_body(c, carry):
        ks = pl.multiple_of(c * kc, kc)
        kib = ki_ref[0, pl.ds(ks, kc), :].astype(BF16)
        score = None
        for g in range(n_heads // 2):
            s_g = jnp.maximum(_dot(kib, w_ref[:, 2 * g * QB:(2 * g + 2) * QB]), 0.0)
            part = wi_t[2 * g:2 * g + 1, :] * s_g[:, 0:QB] + wi_t[2 * g + 1:2 * g + 2, :] * s_g[:, QB:]
            score = part if score is None else score + part
        score = score + 0.0
        bits = pltpu.bitcast(score, I32)
        key = bits ^ ((bits >> 31) & INT_MAX)
        spos = ks + lax.broadcasted_iota(I32, (kc, 1), 0)
        key = jnp.where(spos <= tpos, key, INT_MIN)
        key_ref[pl.ds(ks, kc), :] = key
        ukey = key ^ INT_MIN
        for g in range(kc // BIT_GROUP):
            regs = _bit_transpose32([ukey[g * BIT_GROUP + 8 * i:g * BIT_GROUP + 8 * i + 8, :]
                                     for i in range(32)])
            row0 = pl.multiple_of(c * (kc // 32) + 8 * g, 8)
            for b in range(32):
                planes_ref[b, pl.ds(row0, 8), :] = regs[31 - b]
        return carry

    @pl.when(qb == 0)
    def _():
        for c0 in range(0, seq, kc):
            latt_ref[0:c_lat, c0:c0 + kc] = lat_ref[0, c0:c0 + kc, :].T.astype(BF16)
        latt_ref[c_lat:, :] = jnp.ones((ONES_ROWS, seq), BF16)

    @pl.when((pl.program_id(0) == 0) & (qb == 0))
    def _():
        planes_ref[...] = jnp.zeros(planes_ref.shape, I32)

    lax.fori_loop(0, n_idx, idx_body, 0)

    q_t = q_ref[0].T.astype(BF16)
    for h in range(n_heads):
        qa = _dot(wukt_ref[h], q_t[h * HEAD_DIM:(h + 1) * HEAD_DIM, :])
        wqk_ref[:, h * QB:(h + 1) * QB] = (qa * (scale * LOG2E)).astype(BF16)
    far_end = jnp.maximum(qb - 1, 0) * QB
    n_far = (far_end + kc - 1) // kc
    near_start = pl.multiple_of(far_end, QB)
    near_ld = pl.multiple_of(jnp.minimum(near_start, seq - kc), QB)
    near_off = pl.multiple_of(near_start - near_ld, QB)
    ks0 = pl.multiple_of(jnp.where(n_far > 0, 0, near_ld), QB)
    lat0 = lat_ref[0, pl.ds(ks0, kc), :].astype(BF16)
    for g in range(n_heads // 2):
        cols = slice(2 * g * QB, (2 * g + 2) * QB)
        lg_ref[:, cols] = _dot(lat0, wqk_ref[:, cols])

    n_rows = seq // 32
    valid = lax.broadcasted_iota(I32, (n_rows, 1), 0) < n_idx * (kc // 32)
    kf = float(topk)

    def colsum(w):
        return jnp.sum(lax.population_count(w).astype(F32), axis=0, keepdims=True)

    def bit_body(i, carry):
        alive, c_gt, thr_u = carry
        b0 = 30 - 2 * i
        p1 = planes_ref[b0 + 1]
        p0 = planes_ref[b0]
        w1 = alive & p1
        w0 = alive ^ w1
        w11 = w1 & p0
        w10 = w1 ^ w11
        w01 = w0 & p0
        t11 = c_gt + colsum(w11)
        t10 = t11 + colsum(w10)
        t01 = t10 + colsum(w01)
        k11, k10, k01 = t11 >= kf, t10 >= kf, t01 >= kf
        alive = jnp.where(k11, w11, jnp.where(k10, w10, jnp.where(k01, w01, w0 ^ w01)))
        c_gt = jnp.where(k11, c_gt, jnp.where(k10, t11, jnp.where(k01, t10, t01)))
        digit = jnp.where(k11, 3, jnp.where(k10, 2, jnp.where(k01, 1, 0)))
        thr_u = thr_u | lax.shift_left(digit, b0)
        return alive, c_gt, thr_u

    alive, cnt_gt, thr_u = lax.fori_loop(
        0, 16, bit_body,
        (jnp.broadcast_to(jnp.where(valid, -1, 0), (n_rows, QB)).astype(I32),
         jnp.zeros((1, QB), F32), jnp.zeros((1, QB), I32)))
    thr = thr_u ^ INT_MIN
    cnt_ge = cnt_gt + jnp.sum(lax.population_count(alive).astype(F32), axis=0, keepdims=True)

    def count(pred):
        def body(c, acc):
            ks = pl.multiple_of(c * kc, kc)
            kch = key_ref[pl.ds(ks, kc), :]
            row = ks + lax.broadcasted_iota(I32, (kc, 1), 0)
            hit = jnp.where(pred(kch, row), 1.0, 0.0).reshape(kc // 8, 8, QB)
            parts = [hit[i] for i in range(kc // 8)]
            while len(parts) > 1:
                parts = [parts[i] + parts[i + 1] for i in range(0, len(parts), 2)]
            return acc + parts[0]
        acc = lax.fori_loop(0, n_idx, body, jnp.zeros((8, QB), F32))
        return jnp.sum(acc, axis=0, keepdims=True)

    need = kf - cnt_gt
    excess = jnp.where((cnt_ge > kf) & (thr > INT_MIN), 1.0, 0.0)
    mt_ref[...] = jnp.full((1, QB), INT_MAX, I32)

    @pl.when(jnp.max(excess) > 0.0)
    def _():
        nbits = max(1, (seq - 1).bit_length())

        def tie_body(i, mcur):
            cand = mcur | lax.shift_left(jnp.int32(1), nbits - 1 - i)
            cnt = count(lambda kch, row: (kch == thr) & (row < cand))
            return jnp.where(cnt < need, cand, mcur)

        mt_ref[...] = lax.fori_loop(0, nbits, tie_body, jnp.zeros((1, QB), I32))

    mtie = jnp.where(thr == INT_MIN, -1, mt_ref[...])

    def selected(ks, width, hi_limit):
        kch = key_ref[pl.ds(ks, width), :]
        row = ks + lax.broadcasted_iota(I32, (width, 1), 0)
        sel = (kch > thr) | ((kch == thr) & (row <= mtie))
        if hi_limit is not None:
            sel = sel & (row < hi_limit)
        return sel

    pair_cols = [slice(2 * g * QB, (2 * g + 2) * QB) for g in range(n_heads // 2)]

    def lat_chunk(ks, width):
        return lat_ref[0, pl.ds(ks, width), :].astype(BF16)

    acc_ref[...] = jnp.zeros(acc_ref.shape, F32)
    p_ref[...] = jnp.zeros(p_ref.shape, BF16)


    def lat_t_ones(ks, width):
        return latt_ref[:, pl.ds(ks, width)]

    def far_body(c, ms):
        ksp = pl.multiple_of(jnp.maximum(c - 1, 0) * kc, kc)
        latp_t = lat_t_ones(ksp, kc)
        ksn = pl.multiple_of(jnp.where(c + 1 < n_far, (c + 1) * kc, near_ld), QB)
        latn = lat_chunk(ksn, kc)
        ks = pl.multiple_of(c * kc, kc)
        madd_ref[...] = jnp.where(selected(ks, kc, far_end), 0.0, NEG_MASK)
        new_ms = list(ms)
        for g, cols in enumerate(pair_cols):
            pv_g = _dot(latp_t, p_ref[:, cols])
            lg_next = _dot(latn, wqk_ref[:, cols])
            for h in (2 * g, 2 * g + 1):
                hc = slice(h * QB, (h + 1) * QB)
                bfar = far_ref[h]
                lgm = lg_ref[:, hc] + madd_ref[...]
                m_new = jnp.maximum(ms[h], jnp.max(lgm, axis=0, keepdims=True) + bfar)
                p = jnp.exp2(lgm - (m_new - bfar))
                alpha = jnp.exp2(ms[h] - m_new)
                new_ms[h] = m_new
                p_ref[:, hc] = p.astype(BF16)
                acc_ref[:, hc] = (acc_ref[:, hc] + pv_g[:, (h - 2 * g) * QB:(h - 2 * g + 1) * QB]) * alpha
            lg_ref[:, cols] = lg_next
        return tuple(new_ms)

    ms = lax.fori_loop(0, n_far, far_body,
                       tuple(jnp.full((1, QB), NEG_INIT, F32) for _ in range(n_heads)))
    ks_last = pl.multiple_of(jnp.maximum(n_far - 1, 0) * kc, kc)
    pv = _dot(lat_t_ones(ks_last, kc), p_ref[...])

    tile_id = jnp.where(qb == 0, 1, 0)
    lgn = lg_ref[pl.ds(near_off, NEAR), :]
    sel = selected(near_start, NEAR, None)
    ps, alphas = [], []
    for h in range(n_heads):
        lgm = jnp.where(sel, head(lgn, h) + bias_ref[tile_id, h], NEG_MASK)
        m_new = jnp.maximum(ms[h], jnp.max(lgm, axis=0, keepdims=True))
        alphas.append(jnp.exp2(ms[h] - m_new))
        ps.append(jnp.exp2(lgm - m_new).astype(BF16))
    pvn = _dot(lat_t_ones(near_start, NEAR), jnp.concatenate(ps, axis=1))
    o_parts = []
    for h in range(n_heads):
        tot = (acc_ref[:, h * QB:(h + 1) * QB] + head(pv, h)) * alphas[h] + head(pvn, h)
        o_h = tot[0:c_lat] / tot[c_lat:c_lat + 1]
        o_parts.append(o_h.astype(BF16))
    out_ref[0] = _dot(jnp.concatenate(o_parts, axis=0), wuv_ref[...], _TN)


def _dsa_call(q, qi, wi, ki2, lat, wuk_t, wuv_bd, bias_near, bias_far, *, topk, kc):
    bsz, s, aw = q.shape
    n_heads = aw // HEAD_DIM
    c_lat = lat.shape[-1]
    nb = s // Q_BLOCK
    hq = n_heads * Q_BLOCK
    const = lambda shape: pl.BlockSpec(shape, lambda b, j: tuple(0 for _ in shape))
    blk = lambda w: pl.BlockSpec((1, Q_BLOCK, w), lambda b, j: (b, j, 0))
    full = lambda w: pl.BlockSpec((1, s, w), lambda b, j: (b, 0, 0))
    return pl.pallas_call(
        functools.partial(_dsa_kernel, topk=topk, kc=kc, n_heads=n_heads, seq=s,
                          scale=float(HEAD_DIM) ** -0.5),
        grid=(bsz, nb),
        in_specs=[
            blk(aw), blk(qi.shape[-1]), blk(LANES), full(LANES), full(c_lat),
            const(wuk_t.shape), const(wuv_bd.shape), const(bias_near.shape), const(bias_far.shape),
        ],
        out_specs=blk(aw),
        out_shape=jax.ShapeDtypeStruct((bsz, s, aw), F32),
        scratch_shapes=[
            pltpu.VMEM((s, Q_BLOCK), I32),
            pltpu.VMEM((c_lat + ONES_ROWS, hq), F32),
            pltpu.VMEM((kc, hq), F32),
            pltpu.VMEM((kc, hq), BF16),
            pltpu.VMEM((c_lat, hq), BF16),
            pltpu.VMEM((c_lat, hq), BF16),
            pltpu.VMEM((kc, Q_BLOCK), F32),
            pltpu.VMEM((1, Q_BLOCK), I32),
            pltpu.VMEM((32, s // 32, Q_BLOCK), I32),
            pltpu.VMEM((c_lat + ONES_ROWS, s), BF16),
        ],
        compiler_params=pltpu.CompilerParams(
            dimension_semantics=("arbitrary", "arbitrary"),
            vmem_limit_bytes=VMEM_LIMIT_BYTES),
        name="dsa_attention",
    )(q, qi, wi, ki2, lat, wuk_t, wuv_bd, bias_near, bias_far)


def _out_kernel(ya_ref, za_ref, yb_ref, zb_ref, ga_ref, gb_ref, x_ref, mod_ref,
                wpa_ref, wpb_ref, wo_ref, fg_ref, o_ref, *, d, final):
    za = za_ref[0].astype(F32)
    zb = zb_ref[0].astype(F32)
    ua = (ya_ref[0] * (za * _sigmoid(za))).astype(BF16)
    ub = (yb_ref[0] * (zb * _sigmoid(zb))).astype(BF16)
    br_a = _dot(ua, wpa_ref[...])
    br_b = _dot(ub, wpb_ref[...])
    merged = _sigmoid(ga_ref[0].astype(F32)) * br_a + _sigmoid(gb_ref[0].astype(F32)) * br_b
    gate = mod_ref[0, :, 2 * d:3 * d]
    xo = x_ref[0] + gate * _dot(merged.astype(BF16), wo_ref[...])
    if final:
        xo = xo * lax.rsqrt(jnp.mean(xo * xo, axis=-1, keepdims=True) + NORM_EPS) * fg_ref[...]
    o_ref[0] = xo


def _out_call(ya, za, yb, zb, ga, gb, x, mod, wpa, wpb, wo, final_g, *, final, tm):
    bsz, s, d = x.shape
    d3 = mod.shape[-1]
    row = lambda w: pl.BlockSpec((1, tm, w), lambda b, j: (b, j, 0))
    const = lambda shape: pl.BlockSpec(shape, lambda b, j: tuple(0 for _ in shape))
    return pl.pallas_call(
        functools.partial(_out_kernel, d=d, final=final),
        grid=(bsz, s // tm),
        in_specs=[
            row(ya.shape[-1]), row(za.shape[-1]), row(yb.shape[-1]), row(zb.shape[-1]),
            row(d), row(d), row(d),
            pl.BlockSpec((1, 1, d3), lambda b, j: (b, 0, 0)),
            const(wpa.shape), const(wpb.shape), const(wo.shape), const((1, d)),
        ],
        out_specs=row(d),
        out_shape=jax.ShapeDtypeStruct((bsz, s, d), F32),
        compiler_params=pltpu.CompilerParams(
            dimension_semantics=("arbitrary", "arbitrary"),
            vmem_limit_bytes=VMEM_LIMIT_BYTES),
        name="gated_out",
    )(ya, za, yb, zb, ga, gb, x, mod.reshape(bsz, 1, d3), wpa, wpb, wo, final_g.reshape(1, d))


def _t5_bucket(dist, num_buckets):
    max_exact = num_buckets // 2
    is_small = dist < max_exact
    dd = jnp.maximum(dist, 1).astype(F32)
    large = max_exact + (jnp.log(dd / max_exact) / math.log(MAX_DISTANCE / max_exact)
                         * (num_buckets - max_exact)).astype(I32)
    large = jnp.minimum(large, num_buckets - 1)
    return jnp.where(is_small, dist, large)


def _bias_tiles(rel_bias):
    nbk = rel_bias.shape[0]
    i = jnp.arange(Q_BLOCK, dtype=I32)[:, None]
    jj = jnp.arange(2 * Q_BLOCK, dtype=I32)[None, :]
    tiles = []
    for base in (Q_BLOCK, 0):
        bucket = _t5_bucket(jnp.maximum(base + i - jj, 0), nbk)
        onehot = (bucket[None, :, :] == jnp.arange(nbk, dtype=I32)[:, None, None]).astype(F32)
        tiles.append(jnp.einsum('nh,nij->hji', rel_bias, onehot,
                                precision=lax.Precision.HIGHEST))
    far = rel_bias[_t5_bucket(jnp.full((1, 1), MAX_DISTANCE, I32), nbk)]
    return jnp.stack(tiles, axis=0) * LOG2E, jnp.moveaxis(far, -1, 0) * LOG2E


def _block_diag_heads(w, rows_per_head, cols_per_head):
    n = w.shape[0]
    eye = jnp.eye(n, dtype=w.dtype)
    return jnp.einsum('hrc,hg->hrgc', w, eye).reshape(n * rows_per_head, n * cols_per_head)


def kernel(x, c, ada_w, ada_b, norm_g, w_in, shift_mu, w0, w2, a0, a2, k_k, k_a, r_k, lnx_g, lnx_b,
           kv_norm_g, w_uk, w_uv, w_pa, w_pb, w_o, rel_bias, final_g):
    bsz, s, d = x.shape
    depth = w_in.shape[0]
    rw = w0.shape[-1]
    c_lat, n_heads, dh = w_uk.shape[1:]
    aw = n_heads * dh
    iw = IDX_HEADS * IDX_HEAD_DIM
    n_shift = 3 * rw + DECAY_LORA + ICLR_LORA
    topk = min(TOPK_MAX, s // 4)
    assert DECAY_LORA + ICLR_LORA == LANES and dh == HEAD_DIM and IDX_HEAD_DIM == HEAD_DIM
    assert s % (2 * Q_BLOCK) == 0 and rw % LANES == 0

    sizes = (rw, rw, rw, DECAY_LORA, ICLR_LORA, rw, aw, c_lat, aw, iw, IDX_HEAD_DIM, IDX_HEADS, d, d)
    offs = [0]
    for sz in sizes:
        offs.append(offs[-1] + sz)
    col = lambda i: slice(offs[i], offs[i + 1])
    widths = (rw, aw, c_lat, aw, iw, LANES, LANES, d, d)

    mod = _mod_call(c, ada_w, ada_b)
    bias_near, bias_far = _bias_tiles(rel_bias)
    ones_bd = _block_diag_heads(jnp.ones((rw // HEAD_DIM, HEAD_DIM, HEAD_DIM), BF16), HEAD_DIM, HEAD_DIM)
    kc = 512 if s % 512 == 0 else 2 * Q_BLOCK
    tm_in = 512 if s % 512 == 0 else 256
    tm_out = 512 if s % 512 == 0 else 256

    for l in range(depth):
        w = w_in[l]
        wi_pad = jnp.zeros((d, LANES - IDX_HEADS), w.dtype)
        w_packed = jnp.concatenate(
            [w[:, 0:n_shift], w[:, col(5)], w[:, col(6)], w[:, col(7)], w[:, col(8)], w[:, col(9)],
             w[:, col(10)], w[:, col(10)], w[:, col(11)], wi_pad, w[:, col(12)], w[:, col(13)]],
            axis=1).astype(BF16)
        sh, za, q, lat, zb, qi, ki2, wi, ga, gb = _inproj_call(
            x, mod[l], norm_g[l], w_packed, shift_mu[l], kv_norm_g[l],
            n_shift=n_shift, widths=widths, tm=tm_in)

        vecs = jnp.stack([w0[l], a0[l], k_k[l], k_a[l], r_k[l].reshape(-1), lnx_g[l], lnx_b[l],
                          jnp.zeros((rw,), F32)], axis=0)
        w2p = jnp.concatenate([w2[l], jnp.zeros((ICLR_LORA, rw), F32)], axis=0)
        a2p = jnp.concatenate([jnp.zeros((DECAY_LORA, rw), F32), a2[l]], axis=0)
        ya = _rwkv_call(sh, vecs, w2p, a2p, ones_bd, width=rw, tb=4 * CHUNK)

        wuk_t = jnp.transpose(w_uk[l], (1, 0, 2)).astype(BF16)
        wuv_bd = _block_diag_heads(jnp.transpose(w_uv[l], (1, 0, 2)), c_lat, dh).astype(BF16)
        yb = _dsa_call(q, qi, wi, ki2, lat, wuk_t, wuv_bd, bias_near, bias_far, topk=topk, kc=kc)

        x = _out_call(ya, za, yb, zb, ga, gb, x, mod[l], w_pa[l].astype(BF16), w_pb[l].astype(BF16),
                      w_o[l].astype(BF16), final_g, final=(l == depth - 1), tm=tm_out)
    return x
```

```python
import functools
import math

import jax
import jax.numpy as jnp
from jax import lax
from jax.experimental import pallas as pl
from jax.experimental.pallas import tpu as pltpu

F32 = jnp.float32
BF16 = jnp.bfloat16
I32 = jnp.int32

DECAY_LORA = 64
ICLR_LORA = 64
GN_EPS = 64e-5
IDX_HEADS = 8
IDX_HEAD_DIM = 64
TOPK_MAX = 256
Q_BLOCK = 128
MAX_DISTANCE = 128
NORM_EPS = 1e-6

LANES = 128
VMEM_LIMIT_BYTES = 56 * 1024 * 1024

HEAD_DIM = 64
CHUNK = 64
INT_MIN = -(2 ** 31)
INT_MAX = 2 ** 31 - 1
NEG_MASK = -2e30
NEG_INIT = -1e30
LOG2E = 1.4426950408889634
BIT_GROUP = 256
ONES_ROWS = 16


def _sigmoid(x):
    return 1.0 / (1.0 + jnp.exp(-x))


def _dot(a, b, dims=(((1,), (0,)), ((), ()))):
    return lax.dot_general(a, b, dims, preferred_element_type=F32)


_NN = (((1,), (0,)), ((), ()))
_NT = (((1,), (1,)), ((), ()))
_TN = (((0,), (0,)), ((), ()))


def _split(x, n):
    parts = []
    rem = x
    for i in range(n):
        p = rem.astype(BF16)
        parts.append(p)
        if i + 1 < n:
            rem = rem - p.astype(F32)
    return parts


def _mm(a, b, passes, dims=_NN):
    if passes == 1:
        return _dot(a.astype(BF16), b.astype(BF16), dims)
    if passes == 3:
        a1, a2 = _split(a, 2)
        b1, b2 = _split(b, 2)
        return _dot(a1, b1, dims) + (_dot(a1, b2, dims) + _dot(a2, b1, dims))
    a1, a2, a3 = _split(a, 3)
    b1, b2, b3 = _split(b, 3)
    hi = _dot(a1, b1, dims)
    mid = _dot(a1, b2, dims) + _dot(a2, b1, dims)
    lo = _dot(a2, b2, dims) + (_dot(a1, b3, dims) + _dot(a3, b1, dims))
    return hi + (mid + lo)


def _mm_exact_rhs(a, b_bf16, n):
    out = None
    for p in _split(a, n):
        t = _dot(p, b_bf16)
        out = t if out is None else out + t
    return out


def _mod_kernel(c_ref, w_ref, b_ref, o_ref):
    c = c_ref[...]
    ca = c * _sigmoid(c)
    o_ref[0] = _mm(ca, w_ref[0], 6) + b_ref[0]


def _mod_call(c, ada_w, ada_b):
    depth, d, d3 = ada_w.shape
    bsz = c.shape[0]
    tn = d
    return pl.pallas_call(
        _mod_kernel,
        grid=(depth, d3 // tn),
        in_specs=[
            pl.BlockSpec((bsz, d), lambda l, n: (0, 0)),
            pl.BlockSpec((1, d, tn), lambda l, n: (l, 0, n)),
            pl.BlockSpec((1, 1, tn), lambda l, n: (l, 0, n)),
        ],
        out_specs=pl.BlockSpec((1, bsz, tn), lambda l, n: (l, 0, n)),
        out_shape=jax.ShapeDtypeStruct((depth, bsz, d3), F32),
        compiler_params=pltpu.CompilerParams(
            dimension_semantics=("arbitrary", "arbitrary"),
            vmem_limit_bytes=VMEM_LIMIT_BYTES),
        name="adaln_mod",
    )(c, ada_w, ada_b.reshape(depth, 1, d3))


def _inproj_kernel(x_ref, mod_ref, g_ref, w_ref, mu_ref, kvg_ref,
                   sh_ref, za_ref, q_ref, lat_ref, zb_ref, qi_ref, ki_ref, wi_ref, ga_ref, gb_ref,
                   carry_ref, *, d, n_shift, widths):
    j = pl.program_id(1)
    xb = x_ref[0]
    tm = xb.shape[0]
    shift = mod_ref[0, :, 0:d]
    scale = mod_ref[0, :, d:2 * d]
    ms = jnp.mean(xb * xb, axis=-1, keepdims=True)
    h = xb * lax.rsqrt(ms + NORM_EPS) * g_ref[...]
    h = h * (1.0 + scale) + shift
    hb = h.astype(BF16)

    @pl.when(j == 0)
    def _():
        carry_ref[...] = jnp.zeros_like(carry_ref)

    ps = _dot(hb, w_ref[:, 0:n_shift])
    prev = pltpu.roll(ps, 1, 0)
    row = lax.broadcasted_iota(I32, (tm, 1), 0)
    prev = jnp.where(row == 0, carry_ref[...], prev)
    carry_ref[...] = ps[tm - 1:tm, :]
    sh_ref[0] = ps + (prev - ps) * mu_ref[...]

    off = n_shift
    outs = (za_ref, q_ref, lat_ref, zb_ref, qi_ref, ki_ref, wi_ref, ga_ref, gb_ref)
    for o_ref, wd in zip(outs, widths):
        p = _dot(hb, w_ref[:, off:off + wd])
        if o_ref is lat_ref:
            p = p * lax.rsqrt(jnp.mean(p * p, axis=-1, keepdims=True) + NORM_EPS) * kvg_ref[...]
        o_ref[0] = p.astype(o_ref.dtype)
        off += wd


def _inproj_call(x, mod, norm_g, w_packed, mu, kvg, *, n_shift, widths, tm):
    bsz, s, d = x.shape
    npk = w_packed.shape[1]
    d3 = mod.shape[-1]
    row_spec = lambda w: pl.BlockSpec((1, tm, w), lambda b, j: (b, j, 0))
    const = lambda shape: pl.BlockSpec(shape, lambda b, j: tuple(0 for _ in shape))
    out_widths = (n_shift,) + tuple(widths)
    out_dtypes = (F32, BF16, BF16, BF16, BF16, BF16, BF16, F32, BF16, BF16)
    return pl.pallas_call(
        functools.partial(_inproj_kernel, d=d, n_shift=n_shift, widths=tuple(widths)),
        grid=(bsz, s // tm),
        in_specs=[
            row_spec(d),
            pl.BlockSpec((1, 1, d3), lambda b, j: (b, 0, 0)),
            const((1, d)),
            pl.BlockSpec((d, npk), lambda b, j: (0, 0), pipeline_mode=pl.Buffered(1)),
            const((1, n_shift)),
            const((1, widths[2])),
        ],
        out_specs=[row_spec(w) for w in out_widths],
        out_shape=[jax.ShapeDtypeStruct((bsz, s, w), dt) for w, dt in zip(out_widths, out_dtypes)],
        scratch_shapes=[pltpu.VMEM((1, n_shift), F32)],
        compiler_params=pltpu.CompilerParams(
            dimension_semantics=("arbitrary", "arbitrary"),
            vmem_limit_bytes=VMEM_LIMIT_BYTES),
        name="inproj",
    )(x, mod.reshape(bsz, 1, d3), norm_g.reshape(1, d), w_packed, mu.reshape(1, n_shift),
      kvg.reshape(1, -1))


def _rwkv_kernel(sh_ref, vec_ref, w2_ref, a2_ref, ones_ref, out_ref, st_ref,
                 *, width, p_chain, p_inv, p_state):
    j = pl.program_id(1)
    L = CHUNK
    n_pairs = width // LANES

    @pl.when(j == 0)
    def _():
        st_ref[...] = jnp.zeros_like(st_ref)

    blk = sh_ref[0]
    pr = blk[:, 0:width]
    pk = blk[:, width:2 * width]
    pv = blk[:, 2 * width:3 * width]
    lora = blk[:, 3 * width:3 * width + LANES]
    lane = lax.broadcasted_iota(I32, (1, LANES), 1)
    lo_half = lane < HEAD_DIM
    t_in = jnp.where(lo_half, jnp.tanh(lora), lora)
    dec_in = _mm(t_in, w2_ref[...], 3)
    a_in = _mm(t_in, a2_ref[...], 3)

    w0 = vec_ref[0:1, :]
    a0 = vec_ref[1:2, :]
    k_k = vec_ref[2:3, :]
    k_a = vec_ref[3:4, :]
    r_k = vec_ref[4:5, :]
    lnx_g = vec_ref[5:6, :]
    lnx_b = vec_ref[6:7, :]
    ones_bd = ones_ref[...]

    z = -(w0 + dec_in)
    sp = jnp.maximum(z, 0.0) + jnp.log(1.0 + jnp.exp(-jnp.abs(z)))
    logw = -jnp.exp(-sp - 0.5)
    a = _sigmoid(a0 + a_in)
    kk0 = pk * k_k
    ss = _mm_exact_rhs(kk0 * kk0, ones_bd, 2)
    kk = kk0 / jnp.maximum(jnp.sqrt(ss), 1e-12)
    k_eff = pk * (1.0 + (a - 1.0) * k_a)
    a_vec = -kk
    b_vec = kk * a

    tb = blk.shape[0]
    n_chunks = tb // L
    r_i = lax.broadcasted_iota(I32, (tb, tb), 0)
    c_i = lax.broadcasted_iota(I32, (tb, tb), 1)
    tril = jnp.where(((r_i // L) == (c_i // L)) & (c_i <= r_i), 1.0, 0.0).astype(BF16)
    cl = None
    for part in _split(logw, 3):
        t = _dot(tril, part)
        cl = t if cl is None else cl + t
    cl_last = jnp.concatenate(
        [jnp.broadcast_to(cl[c * L + L - 1:c * L + L, :], (L, width)) for c in range(n_chunks)], axis=0)
    rt = pr * jnp.exp(cl)
    at = a_vec * jnp.exp(cl - logw)
    e_inv = jnp.exp(-cl)
    bt = b_vec * e_inv
    kt = k_eff * e_inv
    e_l = jnp.exp(cl_last - cl)
    bl = b_vec * e_l
    kl = k_eff * e_l
    p_last = jnp.exp(cl_last)

    rr = lax.broadcasted_iota(I32, (LANES, LANES), 0)
    cc = lax.broadcasted_iota(I32, (LANES, LANES), 1)
    same = (rr // L) == (cc // L)
    strict = same & ((cc % L) < (rr % L))
    incl = same & ((cc % L) <= (rr % L))
    eye = rr == cc
    eye_f = jnp.where(eye, 1.0, 0.0)
    rr2 = lax.broadcasted_iota(I32, (LANES, 2 * LANES), 0)
    cc2 = lax.broadcasted_iota(I32, (LANES, 2 * LANES), 1)
    incl2 = ((rr2 // L) == ((cc2 % LANES) // L)) & ((cc2 % L) <= (rr2 % L))

    def stack(x):
        return jnp.concatenate([jnp.where(lo_half, x, 0.0), jnp.where(lo_half, 0.0, x)], axis=0)

    chains = [(c, p) for c in range(n_chunks) for p in range(n_pairs)]

    def piece(x, c, p):
        return x[c * L:(c + 1) * L, p * LANES:(p + 1) * LANES]

    atm, rtm, vm, blkl, a_ab, a_ak, a_rbk = {}, {}, {}, {}, {}, {}, {}
    for ch in chains:
        atm[ch], rtm[ch] = stack(piece(at, *ch)), stack(piece(rt, *ch))
        vm[ch] = stack(piece(pv, *ch)).astype(BF16)
        btm, ktm = stack(piece(bt, *ch)), stack(piece(kt, *ch))
        blkl[ch] = jnp.concatenate([stack(piece(bl, *ch)), stack(piece(kl, *ch))], axis=0).astype(BF16)
        g = _mm(jnp.concatenate([atm[ch], rtm[ch]], axis=0), jnp.concatenate([btm, ktm], axis=0),
                p_chain, _NT)
        a_ab[ch] = jnp.where(strict, g[0:LANES, 0:LANES], 0.0)
        a_ak[ch] = jnp.where(strict, g[0:LANES, LANES:], 0.0)
        a_rbk[ch] = jnp.where(incl2, g[LANES:, :], 0.0).astype(BF16)
    tinv = {ch: eye_f + a_ab[ch] for ch in chains}
    qn = {ch: _mm(a_ab[ch], a_ab[ch], p_inv) for ch in chains}
    sq = 2
    while 2 * sq < L:
        r = {ch: _mm(qn[ch], jnp.concatenate([qn[ch], tinv[ch]], axis=1), p_inv) for ch in chains}
        for ch in chains:
            qn[ch] = r[ch][:, 0:LANES]
            tinv[ch] = tinv[ch] + r[ch][:, LANES:]
        sq *= 2
    tinv = {ch: tinv[ch] + _mm(qn[ch], tinv[ch], p_inv) for ch in chains}
    akv = {ch: _mm(a_ak[ch], vm[ch], p_chain) for ch in chains}
    wz = {ch: _mm(tinv[ch], jnp.concatenate([atm[ch], akv[ch]], axis=1), p_inv) for ch in chains}
    wm = {ch: wz[ch][:, 0:LANES].astype(BF16) for ch in chains}
    zv = {ch: jnp.concatenate([wz[ch][:, LANES:].astype(BF16), vm[ch]], axis=0) for ch in chains}
    rqm = {ch: rtm[ch] + _mm(a_rbk[ch][:, 0:LANES], wm[ch], p_chain) for ch in chains}
    y0m = {ch: _mm(a_rbk[ch], zv[ch], p_chain) for ch in chains}
    mt = {ch: jnp.where(eye, piece(p_last, *ch)[0:1], 0.0)
          + _mm(blkl[ch][0:LANES], wm[ch], p_state, _TN) for ch in chains}
    ct = {ch: _mm(blkl[ch], zv[ch], p_state, _TN) for ch in chains}

    st = {p: st_ref[p] for p in range(n_pairs)}
    y_rows = []
    for c in range(n_chunks):
        ys = []
        for p in range(n_pairs):
            ym = _mm(rqm[(c, p)], st[p], p_state) + y0m[(c, p)]
            ys.append(ym[0:L] + ym[L:])
            st[p] = _mm(mt[(c, p)], st[p], p_state) + ct[(c, p)]
        y_rows.append(jnp.concatenate(ys, axis=1))
    for p in range(n_pairs):
        st_ref[p] = st[p]
    y = jnp.concatenate(y_rows, axis=0)

    inv_n = 1.0 / HEAD_DIM
    mu = _mm_exact_rhs(y, ones_bd, 2) * inv_n
    dlt = y - mu
    var = _mm_exact_rhs(dlt * dlt, ones_bd, 2) * inv_n
    yn = dlt * lax.rsqrt(var + GN_EPS) * lnx_g + lnx_b
    bonus = _mm_exact_rhs(pr * k_eff * r_k, ones_bd, 2) * pv
    out_ref[0] = (yn + bonus).astype(out_ref.dtype)


def _rwkv_call(sh, vecs, w2p, a2p, ones_bd, *, width, tb, p_chain=1, p_inv=1, p_state=1):
    bsz, s, n_shift = sh.shape
    n_pairs = width // LANES
    const = lambda shape: pl.BlockSpec(shape, lambda b, j: tuple(0 for _ in shape))
    return pl.pallas_call(
        functools.partial(_rwkv_kernel, width=width, p_chain=p_chain, p_inv=p_inv, p_state=p_state),
        grid=(bsz, s // tb),
        in_specs=[
            pl.BlockSpec((1, tb, n_shift), lambda b, j: (b, j, 0)),
            const(vecs.shape),
            const(w2p.shape),
            const(a2p.shape),
            const(ones_bd.shape),
        ],
        out_specs=pl.BlockSpec((1, tb, width), lambda b, j: (b, j, 0)),
        out_shape=jax.ShapeDtypeStruct((bsz, s, width), BF16),
        scratch_shapes=[pltpu.VMEM((n_pairs, LANES, LANES), F32)],
        compiler_params=pltpu.CompilerParams(
            dimension_semantics=("arbitrary", "arbitrary"),
            vmem_limit_bytes=VMEM_LIMIT_BYTES),
        name="rwkv7_scan",
    )(sh, vecs, w2p, a2p, ones_bd)


def _bit_transpose32(a):
    a = list(a)
    m, j = 0x0000FFFF, 16
    while j:
        k = 0
        while k < 32:
            t = (a[k] ^ lax.shift_right_logical(a[k + j], jnp.int32(j))) & m
            a[k] = a[k] ^ t
            a[k + j] = a[k + j] ^ lax.shift_left(t, jnp.int32(j))
            k = (k + j + 1) & ~j
        j >>= 1
        m = (m ^ (m << j)) & 0xFFFFFFFF
    return a


def _dsa_kernel(q_ref, qi_ref, wi_ref, ki_ref, lat_ref, wukt_ref, wuv_ref, bias_ref, far_ref, out_ref,
                key_ref, acc_ref, lg_ref, p_ref, w_ref, wqk_ref, madd_ref, mt_ref, planes_ref, latt_ref,
                *, topk, kc, n_heads, seq, scale):
    qb = pl.program_id(1)
    QB = Q_BLOCK
    NEAR = 2 * QB
    HQ = n_heads * QB
    c_lat = lat_ref.shape[-1]
    q_end = (qb + 1) * QB
    n_idx = (jnp.maximum(q_end, NEAR) + kc - 1) // kc
    tpos = qb * QB + lax.broadcasted_iota(I32, (1, QB), 1)

    def head(x, h):
        return x[:, h * QB:(h + 1) * QB]

    qi_t = qi_ref[0].T
    qi_w = jnp.concatenate([qi_t[h * HEAD_DIM:(h + 1) * HEAD_DIM, :] for h in range(n_heads)], axis=1)
    w_ref[...] = jnp.concatenate([qi_w, jnp.zeros_like(qi_w)], axis=0).astype(BF16)
    wi_t = wi_ref[0].T

    def idx_body(c, carry):
        ks = pl.multiple_of(c * kc, kc)
        kib = ki_ref[0, pl.ds(ks, kc), :].astype(BF16)
        score = None
        for g in range(n_heads // 2):
            s_g = jnp.maximum(_dot(kib, w_ref[:, 2 * g * QB:(2 * g + 2) * QB]), 0.0)
            part = wi_t[2 * g:2 * g + 1, :] * s_g[:, 0:QB] + wi_t[2 * g + 1:2 * g + 2, :] * s_g[:, QB:]
            score = part if score is None else score + part
        score = score + 0.0
        bits = pltpu.bitcast(score, I32)
        key = bits ^ ((bits >> 31) & INT_MAX)
        spos = ks + lax.broadcasted_iota(I32, (kc, 1), 0)
        key = jnp.where(spos <= tpos, key, INT_MIN)
        key_ref[pl.ds(ks, kc), :] = key
        ukey = key ^ INT_MIN
        for g in range(kc // BIT_GROUP):
            regs = _bit_transpose32([ukey[g * BIT_GROUP + 8 * i:g * BIT_GROUP + 8 * i + 8, :]
                                     for i in range(32)])
            row0 = pl.multiple_of(c * (kc // 32) + 8 * g, 8)
            for b in range(32):
                planes_ref[b, pl.ds(row0, 8), :] = regs[31 - b]
        return carry

    @pl.when(qb == 0)
    def _():
        for c0 in range(0, seq, kc):
            latt_ref[0:c_lat, c0:c0 + kc] = lat_ref[0, c0:c0 + kc, :].T.astype(BF16)
        latt_ref[c_lat:, :] = jnp.ones((ONES_ROWS, seq), BF16)

    @pl.when((pl.program_id(0) == 0) & (qb == 0))
    def _():
        planes_ref[...] = jnp.zeros(planes_ref.shape, I32)

    lax.fori_loop(0, n_idx, idx_body, 0)

    q_t = q_ref[0].T.astype(BF16)
    for h in range(n_heads):
        qa = _dot(wukt_ref[h], q_t[h * HEAD_DIM:(h + 1) * HEAD_DIM, :])
        wqk_ref[:, h * QB:(h + 1) * QB] = (qa * (scale * LOG2E)).astype(BF16)
    far_end = jnp.maximum(qb - 1, 0) * QB
    n_far = (far_end + kc - 1) // kc
    near_start = pl.multiple_of(far_end, QB)
    near_ld = pl.multiple_of(jnp.minimum(near_start, seq - kc), QB)
    near_off = pl.multiple_of(near_start - near_ld, QB)
    ks0 = pl.multiple_of(jnp.where(n_far > 0, 0, near_ld), QB)
    lat0 = lat_ref[0, pl.ds(ks0, kc), :].astype(BF16)
    for g in range(n_heads // 2):
        cols = slice(2 * g * QB, (2 * g + 2) * QB)
        lg0 = _dot(lat0, wqk_ref[:, cols])
        lg_ref[2 * g] = lg0[:, 0:QB]
        lg_ref[2 * g + 1] = lg0[:, QB:]

    n_rows = seq // 32
    valid = lax.broadcasted_iota(I32, (n_rows, 1), 0) < n_idx * (kc // 32)
    kf = float(topk)

    def colsum(w):
        return jnp.sum(lax.population_count(w).astype(F32), axis=0, keepdims=True)

    def bit_body(i, carry):
        alive, c_gt, thr_u = carry
        b0 = 30 - 2 * i
        p1 = planes_ref[b0 + 1]
        p0 = planes_ref[b0]
        w1 = alive & p1
        w0 = alive ^ w1
        w11 = w1 & p0
        w10 = w1 ^ w11
        w01 = w0 & p0
        t11 = c_gt + colsum(w11)
        t10 = t11 + colsum(w10)
        t01 = t10 + colsum(w01)
        k11, k10, k01 = t11 >= kf, t10 >= kf, t01 >= kf
        alive = jnp.where(k11, w11, jnp.where(k10, w10, jnp.where(k01, w01, w0 ^ w01)))
        c_gt = jnp.where(k11, c_gt, jnp.where(k10, t11, jnp.where(k01, t10, t01)))
        digit = jnp.where(k11, 3, jnp.where(k10, 2, jnp.where(k01, 1, 0)))
        thr_u = thr_u | lax.shift_left(digit, b0)
        return alive, c_gt, thr_u

    alive, cnt_gt, thr_u = lax.fori_loop(
        0, 16, bit_body,
        (jnp.broadcast_to(jnp.where(valid, -1, 0), (n_rows, QB)).astype(I32),
         jnp.zeros((1, QB), F32), jnp.zeros((1, QB), I32)))
    thr = thr_u ^ INT_MIN
    cnt_ge = cnt_gt + jnp.sum(lax.population_count(alive).astype(F32), axis=0, keepdims=True)

    def count(pred):
        def body(c, acc):
            ks = pl.multiple_of(c * kc, kc)
            kch = key_ref[pl.ds(ks, kc), :]
            row = ks + lax.broadcasted_iota(I32, (kc, 1), 0)
            hit = jnp.where(pred(kch, row), 1.0, 0.0).reshape(kc // 8, 8, QB)
            parts = [hit[i] for i in range(kc // 8)]
            while len(parts) > 1:
                parts = [parts[i] + parts[i + 1] for i in range(0, len(parts), 2)]
            return acc + parts[0]
        acc = lax.fori_loop(0, n_idx, body, jnp.zeros((8, QB), F32))
        return jnp.sum(acc, axis=0, keepdims=True)

    need = kf - cnt_gt
    excess = jnp.where((cnt_ge > kf) & (thr > INT_MIN), 1.0, 0.0)
    mt_ref[...] = jnp.full((1, QB), INT_MAX, I32)

    @pl.when(jnp.max(excess) > 0.0)
    def _():
        nbits = max(1, (seq - 1).bit_length())

        def tie_body(i, mcur):
            cand = mcur | lax.shift_left(jnp.int32(1), nbits - 1 - i)
            cnt = count(lambda kch, row: (kch == thr) & (row < cand))
            return jnp.where(cnt < need, cand, mcur)

        mt_ref[...] = lax.fori_loop(0, nbits, tie_body, jnp.zeros((1, QB), I32))

    mtie = jnp.where(thr == INT_MIN, -1, mt_ref[...])

    def selected(ks, width, hi_limit):
        kch = key_ref[pl.ds(ks, width), :]
        row = ks + lax.broadcasted_iota(I32, (width, 1), 0)
        sel = (kch > thr) | ((kch == thr) & (row <= mtie))
        if hi_limit is not None:
            sel = sel & (row < hi_limit)
        return sel

    pair_cols = [slice(2 * g * QB, (2 * g + 2) * QB) for g in range(n_heads // 2)]

    def lat_chunk(ks, width):
        return lat_ref[0, pl.ds(ks, width), :].astype(BF16)

    acc_ref[...] = jnp.zeros(acc_ref.shape, F32)
    p_ref[...] = jnp.zeros(p_ref.shape, BF16)


    def lat_t_ones(ks, width):
        return latt_ref[:, pl.ds(ks, width)]

    def far_body(c, ms):
        ksp = pl.multiple_of(jnp.maximum(c - 1, 0) * kc, kc)
        latp_t = lat_t_ones(ksp, kc)
        ksn = pl.multiple_of(jnp.where(c + 1 < n_far, (c + 1) * kc, near_ld), QB)
        latn = lat_chunk(ksn, kc)
        ks = pl.multiple_of(c * kc, kc)
        madd_ref[...] = jnp.where(selected(ks, kc, far_end), 0.0, NEG_MASK)
        new_ms = list(ms)
        for g, cols in enumerate(pair_cols):
            pv_g = _dot(latp_t, jnp.concatenate([p_ref[2 * g], p_ref[2 * g + 1]], axis=1))
            lg_next = _dot(latn, wqk_ref[:, cols])
            for h in (2 * g, 2 * g + 1):
                hc = slice(h * QB, (h + 1) * QB)
                bfar = far_ref[h]
                lgm = lg_ref[h] + madd_ref[...]
                m_new = jnp.maximum(ms[h], jnp.max(lgm, axis=0, keepdims=True) + bfar)
                p = jnp.exp2(lgm - (m_new - bfar))
                alpha = jnp.exp2(ms[h] - m_new)
                new_ms[h] = m_new
                p_ref[h] = p.astype(BF16)
                acc_ref[h] = (acc_ref[h] + pv_g[:, (h - 2 * g) * QB:(h - 2 * g + 1) * QB]) * alpha
            lg_ref[2 * g] = lg_next[:, 0:QB]
            lg_ref[2 * g + 1] = lg_next[:, QB:]
        return tuple(new_ms)

    ms = lax.fori_loop(0, n_far, far_body,
                       tuple(jnp.full((1, QB), NEG_INIT, F32) for _ in range(n_heads)))
    ks_last = pl.multiple_of(jnp.maximum(n_far - 1, 0) * kc, kc)
    pv = _dot(lat_t_ones(ks_last, kc),
              jnp.concatenate([p_ref[h] for h in range(n_heads)], axis=1))

    tile_id = jnp.where(qb == 0, 1, 0)
    sel = selected(near_start, NEAR, None)
    ps, alphas = [], []
    for h in range(n_heads):
        lgm = jnp.where(sel, lg_ref[h, pl.ds(near_off, NEAR), :] + bias_ref[tile_id, h], NEG_MASK)
        m_new = jnp.maximum(ms[h], jnp.max(lgm, axis=0, keepdims=True))
        alphas.append(jnp.exp2(ms[h] - m_new))
        ps.append(jnp.exp2(lgm - m_new).astype(BF16))
    pvn = _dot(lat_t_ones(near_start, NEAR), jnp.concatenate(ps, axis=1))
    o_parts = []
    for h in range(n_heads):
        tot = (acc_ref[h] + head(pv, h)) * alphas[h] + head(pvn, h)
        o_h = tot[0:c_lat] / tot[c_lat:c_lat + 1]
        o_parts.append(o_h.astype(BF16))
    yb = _dot(jnp.concatenate(o_parts, axis=0), wuv_ref[...], _TN)
    out_ref[0] = yb.astype(out_ref.dtype)


def _dsa_call(q, qi, wi, ki2, lat, wuk_t, wuv_bd, bias_near, bias_far, *, topk, kc):
    bsz, s, aw = q.shape
    n_heads = aw // HEAD_DIM
    c_lat = lat.shape[-1]
    nb = s // Q_BLOCK
    hq = n_heads * Q_BLOCK
    const = lambda shape: pl.BlockSpec(shape, lambda b, j: tuple(0 for _ in shape))
    blk = lambda w: pl.BlockSpec((1, Q_BLOCK, w), lambda b, j: (b, j, 0))
    full = lambda w: pl.BlockSpec((1, s, w), lambda b, j: (b, 0, 0))
    return pl.pallas_call(
        functools.partial(_dsa_kernel, topk=topk, kc=kc, n_heads=n_heads, seq=s,
                          scale=float(HEAD_DIM) ** -0.5),
        grid=(bsz, nb),
        in_specs=[
            blk(aw), blk(qi.shape[-1]), blk(LANES), full(LANES), full(c_lat),
            const(wuk_t.shape), const(wuv_bd.shape), const(bias_near.shape), const(bias_far.shape),
        ],
        out_specs=blk(aw),
        out_shape=jax.ShapeDtypeStruct((bsz, s, aw), BF16),
        scratch_shapes=[
            pltpu.VMEM((s, Q_BLOCK), I32),
            pltpu.VMEM((n_heads, c_lat + ONES_ROWS, Q_BLOCK), F32),
            pltpu.VMEM((n_heads, kc, Q_BLOCK), F32),
            pltpu.VMEM((n_heads, kc, Q_BLOCK), BF16),
            pltpu.VMEM((c_lat, hq), BF16),
            pltpu.VMEM((c_lat, hq), BF16),
            pltpu.VMEM((kc, Q_BLOCK), F32),
            pltpu.VMEM((1, Q_BLOCK), I32),
            pltpu.VMEM((32, s // 32, Q_BLOCK), I32),
            pltpu.VMEM((c_lat + ONES_ROWS, s), BF16),
        ],
        compiler_params=pltpu.CompilerParams(
            dimension_semantics=("arbitrary", "arbitrary"),
            vmem_limit_bytes=VMEM_LIMIT_BYTES),
        name="dsa_attention",
    )(q, qi, wi, ki2, lat, wuk_t, wuv_bd, bias_near, bias_far)


def _out_kernel(ya_ref, za_ref, yb_ref, zb_ref, ga_ref, gb_ref, x_ref, mod_ref,
                wpa_ref, wpb_ref, wo_ref, fg_ref, o_ref, *, d, final):
    za = za_ref[0].astype(F32)
    zb = zb_ref[0].astype(F32)
    ua = (ya_ref[0].astype(F32) * (za * _sigmoid(za))).astype(BF16)
    ub = (yb_ref[0].astype(F32) * (zb * _sigmoid(zb))).astype(BF16)
    br_a = _dot(ua, wpa_ref[...])
    br_b = _dot(ub, wpb_ref[...])
    merged = _sigmoid(ga_ref[0].astype(F32)) * br_a + _sigmoid(gb_ref[0].astype(F32)) * br_b
    gate = mod_ref[0, :, 2 * d:3 * d]
    xo = x_ref[0] + gate * _dot(merged.astype(BF16), wo_ref[...])
    if final:
        xo = xo * lax.rsqrt(jnp.mean(xo * xo, axis=-1, keepdims=True) + NORM_EPS) * fg_ref[...]
    o_ref[0] = xo


def _out_call(ya, za, yb, zb, ga, gb, x, mod, wpa, wpb, wo, final_g, *, final, tm):
    bsz, s, d = x.shape
    d3 = mod.shape[-1]
    row = lambda w: pl.BlockSpec((1, tm, w), lambda b, j: (b, j, 0))
    const = lambda shape: pl.BlockSpec(shape, lambda b, j: tuple(0 for _ in shape))
    return pl.pallas_call(
        functools.partial(_out_kernel, d=d, final=final),
        grid=(bsz, s // tm),
        in_specs=[
            row(ya.shape[-1]), row(za.shape[-1]), row(yb.shape[-1]), row(zb.shape[-1]),
            row(d), row(d), row(d),
            pl.BlockSpec((1, 1, d3), lambda b, j: (b, 0, 0)),
            const(wpa.shape), const(wpb.shape), const(wo.shape), const((1, d)),
        ],
        out_specs=row(d),
        out_shape=jax.ShapeDtypeStruct((bsz, s, d), F32),
        compiler_params=pltpu.CompilerParams(
            dimension_semantics=("arbitrary", "arbitrary"),
            vmem_limit_bytes=VMEM_LIMIT_BYTES),
        name="gated_out",
    )(ya, za, yb, zb, ga, gb, x, mod.reshape(bsz, 1, d3), wpa, wpb, wo, final_g.reshape(1, d))


def _t5_bucket(dist, num_buckets):
    max_exact = num_buckets // 2
    is_small = dist < max_exact
    dd = jnp.maximum(dist, 1).astype(F32)
    large = max_exact + (jnp.log(dd / max_exact) / math.log(MAX_DISTANCE / max_exact)
                         * (num_buckets - max_exact)).astype(I32)
    large = jnp.minimum(large, num_buckets - 1)
    return jnp.where(is_small, dist, large)


def _bias_tiles(rel_bias):
    nbk = rel_bias.shape[0]
    i = jnp.arange(Q_BLOCK, dtype=I32)[:, None]
    jj = jnp.arange(2 * Q_BLOCK, dtype=I32)[None, :]
    tiles = []
    for base in (Q_BLOCK, 0):
        bucket = _t5_bucket(jnp.maximum(base + i - jj, 0), nbk)
        onehot = (bucket[None, :, :] == jnp.arange(nbk, dtype=I32)[:, None, None]).astype(F32)
        tiles.append(jnp.einsum('nh,nij->hji', rel_bias, onehot,
                                precision=lax.Precision.HIGHEST))
    far = rel_bias[_t5_bucket(jnp.full((1, 1), MAX_DISTANCE, I32), nbk)]
    return jnp.stack(tiles, axis=0) * LOG2E, jnp.moveaxis(far, -1, 0) * LOG2E


def _block_diag_heads(w, rows_per_head, cols_per_head):
    n = w.shape[0]
    eye = jnp.eye(n, dtype=w.dtype)
    return jnp.einsum('hrc,hg->hrgc', w, eye).reshape(n * rows_per_head, n * cols_per_head)


def kernel(x, c, ada_w, ada_b, norm_g, w_in, shift_mu, w0, w2, a0, a2, k_k, k_a, r_k, lnx_g, lnx_b,
           kv_norm_g, w_uk, w_uv, w_pa, w_pb, w_o, rel_bias, final_g):
    bsz, s, d = x.shape
    depth = w_in.shape[0]
    rw = w0.shape[-1]
    c_lat, n_heads, dh = w_uk.shape[1:]
    aw = n_heads * dh
    iw = IDX_HEADS * IDX_HEAD_DIM
    n_shift = 3 * rw + DECAY_LORA + ICLR_LORA
    topk = min(TOPK_MAX, s // 4)
    assert DECAY_LORA + ICLR_LORA == LANES and dh == HEAD_DIM and IDX_HEAD_DIM == HEAD_DIM
    assert s % (2 * Q_BLOCK) == 0 and rw % LANES == 0

    sizes = (rw, rw, rw, DECAY_LORA, ICLR_LORA, rw, aw, c_lat, aw, iw, IDX_HEAD_DIM, IDX_HEADS, d, d)
    offs = [0]
    for sz in sizes:
        offs.append(offs[-1] + sz)
    col = lambda i: slice(offs[i], offs[i + 1])
    widths = (rw, aw, c_lat, aw, iw, LANES, LANES, d, d)

    mod = _mod_call(c, ada_w, ada_b)
    bias_near, bias_far = _bias_tiles(rel_bias)
    ones_bd = _block_diag_heads(jnp.ones((rw // HEAD_DIM, HEAD_DIM, HEAD_DIM), BF16), HEAD_DIM, HEAD_DIM)
    kc = 512 if s % 512 == 0 else 2 * Q_BLOCK
    tm_in = 512 if s % 512 == 0 else 256
    tm_out = 512 if s % 512 == 0 else 256

    for l in range(depth):
        w = w_in[l]
        wi_pad = jnp.zeros((d, LANES - IDX_HEADS), w.dtype)
        w_packed = jnp.concatenate(
            [w[:, 0:n_shift], w[:, col(5)], w[:, col(6)], w[:, col(7)], w[:, col(8)], w[:, col(9)],
             w[:, col(10)], w[:, col(10)], w[:, col(11)], wi_pad, w[:, col(12)], w[:, col(13)]],
            axis=1).astype(BF16)
        sh, za, q, lat, zb, qi, ki2, wi, ga, gb = _inproj_call(
            x, mod[l], norm_g[l], w_packed, shift_mu[l], kv_norm_g[l],
            n_shift=n_shift, widths=widths, tm=tm_in)

        vecs = jnp.stack([w0[l], a0[l], k_k[l], k_a[l], r_k[l].reshape(-1), lnx_g[l], lnx_b[l],
                          jnp.zeros((rw,), F32)], axis=0)
        w2p = jnp.concatenate([w2[l], jnp.zeros((ICLR_LORA, rw), F32)], axis=0)
        a2p = jnp.concatenate([jnp.zeros((DECAY_LORA, rw), F32), a2[l]], axis=0)
        ya = _rwkv_call(sh, vecs, w2p, a2p, ones_bd, width=rw, tb=4 * CHUNK)

        wuk_t = jnp.transpose(w_uk[l], (1, 0, 2)).astype(BF16)
        wuv_bd = _block_diag_heads(jnp.transpose(w_uv[l], (1, 0, 2)), c_lat, dh).astype(BF16)
        yb = _dsa_call(q, qi, wi, ki2, lat, wuk_t, wuv_bd, bias_near, bias_far, topk=topk, kc=kc)

        x = _out_call(ya, za, yb, zb, ga, gb, x, mod[l], w_pa[l].astype(BF16), w_pb[l].astype(BF16),
                      w_o[l].astype(BF16), final_g, final=(l == depth - 1), tm=tm_out)
    return x
```

```python
import functools
import math

import jax
import jax.numpy as jnp
from jax import lax
from jax.experimental import pallas as pl
from jax.experimental.pallas import tpu as pltpu

F32 = jnp.float32
BF16 = jnp.bfloat16
I32 = jnp.int32

DECAY_LORA = 64
ICLR_LORA = 64
GN_EPS = 64e-5
IDX_HEADS = 8
IDX_HEAD_DIM = 64
TOPK_MAX = 256
Q_BLOCK = 128
MAX_DISTANCE = 128
NORM_EPS = 1e-6

LANES = 128
SUBLANES = 8
VMEM_LIMIT_BYTES = 56 * 1024 * 1024

HEAD_DIM = 64
CHUNK = 64
KEY_BITS = 32
INT_MIN = -(2 ** 31)
INT_MAX = 2 ** 31 - 1
NEG_MASK = -2e30
NEG_INIT = -1e30
LOG2E = 1.4426950408889634
BIT_GROUP = KEY_BITS * SUBLANES
ONES_ROWS = 16


def _sigmoid(x):
    return 1.0 / (1.0 + jnp.exp(-x))


_NN = (((1,), (0,)), ((), ()))
_NT = (((1,), (1,)), ((), ()))
_TN = (((0,), (0,)), ((), ()))


def _dot(a, b, dims=_NN):
    return lax.dot_general(a, b, dims, preferred_element_type=F32)


def _split(x, n):
    parts = []
    rem = x
    for i in range(n):
        p = rem.astype(BF16)
        parts.append(p)
        if i + 1 < n:
            rem = rem - p.astype(F32)
    return parts


def _mm(a, b, passes=1, dims=_NN):
    if passes == 1:
        return _dot(a.astype(BF16), b.astype(BF16), dims)
    if passes == 3:
        a1, a2 = _split(a, 2)
        b1, b2 = _split(b, 2)
        return _dot(a1, b1, dims) + (_dot(a1, b2, dims) + _dot(a2, b1, dims))
    a1, a2, a3 = _split(a, 3)
    b1, b2, b3 = _split(b, 3)
    hi = _dot(a1, b1, dims)
    mid = _dot(a1, b2, dims) + _dot(a2, b1, dims)
    lo = _dot(a2, b2, dims) + (_dot(a1, b3, dims) + _dot(a3, b1, dims))
    return hi + (mid + lo)


def _mm_exact_rhs(a, b_bf16, n):
    out = None
    for p in _split(a, n):
        t = _dot(p, b_bf16)
        out = t if out is None else out + t
    return out


def _const_spec(shape):
    return pl.BlockSpec(shape, lambda b, j: tuple(0 for _ in shape))


_GRID_PARAMS = pltpu.CompilerParams(dimension_semantics=("arbitrary", "arbitrary"),
                                    vmem_limit_bytes=VMEM_LIMIT_BYTES)


def _mod_kernel(c_ref, w_ref, b_ref, o_ref):
    c = c_ref[...]
    ca = c * _sigmoid(c)
    o_ref[0] = _mm(ca, w_ref[0], 6) + b_ref[0]


def _mod_call(c, ada_w, ada_b):
    depth, d, d3 = ada_w.shape
    bsz = c.shape[0]
    tn = d
    return pl.pallas_call(
        _mod_kernel,
        grid=(depth, d3 // tn),
        in_specs=[
            pl.BlockSpec((bsz, d), lambda l, n: (0, 0)),
            pl.BlockSpec((1, d, tn), lambda l, n: (l, 0, n)),
            pl.BlockSpec((1, 1, tn), lambda l, n: (l, 0, n)),
        ],
        out_specs=pl.BlockSpec((1, bsz, tn), lambda l, n: (l, 0, n)),
        out_shape=jax.ShapeDtypeStruct((depth, bsz, d3), F32),
        compiler_params=_GRID_PARAMS,
        name="adaln_mod",
    )(c, ada_w, ada_b.reshape(depth, 1, d3))


def _inproj_kernel(x_ref, mod_ref, g_ref, w_ref, mu_ref, kvg_ref,
                   sh_ref, za_ref, q_ref, lat_ref, zb_ref, qi_ref, ki_ref, wi_ref, ga_ref, gb_ref,
                   carry_ref, *, d, n_shift, widths):
    j = pl.program_id(1)
    xb = x_ref[0]
    tm = xb.shape[0]
    shift = mod_ref[0, :, 0:d]
    scale = mod_ref[0, :, d:2 * d]
    ms = jnp.mean(xb * xb, axis=-1, keepdims=True)
    h = xb * lax.rsqrt(ms + NORM_EPS) * g_ref[...]
    h = h * (1.0 + scale) + shift
    hb = h.astype(BF16)

    @pl.when(j == 0)
    def _():
        carry_ref[...] = jnp.zeros_like(carry_ref)

    ps = _dot(hb, w_ref[:, 0:n_shift])
    prev = pltpu.roll(ps, 1, 0)
    row = lax.broadcasted_iota(I32, (tm, 1), 0)
    prev = jnp.where(row == 0, carry_ref[...], prev)
    carry_ref[...] = ps[tm - 1:tm, :]
    sh_ref[0] = ps + (prev - ps) * mu_ref[...]

    off = n_shift
    outs = (za_ref, q_ref, lat_ref, zb_ref, qi_ref, ki_ref, wi_ref, ga_ref, gb_ref)
    for o_ref, wd in zip(outs, widths):
        p = _dot(hb, w_ref[:, off:off + wd])
        if o_ref is lat_ref:
            p = p * lax.rsqrt(jnp.mean(p * p, axis=-1, keepdims=True) + NORM_EPS) * kvg_ref[...]
        o_ref[0] = p.astype(o_ref.dtype)
        off += wd


def _inproj_call(x, mod, norm_g, w_packed, mu, kvg, *, n_shift, widths, tm):
    bsz, s, d = x.shape
    npk = w_packed.shape[1]
    d3 = mod.shape[-1]
    row_spec = lambda w: pl.BlockSpec((1, tm, w), lambda b, j: (b, j, 0))
    out_widths = (n_shift,) + tuple(widths)
    out_dtypes = (F32, BF16, BF16, BF16, BF16, BF16, BF16, F32, BF16, BF16)
    return pl.pallas_call(
        functools.partial(_inproj_kernel, d=d, n_shift=n_shift, widths=tuple(widths)),
        grid=(bsz, s // tm),
        in_specs=[
            row_spec(d),
            pl.BlockSpec((1, 1, d3), lambda b, j: (b, 0, 0)),
            _const_spec((1, d)),
            pl.BlockSpec((d, npk), lambda b, j: (0, 0), pipeline_mode=pl.Buffered(1)),
            _const_spec((1, n_shift)),
            _const_spec((1, widths[2])),
        ],
        out_specs=[row_spec(w) for w in out_widths],
        out_shape=[jax.ShapeDtypeStruct((bsz, s, w), dt) for w, dt in zip(out_widths, out_dtypes)],
        scratch_shapes=[pltpu.VMEM((1, n_shift), F32)],
        compiler_params=_GRID_PARAMS,
        name="inproj",
    )(x, mod.reshape(bsz, 1, d3), norm_g.reshape(1, d), w_packed, mu.reshape(1, n_shift),
      kvg.reshape(1, -1))


def _rwkv_kernel(sh_ref, vec_ref, w2_ref, a2_ref, ones_ref, out_ref, st_ref, *, width):
    j = pl.program_id(1)
    L = CHUNK
    n_pairs = width // LANES

    @pl.when(j == 0)
    def _():
        st_ref[...] = jnp.zeros_like(st_ref)

    blk = sh_ref[0]
    pr = blk[:, 0:width]
    pk = blk[:, width:2 * width]
    pv = blk[:, 2 * width:3 * width]
    lora = blk[:, 3 * width:3 * width + LANES]
    lane = lax.broadcasted_iota(I32, (1, LANES), 1)
    lo_half = lane < HEAD_DIM
    t_in = jnp.where(lo_half, jnp.tanh(lora), lora)
    dec_in = _mm(t_in, w2_ref[...], 3)
    a_in = _mm(t_in, a2_ref[...], 3)

    w0 = vec_ref[0:1, :]
    a0 = vec_ref[1:2, :]
    k_k = vec_ref[2:3, :]
    k_a = vec_ref[3:4, :]
    r_k = vec_ref[4:5, :]
    lnx_g = vec_ref[5:6, :]
    lnx_b = vec_ref[6:7, :]
    ones_bd = ones_ref[...]

    z = -(w0 + dec_in)
    sp = jnp.maximum(z, 0.0) + jnp.log(1.0 + jnp.exp(-jnp.abs(z)))
    logw = -jnp.exp(-sp - 0.5)
    a = _sigmoid(a0 + a_in)
    kk0 = pk * k_k
    ss = _mm_exact_rhs(kk0 * kk0, ones_bd, 2)
    kk = kk0 / jnp.maximum(jnp.sqrt(ss), 1e-12)
    k_eff = pk * (1.0 + (a - 1.0) * k_a)
    a_vec = -kk
    b_vec = kk * a

    tb = blk.shape[0]
    n_chunks = tb // L
    r_i = lax.broadcasted_iota(I32, (tb, tb), 0)
    c_i = lax.broadcasted_iota(I32, (tb, tb), 1)
    tril = jnp.where(((r_i // L) == (c_i // L)) & (c_i <= r_i), 1.0, 0.0).astype(BF16)
    cl = None
    for part in _split(logw, 3):
        t = _dot(tril, part)
        cl = t if cl is None else cl + t
    cl_last = jnp.concatenate(
        [jnp.broadcast_to(cl[c * L + L - 1:c * L + L, :], (L, width)) for c in range(n_chunks)], axis=0)
    rt = pr * jnp.exp(cl)
    at = a_vec * jnp.exp(cl - logw)
    e_inv = jnp.exp(-cl)
    bt = b_vec * e_inv
    kt = k_eff * e_inv
    e_l = jnp.exp(cl_last - cl)
    bl = b_vec * e_l
    kl = k_eff * e_l
    p_last = jnp.exp(cl_last)

    rr = lax.broadcasted_iota(I32, (LANES, LANES), 0)
    cc = lax.broadcasted_iota(I32, (LANES, LANES), 1)
    same = (rr // L) == (cc // L)
    strict = same & ((cc % L) < (rr % L))
    eye = rr == cc
    eye_f = jnp.where(eye, 1.0, 0.0)
    rr2 = lax.broadcasted_iota(I32, (LANES, 2 * LANES), 0)
    cc2 = lax.broadcasted_iota(I32, (LANES, 2 * LANES), 1)
    incl2 = ((rr2 // L) == ((cc2 % LANES) // L)) & ((cc2 % L) <= (rr2 % L))

    def stack(x):
        return jnp.concatenate([jnp.where(lo_half, x, 0.0), jnp.where(lo_half, 0.0, x)], axis=0)

    chains = [(c, p) for c in range(n_chunks) for p in range(n_pairs)]

    def piece(x, c, p):
        return x[c * L:(c + 1) * L, p * LANES:(p + 1) * LANES]

    atm, rtm, vm, blkl, a_ab, a_ak, a_rbk = {}, {}, {}, {}, {}, {}, {}
    for ch in chains:
        atm[ch], rtm[ch] = stack(piece(at, *ch)), stack(piece(rt, *ch))
        vm[ch] = stack(piece(pv, *ch)).astype(BF16)
        btm, ktm = stack(piece(bt, *ch)), stack(piece(kt, *ch))
        blkl[ch] = jnp.concatenate([stack(piece(bl, *ch)), stack(piece(kl, *ch))], axis=0).astype(BF16)
        g = _mm(jnp.concatenate([atm[ch], rtm[ch]], axis=0), jnp.concatenate([btm, ktm], axis=0),
                dims=_NT)
        a_ab[ch] = jnp.where(strict, g[0:LANES, 0:LANES], 0.0)
        a_ak[ch] = jnp.where(strict, g[0:LANES, LANES:], 0.0)
        a_rbk[ch] = jnp.where(incl2, g[LANES:, :], 0.0).astype(BF16)
    tinv = {ch: eye_f + a_ab[ch] for ch in chains}
    qn = {ch: _mm(a_ab[ch], a_ab[ch]) for ch in chains}
    sq = 2
    while 2 * sq < L:
        r = {ch: _mm(qn[ch], jnp.concatenate([qn[ch], tinv[ch]], axis=1)) for ch in chains}
        for ch in chains:
            qn[ch] = r[ch][:, 0:LANES]
            tinv[ch] = tinv[ch] + r[ch][:, LANES:]
        sq *= 2
    tinv = {ch: tinv[ch] + _mm(qn[ch], tinv[ch]) for ch in chains}
    akv = {ch: _mm(a_ak[ch], vm[ch]) for ch in chains}
    wz = {ch: _mm(tinv[ch], jnp.concatenate([atm[ch], akv[ch]], axis=1)) for ch in chains}
    wm = {ch: wz[ch][:, 0:LANES].astype(BF16) for ch in chains}
    zv = {ch: jnp.concatenate([wz[ch][:, LANES:].astype(BF16), vm[ch]], axis=0) for ch in chains}
    rqm = {ch: rtm[ch] + _mm(a_rbk[ch][:, 0:LANES], wm[ch]) for ch in chains}
    y0m = {ch: _mm(a_rbk[ch], zv[ch]) for ch in chains}
    mt = {ch: jnp.where(eye, piece(p_last, *ch)[0:1], 0.0)
          + _mm(blkl[ch][0:LANES], wm[ch], dims=_TN) for ch in chains}
    ct = {ch: _mm(blkl[ch], zv[ch], dims=_TN) for ch in chains}

    st = {p: st_ref[p] for p in range(n_pairs)}
    y_rows = []
    for c in range(n_chunks):
        ys = []
        for p in range(n_pairs):
            ym = _mm(rqm[(c, p)], st[p]) + y0m[(c, p)]
            ys.append(ym[0:L] + ym[L:])
            st[p] = _mm(mt[(c, p)], st[p]) + ct[(c, p)]
        y_rows.append(jnp.concatenate(ys, axis=1))
    for p in range(n_pairs):
        st_ref[p] = st[p]
    y = jnp.concatenate(y_rows, axis=0)

    inv_n = 1.0 / HEAD_DIM
    mu = _mm_exact_rhs(y, ones_bd, 2) * inv_n
    dlt = y - mu
    var = _mm_exact_rhs(dlt * dlt, ones_bd, 2) * inv_n
    yn = dlt * lax.rsqrt(var + GN_EPS) * lnx_g + lnx_b
    bonus = _mm_exact_rhs(pr * k_eff * r_k, ones_bd, 2) * pv
    out_ref[0] = (yn + bonus).astype(out_ref.dtype)


def _rwkv_call(sh, vecs, w2p, a2p, ones_bd, *, width, tb):
    bsz, s, n_shift = sh.shape
    n_pairs = width // LANES
    return pl.pallas_call(
        functools.partial(_rwkv_kernel, width=width),
        grid=(bsz, s // tb),
        in_specs=[
            pl.BlockSpec((1, tb, n_shift), lambda b, j: (b, j, 0)),
            _const_spec(vecs.shape),
            _const_spec(w2p.shape),
            _const_spec(a2p.shape),
            _const_spec(ones_bd.shape),
        ],
        out_specs=pl.BlockSpec((1, tb, width), lambda b, j: (b, j, 0)),
        out_shape=jax.ShapeDtypeStruct((bsz, s, width), BF16),
        scratch_shapes=[pltpu.VMEM((n_pairs, LANES, LANES), F32)],
        compiler_params=_GRID_PARAMS,
        name="rwkv7_scan",
    )(sh, vecs, w2p, a2p, ones_bd)


def _float_to_key(x):
    bits = pltpu.bitcast(x, I32)
    return bits ^ ((bits >> 31) & INT_MAX)


def _key_to_float(k):
    return pltpu.bitcast(k ^ ((k >> 31) & INT_MAX), F32)


def _bit_transpose32(a):
    a = list(a)
    m, j = 0x0000FFFF, 16
    while j:
        k = 0
        while k < KEY_BITS:
            t = (a[k] ^ lax.shift_right_logical(a[k + j], jnp.int32(j))) & m
            a[k] = a[k] ^ t
            a[k + j] = a[k + j] ^ lax.shift_left(t, jnp.int32(j))
            k = (k + j + 1) & ~j
        j >>= 1
        m = (m ^ (m << j)) & 0xFFFFFFFF
    return a


def _dsa_kernel(q_ref, qi_ref, wi_ref, ki_ref, lat_ref, wukt_ref, wuv_ref, bias_ref, far_ref, out_ref,
                score_ref, acc_ref, lg_ref, p_ref, qiw_ref, wqk_ref, madd_ref, mt_ref, planes_ref, latt_ref,
                thr_ref, cnt_ref,
                *, topk, kc, n_heads, seq, scale):
    qb = pl.program_id(1)
    QB = Q_BLOCK
    NEAR = 2 * QB
    E = q_ref.shape[0]
    ES = range(E)
    n_pairs = n_heads // 2
    c_lat = lat_ref.shape[-1]
    q_end = (qb + 1) * QB
    n_idx = (jnp.maximum(q_end, NEAR) + kc - 1) // kc
    tpos = qb * QB + lax.broadcasted_iota(I32, (1, QB), 1)
    pair_cols = [slice(2 * g * QB, (2 * g + 2) * QB) for g in range(n_pairs)]

    def head(x, h):
        return x[:, h * QB:(h + 1) * QB]

    wi_t = []
    for e in ES:
        qi_t = qi_ref[e].T
        qi_w = jnp.concatenate([qi_t[h * HEAD_DIM:(h + 1) * HEAD_DIM, :] for h in range(n_heads)], axis=1)
        qiw_ref[e] = jnp.concatenate([qi_w, jnp.zeros_like(qi_w)], axis=0).astype(BF16)
        wi_t.append(wi_ref[e].T)

    def idx_body(c, carry):
        ks = pl.multiple_of(c * kc, kc)
        kib = [ki_ref[e, pl.ds(ks, kc), :].astype(BF16) for e in ES]
        score = [None] * E
        for g in range(n_pairs):
            for e in ES:
                s_g = jnp.maximum(_dot(kib[e], qiw_ref[e, :, pair_cols[g]]), 0.0)
                part = (wi_t[e][2 * g:2 * g + 1, :] * s_g[:, 0:QB]
                        + wi_t[e][2 * g + 1:2 * g + 2, :] * s_g[:, QB:])
                score[e] = part if score[e] is None else score[e] + part
        spos = ks + lax.broadcasted_iota(I32, (kc, 1), 0)
        for e in ES:
            sc = jnp.where(spos <= tpos, score[e] + 0.0, -jnp.inf)
            score_ref[e, pl.ds(ks, kc), :] = sc
            ukey = _float_to_key(sc) ^ INT_MIN
            for g in range(kc // BIT_GROUP):
                regs = _bit_transpose32(
                    [ukey[g * BIT_GROUP + SUBLANES * i:g * BIT_GROUP + SUBLANES * (i + 1), :]
                     for i in range(KEY_BITS)])
                row0 = pl.multiple_of(c * (kc // KEY_BITS) + SUBLANES * g, SUBLANES)
                for b in range(KEY_BITS):
                    planes_ref[e, b, pl.ds(row0, SUBLANES), :] = regs[KEY_BITS - 1 - b]
        return carry

    @pl.when(qb == 0)
    def _():
        for e in ES:
            for c0 in range(0, seq, kc):
                latt_ref[e, 0:c_lat, c0:c0 + kc] = lat_ref[e, c0:c0 + kc, :].T.astype(BF16)
            latt_ref[e, c_lat:, :] = jnp.ones((ONES_ROWS, seq), BF16)

    @pl.when((pl.program_id(0) == 0) & (qb == 0))
    def _():
        planes_ref[...] = jnp.zeros(planes_ref.shape, I32)

    lax.fori_loop(0, n_idx, idx_body, 0)

    q_t = [q_ref[e].T.astype(BF16) for e in ES]
    for h in range(n_heads):
        for e in ES:
            qa = _dot(wukt_ref[h], q_t[e][h * HEAD_DIM:(h + 1) * HEAD_DIM, :])
            wqk_ref[e, :, h * QB:(h + 1) * QB] = (qa * (scale * LOG2E)).astype(BF16)
    far_end = jnp.maximum(qb - 1, 0) * QB
    n_far = (far_end + kc - 1) // kc
    near_start = pl.multiple_of(far_end, QB)
    near_ld = pl.multiple_of(jnp.minimum(near_start, seq - kc), QB)
    near_off = pl.multiple_of(near_start - near_ld, QB)
    ks0 = pl.multiple_of(jnp.where(n_far > 0, 0, near_ld), QB)

    def lat_chunk(e, ks, width):
        return lat_ref[e, pl.ds(ks, width), :].astype(BF16)

    def store_logits(e, g, lg):
        lg_ref[e, 2 * g] = lg[:, 0:QB]
        lg_ref[e, 2 * g + 1] = lg[:, QB:]

    lat0 = [lat_chunk(e, ks0, kc) for e in ES]
    for g in range(n_pairs):
        for e in ES:
            store_logits(e, g, _dot(lat0[e], wqk_ref[e, :, pair_cols[g]]))

    n_rows = seq // KEY_BITS
    valid = lax.broadcasted_iota(I32, (n_rows, 1), 0) < n_idx * (kc // KEY_BITS)
    kf = float(topk)

    def colsum(w):
        return jnp.sum(lax.population_count(w).astype(F32), axis=0, keepdims=True)

    def bit_body(i, carry):
        alive, c_gt, thr_u = (list(t) for t in carry)
        b0 = KEY_BITS - 2 - 2 * i
        w0, w11, w10, w01 = [], [], [], []
        for e in ES:
            p1 = planes_ref[e, b0 + 1]
            p0 = planes_ref[e, b0]
            w1 = alive[e] & p1
            w0.append(alive[e] ^ w1)
            w11.append(w1 & p0)
            w10.append(w1 ^ w11[e])
            w01.append(w0[e] & p0)
        t11 = [c_gt[e] + colsum(w11[e]) for e in ES]
        t10 = [t11[e] + colsum(w10[e]) for e in ES]
        t01 = [t10[e] + colsum(w01[e]) for e in ES]
        for e in ES:
            k11, k10, k01 = t11[e] >= kf, t10[e] >= kf, t01[e] >= kf
            alive[e] = jnp.where(k11, w11[e], jnp.where(k10, w10[e], jnp.where(k01, w01[e], w0[e] ^ w01[e])))
            c_gt[e] = jnp.where(k11, c_gt[e], jnp.where(k10, t11[e], jnp.where(k01, t10[e], t01[e])))
            digit = jnp.where(k11, 3, jnp.where(k10, 2, jnp.where(k01, 1, 0)))
            thr_u[e] = thr_u[e] | lax.shift_left(digit, b0)
        return tuple(alive), tuple(c_gt), tuple(thr_u)

    alive0 = jnp.broadcast_to(jnp.where(valid, -1, 0), (n_rows, QB)).astype(I32)
    _, _, thr_u = lax.fori_loop(
        0, KEY_BITS // 2, bit_body,
        (tuple(alive0 for _ in ES), tuple(jnp.zeros((1, QB), F32) for _ in ES),
         tuple(jnp.zeros((1, QB), I32) for _ in ES)))
    few = (tpos + 1) <= topk

    def counts(e, preds):
        def body(c, accs):
            ks = pl.multiple_of(c * kc, kc)
            sc = score_ref[e, pl.ds(ks, kc), :]
            row = ks + lax.broadcasted_iota(I32, (kc, 1), 0)
            out = []
            for pred, acc in zip(preds, accs):
                hit = jnp.where(pred(sc, row), 1.0, 0.0).reshape(kc // SUBLANES, SUBLANES, QB)
                parts = [hit[i] for i in range(kc // SUBLANES)]
                while len(parts) > 1:
                    parts = [parts[i] + parts[i + 1] for i in range(0, len(parts), 2)]
                out.append(acc + parts[0])
            return tuple(out)
        accs = lax.fori_loop(0, n_idx, body, tuple(jnp.zeros((SUBLANES, QB), F32) for _ in preds))
        return [jnp.sum(a, axis=0, keepdims=True) for a in accs]

    def store_threshold(e, thr_f):
        thr_f = jnp.where(few, -jnp.inf, thr_f)
        gt, ge = counts(e, (lambda sc, row: sc > thr_f, lambda sc, row: sc >= thr_f))
        thr_ref[e] = thr_f
        cnt_ref[e, 0:1, :] = gt
        cnt_ref[e, 1:2, :] = ge

    for e in ES:
        store_threshold(e, _key_to_float(thr_u[e] ^ INT_MIN))

    bad = None
    for e in ES:
        ok = few | ((cnt_ref[e, 0:1, :] < kf) & (cnt_ref[e, 1:2, :] >= kf))
        b = jnp.where(ok, 0.0, 1.0)
        bad = b if bad is None else jnp.maximum(bad, b)

    @pl.when(jnp.max(bad) > 0.0)
    def _():
        for e in ES:
            def bisect_body(i, pre_u, e=e):
                cand_u = pre_u | lax.shift_left(jnp.int32(1), KEY_BITS - 1 - i)
                cand = _key_to_float(cand_u ^ INT_MIN)
                cnt, = counts(e, (lambda sc, row: sc >= cand,))
                return jnp.where(cnt >= kf, cand_u, pre_u)

            pre_u = lax.fori_loop(0, KEY_BITS, bisect_body, jnp.zeros((1, QB), I32))
            store_threshold(e, _key_to_float(pre_u ^ INT_MIN))

    thr = [thr_ref[e] for e in ES]
    need = [kf - cnt_ref[e, 0:1, :] for e in ES]
    excess = None
    for e in ES:
        ex = jnp.where((cnt_ref[e, 1:2, :] > kf) & jnp.logical_not(few), 1.0, 0.0)
        excess = ex if excess is None else jnp.maximum(excess, ex)
        mt_ref[e] = jnp.full((1, QB), INT_MAX, I32)

    @pl.when(jnp.max(excess) > 0.0)
    def _():
        nbits = max(1, (seq - 1).bit_length())
        for e in ES:
            def tie_body(i, mcur, e=e):
                cand = mcur | lax.shift_left(jnp.int32(1), nbits - 1 - i)
                cnt, = counts(e, (lambda sc, row: (sc == thr[e]) & (row < cand),))
                return jnp.where(cnt < need[e], cand, mcur)

            mt_ref[e] = lax.fori_loop(0, nbits, tie_body, jnp.zeros((1, QB), I32))

    mtie = [jnp.where(few, -1, mt_ref[e]) for e in ES]

    def selected(e, ks, width, hi_limit):
        sc = score_ref[e, pl.ds(ks, width), :]
        row = ks + lax.broadcasted_iota(I32, (width, 1), 0)
        sel = (sc > thr[e]) | ((sc == thr[e]) & (row <= mtie[e]))
        if hi_limit is not None:
            sel = sel & (row < hi_limit)
        return sel

    acc_ref[...] = jnp.zeros(acc_ref.shape, F32)
    p_ref[...] = jnp.zeros(p_ref.shape, BF16)

    def lat_t_ones(e, ks, width):
        return latt_ref[e, :, pl.ds(ks, width)]

    def pair_probs(e, g):
        return jnp.concatenate([p_ref[e, 2 * g], p_ref[e, 2 * g + 1]], axis=1)

    def far_body(c, ms_flat):
        ms = [list(ms_flat[e * n_heads:(e + 1) * n_heads]) for e in ES]
        ksp = pl.multiple_of(jnp.maximum(c - 1, 0) * kc, kc)
        ksn = pl.multiple_of(jnp.where(c + 1 < n_far, (c + 1) * kc, near_ld), QB)
        ks = pl.multiple_of(c * kc, kc)
        latp_t = [lat_t_ones(e, ksp, kc) for e in ES]
        latn = [lat_chunk(e, ksn, kc) for e in ES]
        for e in ES:
            madd_ref[e] = jnp.where(selected(e, ks, kc, far_end), 0.0, NEG_MASK)
        for g in range(n_pairs):
            pv_g = [_dot(latp_t[e], pair_probs(e, g)) for e in ES]
            lg_next = [_dot(latn[e], wqk_ref[e, :, pair_cols[g]]) for e in ES]
            for h in (2 * g, 2 * g + 1):
                bfar = far_ref[h]
                for e in ES:
                    lgm = lg_ref[e, h] + madd_ref[e]
                    m_new = jnp.maximum(ms[e][h], jnp.max(lgm, axis=0, keepdims=True) + bfar)
                    p = jnp.exp2(lgm - (m_new - bfar))
                    alpha = jnp.exp2(ms[e][h] - m_new)
                    ms[e][h] = m_new
                    p_ref[e, h] = p.astype(BF16)
                    acc_ref[e, h] = (acc_ref[e, h] + head(pv_g[e], h - 2 * g)) * alpha
            for e in ES:
                store_logits(e, g, lg_next[e])
        return tuple(m for e in ES for m in ms[e])

    ms_flat = lax.fori_loop(0, n_far, far_body,
                            tuple(jnp.full((1, QB), NEG_INIT, F32) for _ in range(E * n_heads)))
    ms = [ms_flat[e * n_heads:(e + 1) * n_heads] for e in ES]
    ks_last = pl.multiple_of(jnp.maximum(n_far - 1, 0) * kc, kc)
    pv = [_dot(lat_t_ones(e, ks_last, kc),
               jnp.concatenate([p_ref[e, h] for h in range(n_heads)], axis=1)) for e in ES]

    tile_id = jnp.where(qb == 0, 1, 0)
    sel = [selected(e, near_start, NEAR, None) for e in ES]
    ps = [[] for _ in ES]
    alphas = [[] for _ in ES]
    for h in range(n_heads):
        for e in ES:
            lgm = jnp.where(sel[e], lg_ref[e, h, pl.ds(near_off, NEAR), :] + bias_ref[tile_id, h], NEG_MASK)
            m_new = jnp.maximum(ms[e][h], jnp.max(lgm, axis=0, keepdims=True))
            alphas[e].append(jnp.exp2(ms[e][h] - m_new))
            ps[e].append(jnp.exp2(lgm - m_new).astype(BF16))
    pvn = [_dot(lat_t_ones(e, near_start, NEAR), jnp.concatenate(ps[e], axis=1)) for e in ES]
    o_stack = []
    for e in ES:
        o_parts = []
        for h in range(n_heads):
            tot = (acc_ref[e, h] + head(pv[e], h)) * alphas[e][h] + head(pvn[e], h)
            o_parts.append((tot[0:c_lat] / tot[c_lat:c_lat + 1]).astype(BF16))
        o_stack.append(jnp.concatenate(o_parts, axis=0))
    for e in ES:
        out_ref[e] = _dot(o_stack[e], wuv_ref[...], _TN).astype(out_ref.dtype)


def _dsa_call(q, qi, wi, ki2, lat, wuk_t, wuv_bd, bias_near, bias_far, *, topk, kc):
    bsz, s, aw = q.shape
    n_heads = aw // HEAD_DIM
    c_lat = lat.shape[-1]
    nb = s // Q_BLOCK
    hq = n_heads * Q_BLOCK
    e = 2 if bsz % 2 == 0 else 1
    blk = lambda w: pl.BlockSpec((e, Q_BLOCK, w), lambda b, j: (b, j, 0))
    full = lambda w: pl.BlockSpec((e, s, w), lambda b, j: (b, 0, 0))
    return pl.pallas_call(
        functools.partial(_dsa_kernel, topk=topk, kc=kc, n_heads=n_heads, seq=s,
                          scale=float(HEAD_DIM) ** -0.5),
        grid=(bsz // e, nb),
        in_specs=[
            blk(aw), blk(qi.shape[-1]), blk(LANES), full(LANES), full(c_lat),
            _const_spec(wuk_t.shape), _const_spec(wuv_bd.shape), _const_spec(bias_near.shape),
            _const_spec(bias_far.shape),
        ],
        out_specs=blk(aw),
        out_shape=jax.ShapeDtypeStruct((bsz, s, aw), BF16),
        scratch_shapes=[
            pltpu.VMEM((e, s, Q_BLOCK), F32),
            pltpu.VMEM((e, n_heads, c_lat + ONES_ROWS, Q_BLOCK), F32),
            pltpu.VMEM((e, n_heads, kc, Q_BLOCK), F32),
            pltpu.VMEM((e, n_heads, kc, Q_BLOCK), BF16),
            pltpu.VMEM((e, LANES, hq), BF16),
            pltpu.VMEM((e, c_lat, hq), BF16),
            pltpu.VMEM((e, kc, Q_BLOCK), F32),
            pltpu.VMEM((e, 1, Q_BLOCK), I32),
            pltpu.VMEM((e, KEY_BITS, s // KEY_BITS, Q_BLOCK), I32),
            pltpu.VMEM((e, c_lat + ONES_ROWS, s), BF16),
            pltpu.VMEM((e, 1, Q_BLOCK), F32),
            pltpu.VMEM((e, SUBLANES, Q_BLOCK), F32),
        ],
        compiler_params=_GRID_PARAMS,
        name="dsa_attention",
    )(q, qi, wi, ki2, lat, wuk_t, wuv_bd, bias_near, bias_far)


def _out_kernel(ya_ref, za_ref, yb_ref, zb_ref, ga_ref, gb_ref, x_ref, mod_ref,
                wpa_ref, wpb_ref, wo_ref, fg_ref, o_ref, *, d, final):
    za = za_ref[0].astype(F32)
    zb = zb_ref[0].astype(F32)
    ua = (ya_ref[0].astype(F32) * (za * _sigmoid(za))).astype(BF16)
    ub = (yb_ref[0].astype(F32) * (zb * _sigmoid(zb))).astype(BF16)
    br_a = _dot(ua, wpa_ref[...])
    br_b = _dot(ub, wpb_ref[...])
    merged = _sigmoid(ga_ref[0].astype(F32)) * br_a + _sigmoid(gb_ref[0].astype(F32)) * br_b
    gate = mod_ref[0, :, 2 * d:3 * d]
    xo = x_ref[0] + gate * _dot(merged.astype(BF16), wo_ref[...])
    if final:
        xo = xo * lax.rsqrt(jnp.mean(xo * xo, axis=-1, keepdims=True) + NORM_EPS) * fg_ref[...]
    o_ref[0] = xo


def _out_call(ya, za, yb, zb, ga, gb, x, mod, wpa, wpb, wo, final_g, *, final, tm):
    bsz, s, d = x.shape
    d3 = mod.shape[-1]
    row = lambda w: pl.BlockSpec((1, tm, w), lambda b, j: (b, j, 0))
    return pl.pallas_call(
        functools.partial(_out_kernel, d=d, final=final),
        grid=(bsz, s // tm),
        in_specs=[
            row(ya.shape[-1]), row(za.shape[-1]), row(yb.shape[-1]), row(zb.shape[-1]),
            row(d), row(d), row(d),
            pl.BlockSpec((1, 1, d3), lambda b, j: (b, 0, 0)),
            _const_spec(wpa.shape), _const_spec(wpb.shape), _const_spec(wo.shape), _const_spec((1, d)),
        ],
        out_specs=row(d),
        out_shape=jax.ShapeDtypeStruct((bsz, s, d), F32),
        compiler_params=_GRID_PARAMS,
        name="gated_out",
    )(ya, za, yb, zb, ga, gb, x, mod.reshape(bsz, 1, d3), wpa, wpb, wo, final_g.reshape(1, d))


def _t5_bucket(dist, num_buckets):
    max_exact = num_buckets // 2
    is_small = dist < max_exact
    dd = jnp.maximum(dist, 1).astype(F32)
    x = jnp.log(dd / max_exact) / math.log(MAX_DISTANCE / max_exact) * (num_buckets - max_exact)
    large = max_exact + jnp.where(x >= 0, jnp.floor(x), jnp.ceil(x)).astype(I32)
    large = jnp.minimum(large, num_buckets - 1)
    return jnp.where(is_small, dist, large)


def _bias_tiles(rel_bias):
    nbk = rel_bias.shape[0]
    i = jnp.arange(Q_BLOCK, dtype=I32)[:, None]
    jj = jnp.arange(2 * Q_BLOCK, dtype=I32)[None, :]
    tiles = []
    for base in (Q_BLOCK, 0):
        bucket = _t5_bucket(jnp.maximum(base + i - jj, 0), nbk)
        onehot = (bucket[None, :, :] == jnp.arange(nbk, dtype=I32)[:, None, None]).astype(F32)
        tiles.append(jnp.einsum('nh,nij->hji', rel_bias, onehot,
                                precision=lax.Precision.HIGHEST))
    far = rel_bias[_t5_bucket(jnp.full((1, 1), MAX_DISTANCE, I32), nbk)]
    return jnp.stack(tiles, axis=0) * LOG2E, jnp.moveaxis(far, -1, 0) * LOG2E


def _block_diag_heads(w, rows_per_head, cols_per_head):
    n = w.shape[0]
    eye = jnp.eye(n, dtype=w.dtype)
    return jnp.einsum('hrc,hg->hrgc', w, eye).reshape(n * rows_per_head, n * cols_per_head)


def _tiles(s):
    big = s % 512 == 0
    return dict(tm_in=512 if big else 2 * Q_BLOCK, tm_out=512 if big else 2 * Q_BLOCK,
                kc=512 if big else 2 * Q_BLOCK, tb=4 * CHUNK)


def kernel(x, c, ada_w, ada_b, norm_g, w_in, shift_mu, w0, w2, a0, a2, k_k, k_a, r_k, lnx_g, lnx_b,
           kv_norm_g, w_uk, w_uv, w_pa, w_pb, w_o, rel_bias, final_g):
    bsz, s, d = x.shape
    depth = w_in.shape[0]
    rw = w0.shape[-1]
    c_lat, n_heads, dh = w_uk.shape[1:]
    aw = n_heads * dh
    iw = IDX_HEADS * IDX_HEAD_DIM
    n_shift = 3 * rw + DECAY_LORA + ICLR_LORA
    topk = min(TOPK_MAX, s // 4)
    assert DECAY_LORA + ICLR_LORA == LANES and dh == HEAD_DIM and IDX_HEAD_DIM == HEAD_DIM
    assert s % (2 * Q_BLOCK) == 0 and rw % LANES == 0 and n_heads % 2 == 0
    tiles = _tiles(s)

    sizes = (rw, rw, rw, DECAY_LORA, ICLR_LORA, rw, aw, c_lat, aw, iw, IDX_HEAD_DIM, IDX_HEADS, d, d)
    offs = [0]
    for sz in sizes:
        offs.append(offs[-1] + sz)
    col = lambda i: slice(offs[i], offs[i + 1])
    widths = (rw, aw, c_lat, aw, iw, LANES, LANES, d, d)

    mod = _mod_call(c, ada_w, ada_b)
    bias_near, bias_far = _bias_tiles(rel_bias)
    ones_bd = _block_diag_heads(jnp.ones((rw // HEAD_DIM, HEAD_DIM, HEAD_DIM), BF16), HEAD_DIM, HEAD_DIM)

    for l in range(depth):
        w = w_in[l]
        wi_pad = jnp.zeros((d, LANES - IDX_HEADS), w.dtype)
        w_packed = jnp.concatenate(
            [w[:, 0:n_shift], w[:, col(5)], w[:, col(6)], w[:, col(7)], w[:, col(8)], w[:, col(9)],
             w[:, col(10)], w[:, col(10)], w[:, col(11)], wi_pad, w[:, col(12)], w[:, col(13)]],
            axis=1).astype(BF16)
        sh, za, q, lat, zb, qi, ki2, wi, ga, gb = _inproj_call(
            x, mod[l], norm_g[l], w_packed, shift_mu[l], kv_norm_g[l],
            n_shift=n_shift, widths=widths, tm=tiles["tm_in"])

        vecs = jnp.stack([w0[l], a0[l], k_k[l], k_a[l], r_k[l].reshape(-1), lnx_g[l], lnx_b[l],
                          jnp.zeros((rw,), F32)], axis=0)
        w2p = jnp.concatenate([w2[l], jnp.zeros((ICLR_LORA, rw), F32)], axis=0)
        a2p = jnp.concatenate([jnp.zeros((DECAY_LORA, rw), F32), a2[l]], axis=0)
        ya = _rwkv_call(sh, vecs, w2p, a2p, ones_bd, width=rw, tb=tiles["tb"])

        wuk_t = jnp.transpose(w_uk[l], (1, 0, 2)).astype(BF16)
        wuv_bd = _block_diag_heads(jnp.transpose(w_uv[l], (1, 0, 2)), c_lat, dh).astype(BF16)
        yb = _dsa_call(q, qi, wi, ki2, lat, wuk_t, wuv_bd, bias_near, bias_far,
                       topk=topk, kc=tiles["kc"])

        x = _out_call(ya, za, yb, zb, ga, gb, x, mod[l], w_pa[l].astype(BF16), w_pb[l].astype(BF16),
                      w_o[l].astype(BF16), final_g, final=(l == depth - 1), tm=tiles["tm_out"])
    return x
```

```python
import functools
import math

import jax
import jax.numpy as jnp
from jax import lax
from jax.experimental import pallas as pl
from jax.experimental.pallas import tpu as pltpu

F32 = jnp.float32
BF16 = jnp.bfloat16
I32 = jnp.int32

DECAY_LORA = 64
ICLR_LORA = 64
GN_EPS = 64e-5
IDX_HEADS = 8
IDX_HEAD_DIM = 64
TOPK_MAX = 256
Q_BLOCK = 128
MAX_DISTANCE = 128
NORM_EPS = 1e-6

LANES = 128
SUBLANES = 8
VMEM_LIMIT_BYTES = 56 * 1024 * 1024

HEAD_DIM = 64
CHUNK = 64
KEY_BITS = 32
INT_MIN = -(2 ** 31)
INT_MAX = 2 ** 31 - 1
NEG_MASK = -2e30
NEG_INIT = -1e30
LOG2E = 1.4426950408889634
BIT_GROUP = KEY_BITS * SUBLANES
ONES_ROWS = 16


def _sigmoid(x):
    return 1.0 / (1.0 + jnp.exp(-x))


_NN = (((1,), (0,)), ((), ()))
_NT = (((1,), (1,)), ((), ()))
_TN = (((0,), (0,)), ((), ()))


def _dot(a, b, dims=_NN):
    return lax.dot_general(a, b, dims, preferred_element_type=F32)


def _split(x, n):
    parts = []
    rem = x
    for i in range(n):
        p = rem.astype(BF16)
        parts.append(p)
        if i + 1 < n:
            rem = rem - p.astype(F32)
    return parts


def _mm(a, b, passes=1, dims=_NN):
    if passes == 1:
        return _dot(a.astype(BF16), b.astype(BF16), dims)
    if passes == 3:
        a1, a2 = _split(a, 2)
        b1, b2 = _split(b, 2)
        return _dot(a1, b1, dims) + (_dot(a1, b2, dims) + _dot(a2, b1, dims))
    a1, a2, a3 = _split(a, 3)
    b1, b2, b3 = _split(b, 3)
    hi = _dot(a1, b1, dims)
    mid = _dot(a1, b2, dims) + _dot(a2, b1, dims)
    lo = _dot(a2, b2, dims) + (_dot(a1, b3, dims) + _dot(a3, b1, dims))
    return hi + (mid + lo)


def _mm_exact_rhs(a, b_bf16, n):
    out = None
    for p in _split(a, n):
        t = _dot(p, b_bf16)
        out = t if out is None else out + t
    return out


def _const_spec(shape):
    return pl.BlockSpec(shape, lambda b, j: tuple(0 for _ in shape))


_GRID_PARAMS = pltpu.CompilerParams(dimension_semantics=("arbitrary", "arbitrary"),
                                    vmem_limit_bytes=VMEM_LIMIT_BYTES)


def _mod_kernel(c_ref, w_ref, b_ref, o_ref):
    c = c_ref[...]
    ca = c * _sigmoid(c)
    o_ref[0] = _mm(ca, w_ref[0], 6) + b_ref[0]


def _mod_call(c, ada_w, ada_b):
    depth, d, d3 = ada_w.shape
    bsz = c.shape[0]
    tn = d
    return pl.pallas_call(
        _mod_kernel,
        grid=(depth, d3 // tn),
        in_specs=[
            pl.BlockSpec((bsz, d), lambda l, n: (0, 0)),
            pl.BlockSpec((1, d, tn), lambda l, n: (l, 0, n)),
            pl.BlockSpec((1, 1, tn), lambda l, n: (l, 0, n)),
        ],
        out_specs=pl.BlockSpec((1, bsz, tn), lambda l, n: (l, 0, n)),
        out_shape=jax.ShapeDtypeStruct((depth, bsz, d3), F32),
        compiler_params=_GRID_PARAMS,
        name="adaln_mod",
    )(c, ada_w, ada_b.reshape(depth, 1, d3))


def _inproj_kernel(x_ref, mod_ref, g_ref, w_ref, mu_ref, kvg_ref,
                   sh_ref, za_ref, q_ref, lat_ref, zb_ref, qi_ref, ki_ref, wi_ref, ga_ref, gb_ref,
                   carry_ref, *, d, n_shift, widths):
    j = pl.program_id(1)
    xb = x_ref[0]
    tm = xb.shape[0]
    shift = mod_ref[0, :, 0:d]
    scale = mod_ref[0, :, d:2 * d]
    ms = jnp.mean(xb * xb, axis=-1, keepdims=True)
    h = xb * lax.rsqrt(ms + NORM_EPS) * g_ref[...]
    h = h * (1.0 + scale) + shift
    hb = h.astype(BF16)

    @pl.when(j == 0)
    def _():
        carry_ref[...] = jnp.zeros_like(carry_ref)

    ps = _dot(hb, w_ref[:, 0:n_shift])
    prev = pltpu.roll(ps, 1, 0)
    row = lax.broadcasted_iota(I32, (tm, 1), 0)
    prev = jnp.where(row == 0, carry_ref[...], prev)
    carry_ref[...] = ps[tm - 1:tm, :]
    sh_ref[0] = ps + (prev - ps) * mu_ref[...]

    off = n_shift
    outs = (za_ref, q_ref, lat_ref, zb_ref, qi_ref, ki_ref, wi_ref, ga_ref, gb_ref)
    for o_ref, wd in zip(outs, widths):
        p = _dot(hb, w_ref[:, off:off + wd])
        if o_ref is lat_ref:
            p = p * lax.rsqrt(jnp.mean(p * p, axis=-1, keepdims=True) + NORM_EPS) * kvg_ref[...]
        o_ref[0] = p.astype(o_ref.dtype)
        off += wd


def _inproj_call(x, mod, norm_g, w_packed, mu, kvg, *, n_shift, widths, tm):
    bsz, s, d = x.shape
    npk = w_packed.shape[1]
    d3 = mod.shape[-1]
    row_spec = lambda w: pl.BlockSpec((1, tm, w), lambda b, j: (b, j, 0))
    out_widths = (n_shift,) + tuple(widths)
    out_dtypes = (F32, BF16, BF16, BF16, BF16, BF16, BF16, F32, BF16, BF16)
    return pl.pallas_call(
        functools.partial(_inproj_kernel, d=d, n_shift=n_shift, widths=tuple(widths)),
        grid=(bsz, s // tm),
        in_specs=[
            row_spec(d),
            pl.BlockSpec((1, 1, d3), lambda b, j: (b, 0, 0)),
            _const_spec((1, d)),
            pl.BlockSpec((d, npk), lambda b, j: (0, 0), pipeline_mode=pl.Buffered(1)),
            _const_spec((1, n_shift)),
            _const_spec((1, widths[2])),
        ],
        out_specs=[row_spec(w) for w in out_widths],
        out_shape=[jax.ShapeDtypeStruct((bsz, s, w), dt) for w, dt in zip(out_widths, out_dtypes)],
        scratch_shapes=[pltpu.VMEM((1, n_shift), F32)],
        compiler_params=_GRID_PARAMS,
        name="inproj",
    )(x, mod.reshape(bsz, 1, d3), norm_g.reshape(1, d), w_packed, mu.reshape(1, n_shift),
      kvg.reshape(1, -1))


def _rwkv_kernel(sh_ref, vec_ref, w2_ref, a2_ref, ones_ref, out_ref, st_ref, *, width):
    j = pl.program_id(1)
    L = CHUNK
    n_pairs = width // LANES

    @pl.when(j == 0)
    def _():
        st_ref[...] = jnp.zeros_like(st_ref)

    blk = sh_ref[0]
    pr = blk[:, 0:width]
    pk = blk[:, width:2 * width]
    pv = blk[:, 2 * width:3 * width]
    lora = blk[:, 3 * width:3 * width + LANES]
    lane = lax.broadcasted_iota(I32, (1, LANES), 1)
    lo_half = lane < HEAD_DIM
    t_in = jnp.where(lo_half, jnp.tanh(lora), lora)
    dec_in = _mm(t_in, w2_ref[...])
    a_in = _mm(t_in, a2_ref[...])

    w0 = vec_ref[0:1, :]
    a0 = vec_ref[1:2, :]
    k_k = vec_ref[2:3, :]
    k_a = vec_ref[3:4, :]
    r_k = vec_ref[4:5, :]
    lnx_g = vec_ref[5:6, :]
    lnx_b = vec_ref[6:7, :]
    ones_bd = ones_ref[...]

    z = -(w0 + dec_in)
    sp = jnp.maximum(z, 0.0) + jnp.log(1.0 + jnp.exp(-jnp.abs(z)))
    logw = -jnp.exp(-sp - 0.5)
    a = _sigmoid(a0 + a_in)
    kk0 = pk * k_k
    ss = _mm_exact_rhs(kk0 * kk0, ones_bd, 2)
    kk = kk0 / jnp.maximum(jnp.sqrt(ss), 1e-12)
    k_eff = pk * (1.0 + (a - 1.0) * k_a)
    a_vec = -kk
    b_vec = kk * a

    tb = blk.shape[0]
    n_chunks = tb // L
    r_i = lax.broadcasted_iota(I32, (tb, tb), 0)
    c_i = lax.broadcasted_iota(I32, (tb, tb), 1)
    tril = jnp.where(((r_i // L) == (c_i // L)) & (c_i <= r_i), 1.0, 0.0).astype(BF16)
    cl = None
    for part in _split(logw, 2):
        t = _dot(tril, part)
        cl = t if cl is None else cl + t
    cl_last = jnp.concatenate(
        [jnp.broadcast_to(cl[c * L + L - 1:c * L + L, :], (L, width)) for c in range(n_chunks)], axis=0)
    rt = pr * jnp.exp(cl)
    at = a_vec * jnp.exp(cl - logw)
    e_inv = jnp.exp(-cl)
    bt = b_vec * e_inv
    kt = k_eff * e_inv
    e_l = jnp.exp(cl_last - cl)
    bl = b_vec * e_l
    kl = k_eff * e_l
    p_last = jnp.exp(cl_last)

    rr = lax.broadcasted_iota(I32, (LANES, LANES), 0)
    cc = lax.broadcasted_iota(I32, (LANES, LANES), 1)
    same = (rr // L) == (cc // L)
    strict = same & ((cc % L) < (rr % L))
    eye = rr == cc
    eye_f = jnp.where(eye, 1.0, 0.0)
    rr2 = lax.broadcasted_iota(I32, (LANES, 2 * LANES), 0)
    cc2 = lax.broadcasted_iota(I32, (LANES, 2 * LANES), 1)
    incl2 = ((rr2 // L) == ((cc2 % LANES) // L)) & ((cc2 % L) <= (rr2 % L))

    def stack(x):
        return jnp.concatenate([jnp.where(lo_half, x, 0.0), jnp.where(lo_half, 0.0, x)], axis=0)

    chains = [(c, p) for c in range(n_chunks) for p in range(n_pairs)]

    def piece(x, c, p):
        return x[c * L:(c + 1) * L, p * LANES:(p + 1) * LANES]

    atm, rtm, vm, blkl, a_ab, a_ak, a_rbk = {}, {}, {}, {}, {}, {}, {}
    for ch in chains:
        atm[ch], rtm[ch] = stack(piece(at, *ch)), stack(piece(rt, *ch))
        vm[ch] = stack(piece(pv, *ch)).astype(BF16)
        btm, ktm = stack(piece(bt, *ch)), stack(piece(kt, *ch))
        blkl[ch] = jnp.concatenate([stack(piece(bl, *ch)), stack(piece(kl, *ch))], axis=0).astype(BF16)
        g = _mm(jnp.concatenate([atm[ch], rtm[ch]], axis=0), jnp.concatenate([btm, ktm], axis=0),
                dims=_NT)
        a_ab[ch] = jnp.where(strict, g[0:LANES, 0:LANES], 0.0)
        a_ak[ch] = jnp.where(strict, g[0:LANES, LANES:], 0.0)
        a_rbk[ch] = jnp.where(incl2, g[LANES:, :], 0.0).astype(BF16)
    tinv = {ch: eye_f + a_ab[ch] for ch in chains}
    qn = {ch: _mm(a_ab[ch], a_ab[ch]) for ch in chains}
    sq = 2
    while 2 * sq < L:
        r = {ch: _mm(qn[ch], jnp.concatenate([qn[ch], tinv[ch]], axis=1)) for ch in chains}
        for ch in chains:
            qn[ch] = r[ch][:, 0:LANES]
            tinv[ch] = tinv[ch] + r[ch][:, LANES:]
        sq *= 2
    tinv = {ch: tinv[ch] + _mm(qn[ch], tinv[ch]) for ch in chains}
    akv = {ch: _mm(a_ak[ch], vm[ch]) for ch in chains}
    wz = {ch: _mm(tinv[ch], jnp.concatenate([atm[ch], akv[ch]], axis=1)) for ch in chains}
    wm = {ch: wz[ch][:, 0:LANES].astype(BF16) for ch in chains}
    zv = {ch: jnp.concatenate([wz[ch][:, LANES:].astype(BF16), vm[ch]], axis=0) for ch in chains}
    rqm = {ch: rtm[ch] + _mm(a_rbk[ch][:, 0:LANES], wm[ch]) for ch in chains}
    y0m = {ch: _mm(a_rbk[ch], zv[ch]) for ch in chains}
    mt = {ch: jnp.where(eye, piece(p_last, *ch)[0:1], 0.0)
          + _mm(blkl[ch][0:LANES], wm[ch], dims=_TN) for ch in chains}
    ct = {ch: _mm(blkl[ch], zv[ch], dims=_TN) for ch in chains}

    st = {p: st_ref[p] for p in range(n_pairs)}
    y_rows = []
    for c in range(n_chunks):
        ys = []
        for p in range(n_pairs):
            ym = _mm(rqm[(c, p)], st[p]) + y0m[(c, p)]
            ys.append(ym[0:L] + ym[L:])
            st[p] = _mm(mt[(c, p)], st[p]) + ct[(c, p)]
        y_rows.append(jnp.concatenate(ys, axis=1))
    for p in range(n_pairs):
        st_ref[p] = st[p]
    y = jnp.concatenate(y_rows, axis=0)

    inv_n = 1.0 / HEAD_DIM
    mu = _mm_exact_rhs(y, ones_bd, 2) * inv_n
    dlt = y - mu
    var = _mm_exact_rhs(dlt * dlt, ones_bd, 2) * inv_n
    yn = dlt * lax.rsqrt(var + GN_EPS) * lnx_g + lnx_b
    bonus = _mm_exact_rhs(pr * k_eff * r_k, ones_bd, 1) * pv
    out_ref[0] = (yn + bonus).astype(out_ref.dtype)


def _rwkv_call(sh, vecs, w2p, a2p, ones_bd, *, width, tb):
    bsz, s, n_shift = sh.shape
    n_pairs = width // LANES
    return pl.pallas_call(
        functools.partial(_rwkv_kernel, width=width),
        grid=(bsz, s // tb),
        in_specs=[
            pl.BlockSpec((1, tb, n_shift), lambda b, j: (b, j, 0)),
            _const_spec(vecs.shape),
            _const_spec(w2p.shape),
            _const_spec(a2p.shape),
            _const_spec(ones_bd.shape),
        ],
        out_specs=pl.BlockSpec((1, tb, width), lambda b, j: (b, j, 0)),
        out_shape=jax.ShapeDtypeStruct((bsz, s, width), BF16),
        scratch_shapes=[pltpu.VMEM((n_pairs, LANES, LANES), F32)],
        compiler_params=_GRID_PARAMS,
        name="rwkv7_scan",
    )(sh, vecs, w2p, a2p, ones_bd)


def _float_to_key(x):
    bits = pltpu.bitcast(x, I32)
    return bits ^ ((bits >> 31) & INT_MAX)


def _key_to_float(k):
    return pltpu.bitcast(k ^ ((k >> 31) & INT_MAX), F32)


def _bit_transpose32(a):
    a = list(a)
    m, j = 0x0000FFFF, 16
    while j:
        k = 0
        while k < KEY_BITS:
            t = (a[k] ^ lax.shift_right_logical(a[k + j], jnp.int32(j))) & m
            a[k] = a[k] ^ t
            a[k + j] = a[k + j] ^ lax.shift_left(t, jnp.int32(j))
            k = (k + j + 1) & ~j
        j >>= 1
        m = (m ^ (m << j)) & 0xFFFFFFFF
    return a


def _dsa_kernel(q_ref, qi_ref, wi_ref, ki_ref, lat_ref, wukt_ref, wuv_ref, bias_ref, far_ref, out_ref,
                score_ref, acc_ref, lg_ref, p_ref, qiw_ref, wqk_ref, madd_ref, mt_ref, planes_ref, latt_ref,
                thr_ref, cnt_ref,
                *, topk, kc, n_heads, seq, scale):
    qb = pl.program_id(1)
    QB = Q_BLOCK
    NEAR = 2 * QB
    E = q_ref.shape[0]
    ES = range(E)
    n_pairs = n_heads // 2
    c_lat = lat_ref.shape[-1]
    q_end = (qb + 1) * QB
    n_idx = (jnp.maximum(q_end, NEAR) + kc - 1) // kc
    tpos = qb * QB + lax.broadcasted_iota(I32, (1, QB), 1)
    pair_cols = [slice(2 * g * QB, (2 * g + 2) * QB) for g in range(n_pairs)]

    def head(x, h):
        return x[:, h * QB:(h + 1) * QB]

    wi_t = []
    for e in ES:
        qi_t = qi_ref[e].T
        qi_w = jnp.concatenate([qi_t[h * HEAD_DIM:(h + 1) * HEAD_DIM, :] for h in range(n_heads)], axis=1)
        qiw_ref[e] = jnp.concatenate([qi_w, jnp.zeros_like(qi_w)], axis=0).astype(BF16)
        wi_t.append(wi_ref[e].T)

    def idx_body(c, carry):
        ks = pl.multiple_of(c * kc, kc)
        kib = [ki_ref[e, pl.ds(ks, kc), :].astype(BF16) for e in ES]
        score = [None] * E
        for g in range(n_pairs):
            for e in ES:
                s_g = jnp.maximum(_dot(kib[e], qiw_ref[e, :, pair_cols[g]]), 0.0)
                part = (wi_t[e][2 * g:2 * g + 1, :] * s_g[:, 0:QB]
                        + wi_t[e][2 * g + 1:2 * g + 2, :] * s_g[:, QB:])
                score[e] = part if score[e] is None else score[e] + part
        spos = ks + lax.broadcasted_iota(I32, (kc, 1), 0)
        for e in ES:
            sc = jnp.where(spos <= tpos, score[e] + 0.0, -jnp.inf)
            score_ref[e, pl.ds(ks, kc), :] = sc
            ukey = _float_to_key(sc) ^ INT_MIN
            for g in range(kc // BIT_GROUP):
                regs = _bit_transpose32(
                    [ukey[g * BIT_GROUP + SUBLANES * i:g * BIT_GROUP + SUBLANES * (i + 1), :]
                     for i in range(KEY_BITS)])
                row0 = pl.multiple_of(c * (kc // KEY_BITS) + SUBLANES * g, SUBLANES)
                for b in range(KEY_BITS):
                    planes_ref[e, b, pl.ds(row0, SUBLANES), :] = regs[KEY_BITS - 1 - b]
        return carry

    @pl.when(qb == 0)
    def _():
        for e in ES:
            for c0 in range(0, seq, kc):
                latt_ref[e, 0:c_lat, c0:c0 + kc] = lat_ref[e, c0:c0 + kc, :].T.astype(BF16)
            latt_ref[e, c_lat:, :] = jnp.ones((ONES_ROWS, seq), BF16)

    @pl.when((pl.program_id(0) == 0) & (qb == 0))
    def _():
        planes_ref[...] = jnp.zeros(planes_ref.shape, I32)

    lax.fori_loop(0, n_idx, idx_body, 0)

    q_t = [q_ref[e].T.astype(BF16) for e in ES]
    for h in range(n_heads):
        for e in ES:
            qa = _dot(wukt_ref[h], q_t[e][h * HEAD_DIM:(h + 1) * HEAD_DIM, :])
            wqk_ref[e, :, h * QB:(h + 1) * QB] = (qa * (scale * LOG2E)).astype(BF16)
    far_end = jnp.maximum(qb - 1, 0) * QB
    n_far = (far_end + kc - 1) // kc
    near_start = pl.multiple_of(far_end, QB)
    near_ld = pl.multiple_of(jnp.minimum(near_start, seq - kc), QB)
    near_off = pl.multiple_of(near_start - near_ld, QB)
    ks0 = pl.multiple_of(jnp.where(n_far > 0, 0, near_ld), QB)

    def lat_chunk(e, ks, width):
        return lat_ref[e, pl.ds(ks, width), :].astype(BF16)

    def store_logits(e, g, lg):
        lg_ref[e, 2 * g] = lg[:, 0:QB]
        lg_ref[e, 2 * g + 1] = lg[:, QB:]

    lat0 = [lat_chunk(e, ks0, kc) for e in ES]
    for g in range(n_pairs):
        for e in ES:
            store_logits(e, g, _dot(lat0[e], wqk_ref[e, :, pair_cols[g]]))

    n_rows = seq // KEY_BITS
    valid = lax.broadcasted_iota(I32, (n_rows, 1), 0) < n_idx * (kc // KEY_BITS)
    kf = float(topk)

    def colsum(w):
        return jnp.sum(lax.population_count(w).astype(F32), axis=0, keepdims=True)

    def bit_body(i, carry):
        alive, c_gt, thr_u = (list(t) for t in carry)
        b0 = KEY_BITS - 2 - 2 * i
        w0, w11, w10, w01 = [], [], [], []
        for e in ES:
            p1 = planes_ref[e, b0 + 1]
            p0 = planes_ref[e, b0]
            w1 = alive[e] & p1
            w0.append(alive[e] ^ w1)
            w11.append(w1 & p0)
            w10.append(w1 ^ w11[e])
            w01.append(w0[e] & p0)
        t11 = [c_gt[e] + colsum(w11[e]) for e in ES]
        t10 = [t11[e] + colsum(w10[e]) for e in ES]
        t01 = [t10[e] + colsum(w01[e]) for e in ES]
        for e in ES:
            k11, k10, k01 = t11[e] >= kf, t10[e] >= kf, t01[e] >= kf
            alive[e] = jnp.where(k11, w11[e], jnp.where(k10, w10[e], jnp.where(k01, w01[e], w0[e] ^ w01[e])))
            c_gt[e] = jnp.where(k11, c_gt[e], jnp.where(k10, t11[e], jnp.where(k01, t10[e], t01[e])))
            digit = jnp.where(k11, 3, jnp.where(k10, 2, jnp.where(k01, 1, 0)))
            thr_u[e] = thr_u[e] | lax.shift_left(digit, b0)
        return tuple(alive), tuple(c_gt), tuple(thr_u)

    alive0 = jnp.broadcast_to(jnp.where(valid, -1, 0), (n_rows, QB)).astype(I32)
    _, _, thr_u = lax.fori_loop(
        0, KEY_BITS // 2, bit_body,
        (tuple(alive0 for _ in ES), tuple(jnp.zeros((1, QB), F32) for _ in ES),
         tuple(jnp.zeros((1, QB), I32) for _ in ES)))
    few = (tpos + 1) <= topk

    def counts(e, preds):
        def body(c, accs):
            ks = pl.multiple_of(c * kc, kc)
            sc = score_ref[e, pl.ds(ks, kc), :]
            row = ks + lax.broadcasted_iota(I32, (kc, 1), 0)
            out = []
            for pred, acc in zip(preds, accs):
                hit = jnp.where(pred(sc, row), 1.0, 0.0).reshape(kc // SUBLANES, SUBLANES, QB)
                parts = [hit[i] for i in range(kc // SUBLANES)]
                while len(parts) > 1:
                    parts = [parts[i] + parts[i + 1] for i in range(0, len(parts), 2)]
                out.append(acc + parts[0])
            return tuple(out)
        accs = lax.fori_loop(0, n_idx, body, tuple(jnp.zeros((SUBLANES, QB), F32) for _ in preds))
        return [jnp.sum(a, axis=0, keepdims=True) for a in accs]

    def store_threshold(e, thr_f):
        thr_f = jnp.where(few, -jnp.inf, thr_f)
        gt, ge = counts(e, (lambda sc, row: sc > thr_f, lambda sc, row: sc >= thr_f))
        thr_ref[e] = thr_f
        cnt_ref[e, 0:1, :] = gt
        cnt_ref[e, 1:2, :] = ge

    for e in ES:
        store_threshold(e, _key_to_float(thr_u[e] ^ INT_MIN))

    bad = None
    for e in ES:
        ok = few | ((cnt_ref[e, 0:1, :] < kf) & (cnt_ref[e, 1:2, :] >= kf))
        b = jnp.where(ok, 0.0, 1.0)
        bad = b if bad is None else jnp.maximum(bad, b)

    @pl.when(jnp.max(bad) > 0.0)
    def _():
        for e in ES:
            def bisect_body(i, pre_u, e=e):
                cand_u = pre_u | lax.shift_left(jnp.int32(1), KEY_BITS - 1 - i)
                cand = _key_to_float(cand_u ^ INT_MIN)
                cnt, = counts(e, (lambda sc, row: sc >= cand,))
                return jnp.where(cnt >= kf, cand_u, pre_u)

            pre_u = lax.fori_loop(0, KEY_BITS, bisect_body, jnp.zeros((1, QB), I32))
            store_threshold(e, _key_to_float(pre_u ^ INT_MIN))

    thr = [thr_ref[e] for e in ES]
    need = [kf - cnt_ref[e, 0:1, :] for e in ES]
    excess = None
    for e in ES:
        ex = jnp.where((cnt_ref[e, 1:2, :] > kf) & jnp.logical_not(few), 1.0, 0.0)
        excess = ex if excess is None else jnp.maximum(excess, ex)
        mt_ref[e] = jnp.full((1, QB), INT_MAX, I32)

    @pl.when(jnp.max(excess) > 0.0)
    def _():
        nbits = max(1, (seq - 1).bit_length())
        for e in ES:
            def tie_body(i, mcur, e=e):
                cand = mcur | lax.shift_left(jnp.int32(1), nbits - 1 - i)
                cnt, = counts(e, (lambda sc, row: (sc == thr[e]) & (row < cand),))
                return jnp.where(cnt < need[e], cand, mcur)

            mt_ref[e] = lax.fori_loop(0, nbits, tie_body, jnp.zeros((1, QB), I32))

    mtie = [jnp.where(few, -1, mt_ref[e]) for e in ES]

    def selected(e, ks, width, hi_limit):
        sc = score_ref[e, pl.ds(ks, width), :]
        row = ks + lax.broadcasted_iota(I32, (width, 1), 0)
        sel = (sc > thr[e]) | ((sc == thr[e]) & (row <= mtie[e]))
        if hi_limit is not None:
            sel = sel & (row < hi_limit)
        return sel

    acc_ref[...] = jnp.zeros(acc_ref.shape, F32)
    p_ref[...] = jnp.zeros(p_ref.shape, BF16)

    def lat_t_ones(e, ks, width):
        return latt_ref[e, :, pl.ds(ks, width)]

    def pair_probs(e, g):
        return jnp.concatenate([p_ref[e, 2 * g], p_ref[e, 2 * g + 1]], axis=1)

    def far_body(c, ms_flat):
        ms = [list(ms_flat[e * n_heads:(e + 1) * n_heads]) for e in ES]
        ksp = pl.multiple_of(jnp.maximum(c - 1, 0) * kc, kc)
        ksn = pl.multiple_of(jnp.where(c + 1 < n_far, (c + 1) * kc, near_ld), QB)
        ks = pl.multiple_of(c * kc, kc)
        latp_t = [lat_t_ones(e, ksp, kc) for e in ES]
        latn = [lat_chunk(e, ksn, kc) for e in ES]
        for e in ES:
            madd_ref[e] = jnp.where(selected(e, ks, kc, far_end), 0.0, NEG_MASK)
        for g in range(n_pairs):
            pv_g = [_dot(latp_t[e], pair_probs(e, g)) for e in ES]
            lg_next = [_dot(latn[e], wqk_ref[e, :, pair_cols[g]]) for e in ES]
            for h in (2 * g, 2 * g + 1):
                bfar = far_ref[h]
                for e in ES:
                    lgm = lg_ref[e, h] + madd_ref[e]
                    m_new = jnp.maximum(ms[e][h], jnp.max(lgm, axis=0, keepdims=True) + bfar)
                    p = jnp.exp2(lgm - (m_new - bfar))
                    alpha = jnp.exp2(ms[e][h] - m_new)
                    ms[e][h] = m_new
                    p_ref[e, h] = p.astype(BF16)
                    acc_ref[e, h] = (acc_ref[e, h] + head(pv_g[e], h - 2 * g)) * alpha
            for e in ES:
                store_logits(e, g, lg_next[e])
        return tuple(m for e in ES for m in ms[e])

    ms_flat = lax.fori_loop(0, n_far, far_body,
                            tuple(jnp.full((1, QB), NEG_INIT, F32) for _ in range(E * n_heads)))
    ms = [ms_flat[e * n_heads:(e + 1) * n_heads] for e in ES]
    ks_last = pl.multiple_of(jnp.maximum(n_far - 1, 0) * kc, kc)
    pv = [_dot(lat_t_ones(e, ks_last, kc),
               jnp.concatenate([p_ref[e, h] for h in range(n_heads)], axis=1)) for e in ES]

    tile_id = jnp.where(qb == 0, 1, 0)
    sel = [selected(e, near_start, NEAR, None) for e in ES]
    ps = [[] for _ in ES]
    alphas = [[] for _ in ES]
    for h in range(n_heads):
        for e in ES:
            lgm = jnp.where(sel[e], lg_ref[e, h, pl.ds(near_off, NEAR), :] + bias_ref[tile_id, h], NEG_MASK)
            m_new = jnp.maximum(ms[e][h], jnp.max(lgm, axis=0, keepdims=True))
            alphas[e].append(jnp.exp2(ms[e][h] - m_new))
            ps[e].append(jnp.exp2(lgm - m_new).astype(BF16))
    pvn = [_dot(lat_t_ones(e, near_start, NEAR), jnp.concatenate(ps[e], axis=1)) for e in ES]
    o_stack = []
    for e in ES:
        o_parts = []
        for h in range(n_heads):
            tot = (acc_ref[e, h] + head(pv[e], h)) * alphas[e][h] + head(pvn[e], h)
            o_parts.append((tot[0:c_lat] / tot[c_lat:c_lat + 1]).astype(BF16))
        o_stack.append(jnp.concatenate(o_parts, axis=0))
    for e in ES:
        out_ref[e] = _dot(o_stack[e], wuv_ref[...], _TN).astype(out_ref.dtype)


def _dsa_call(q, qi, wi, ki2, lat, wuk_t, wuv_bd, bias_near, bias_far, *, topk, kc):
    bsz, s, aw = q.shape
    n_heads = aw // HEAD_DIM
    c_lat = lat.shape[-1]
    nb = s // Q_BLOCK
    hq = n_heads * Q_BLOCK
    e = 2 if bsz % 2 == 0 else 1
    blk = lambda w: pl.BlockSpec((e, Q_BLOCK, w), lambda b, j: (b, j, 0))
    full = lambda w: pl.BlockSpec((e, s, w), lambda b, j: (b, 0, 0))
    return pl.pallas_call(
        functools.partial(_dsa_kernel, topk=topk, kc=kc, n_heads=n_heads, seq=s,
                          scale=float(HEAD_DIM) ** -0.5),
        grid=(bsz // e, nb),
        in_specs=[
            blk(aw), blk(qi.shape[-1]), blk(LANES), full(LANES), full(c_lat),
            _const_spec(wuk_t.shape), _const_spec(wuv_bd.shape), _const_spec(bias_near.shape),
            _const_spec(bias_far.shape),
        ],
        out_specs=blk(aw),
        out_shape=jax.ShapeDtypeStruct((bsz, s, aw), BF16),
        scratch_shapes=[
            pltpu.VMEM((e, s, Q_BLOCK), F32),
            pltpu.VMEM((e, n_heads, c_lat + ONES_ROWS, Q_BLOCK), F32),
            pltpu.VMEM((e, n_heads, kc, Q_BLOCK), F32),
            pltpu.VMEM((e, n_heads, kc, Q_BLOCK), BF16),
            pltpu.VMEM((e, LANES, hq), BF16),
            pltpu.VMEM((e, c_lat, hq), BF16),
            pltpu.VMEM((e, kc, Q_BLOCK), F32),
            pltpu.VMEM((e, 1, Q_BLOCK), I32),
            pltpu.VMEM((e, KEY_BITS, s // KEY_BITS, Q_BLOCK), I32),
            pltpu.VMEM((e, c_lat + ONES_ROWS, s), BF16),
            pltpu.VMEM((e, 1, Q_BLOCK), F32),
            pltpu.VMEM((e, SUBLANES, Q_BLOCK), F32),
        ],
        compiler_params=_GRID_PARAMS,
        name="dsa_attention",
    )(q, qi, wi, ki2, lat, wuk_t, wuv_bd, bias_near, bias_far)


def _out_kernel(ya_ref, za_ref, yb_ref, zb_ref, ga_ref, gb_ref, x_ref, mod_ref,
                wpa_ref, wpb_ref, wo_ref, fg_ref, o_ref, *, d, final):
    za = za_ref[0].astype(F32)
    zb = zb_ref[0].astype(F32)
    ua = (ya_ref[0].astype(F32) * (za * _sigmoid(za))).astype(BF16)
    ub = (yb_ref[0].astype(F32) * (zb * _sigmoid(zb))).astype(BF16)
    br_a = _dot(ua, wpa_ref[...])
    br_b = _dot(ub, wpb_ref[...])
    merged = _sigmoid(ga_ref[0].astype(F32)) * br_a + _sigmoid(gb_ref[0].astype(F32)) * br_b
    gate = mod_ref[0, :, 2 * d:3 * d]
    xo = x_ref[0] + gate * _dot(merged.astype(BF16), wo_ref[...])
    if final:
        xo = xo * lax.rsqrt(jnp.mean(xo * xo, axis=-1, keepdims=True) + NORM_EPS) * fg_ref[...]
    o_ref[0] = xo


def _out_call(ya, za, yb, zb, ga, gb, x, mod, wpa, wpb, wo, final_g, *, final, tm):
    bsz, s, d = x.shape
    d3 = mod.shape[-1]
    row = lambda w: pl.BlockSpec((1, tm, w), lambda b, j: (b, j, 0))
    return pl.pallas_call(
        functools.partial(_out_kernel, d=d, final=final),
        grid=(bsz, s // tm),
        in_specs=[
            row(ya.shape[-1]), row(za.shape[-1]), row(yb.shape[-1]), row(zb.shape[-1]),
            row(d), row(d), row(d),
            pl.BlockSpec((1, 1, d3), lambda b, j: (b, 0, 0)),
            _const_spec(wpa.shape), _const_spec(wpb.shape), _const_spec(wo.shape), _const_spec((1, d)),
        ],
        out_specs=row(d),
        out_shape=jax.ShapeDtypeStruct((bsz, s, d), F32),
        compiler_params=_GRID_PARAMS,
        name="gated_out",
    )(ya, za, yb, zb, ga, gb, x, mod.reshape(bsz, 1, d3), wpa, wpb, wo, final_g.reshape(1, d))


def _t5_bucket(dist, num_buckets):
    max_exact = num_buckets // 2
    is_small = dist < max_exact
    dd = jnp.maximum(dist, 1).astype(F32)
    x = jnp.log(dd / max_exact) / math.log(MAX_DISTANCE / max_exact) * (num_buckets - max_exact)
    large = max_exact + jnp.where(x >= 0, jnp.floor(x), jnp.ceil(x)).astype(I32)
    large = jnp.minimum(large, num_buckets - 1)
    return jnp.where(is_small, dist, large)


def _bias_tiles(rel_bias):
    nbk = rel_bias.shape[0]
    i = jnp.arange(Q_BLOCK, dtype=I32)[:, None]
    jj = jnp.arange(2 * Q_BLOCK, dtype=I32)[None, :]
    tiles = []
    for base in (Q_BLOCK, 0):
        bucket = _t5_bucket(jnp.maximum(base + i - jj, 0), nbk)
        onehot = (bucket[None, :, :] == jnp.arange(nbk, dtype=I32)[:, None, None]).astype(F32)
        tiles.append(jnp.einsum('nh,nij->hji', rel_bias, onehot,
                                precision=lax.Precision.HIGHEST))
    far = rel_bias[_t5_bucket(jnp.full((1, 1), MAX_DISTANCE, I32), nbk)]
    return jnp.stack(tiles, axis=0) * LOG2E, jnp.moveaxis(far, -1, 0) * LOG2E


def _block_diag_heads(w, rows_per_head, cols_per_head):
    n = w.shape[0]
    eye = jnp.eye(n, dtype=w.dtype)
    return jnp.einsum('hrc,hg->hrgc', w, eye).reshape(n * rows_per_head, n * cols_per_head)


def _tiles(s):
    big = s % 512 == 0
    return dict(tm_in=512 if big else 2 * Q_BLOCK, tm_out=512 if big else 2 * Q_BLOCK,
                kc=512 if big else 2 * Q_BLOCK, tb=4 * CHUNK)


def kernel(x, c, ada_w, ada_b, norm_g, w_in, shift_mu, w0, w2, a0, a2, k_k, k_a, r_k, lnx_g, lnx_b,
           kv_norm_g, w_uk, w_uv, w_pa, w_pb, w_o, rel_bias, final_g):
    bsz, s, d = x.shape
    depth = w_in.shape[0]
    rw = w0.shape[-1]
    c_lat, n_heads, dh = w_uk.shape[1:]
    aw = n_heads * dh
    iw = IDX_HEADS * IDX_HEAD_DIM
    n_shift = 3 * rw + DECAY_LORA + ICLR_LORA
    topk = min(TOPK_MAX, s // 4)
    assert DECAY_LORA + ICLR_LORA == LANES and dh == HEAD_DIM and IDX_HEAD_DIM == HEAD_DIM
    assert s % (2 * Q_BLOCK) == 0 and rw % LANES == 0 and n_heads % 2 == 0
    tiles = _tiles(s)

    sizes = (rw, rw, rw, DECAY_LORA, ICLR_LORA, rw, aw, c_lat, aw, iw, IDX_HEAD_DIM, IDX_HEADS, d, d)
    offs = [0]
    for sz in sizes:
        offs.append(offs[-1] + sz)
    col = lambda i: slice(offs[i], offs[i + 1])
    widths = (rw, aw, c_lat, aw, iw, LANES, LANES, d, d)

    mod = _mod_call(c, ada_w, ada_b)
    bias_near, bias_far = _bias_tiles(rel_bias)
    ones_bd = _block_diag_heads(jnp.ones((rw // HEAD_DIM, HEAD_DIM, HEAD_DIM), BF16), HEAD_DIM, HEAD_DIM)

    for l in range(depth):
        w = w_in[l]
        wi_pad = jnp.zeros((d, LANES - IDX_HEADS), w.dtype)
        w_packed = jnp.concatenate(
            [w[:, 0:n_shift], w[:, col(5)], w[:, col(6)], w[:, col(7)], w[:, col(8)], w[:, col(9)],
             w[:, col(10)], w[:, col(10)], w[:, col(11)], wi_pad, w[:, col(12)], w[:, col(13)]],
            axis=1).astype(BF16)
        sh, za, q, lat, zb, qi, ki2, wi, ga, gb = _inproj_call(
            x, mod[l], norm_g[l], w_packed, shift_mu[l], kv_norm_g[l],
            n_shift=n_shift, widths=widths, tm=tiles["tm_in"])

        vecs = jnp.stack([w0[l], a0[l], k_k[l], k_a[l], r_k[l].reshape(-1), lnx_g[l], lnx_b[l],
                          jnp.zeros((rw,), F32)], axis=0)
        w2p = jnp.concatenate([w2[l], jnp.zeros((ICLR_LORA, rw), F32)], axis=0)
        a2p = jnp.concatenate([jnp.zeros((DECAY_LORA, rw), F32), a2[l]], axis=0)
        ya = _rwkv_call(sh, vecs, w2p, a2p, ones_bd, width=rw, tb=tiles["tb"])

        wuk_t = jnp.transpose(w_uk[l], (1, 0, 2)).astype(BF16)
        wuv_bd = _block_diag_heads(jnp.transpose(w_uv[l], (1, 0, 2)), c_lat, dh).astype(BF16)
        yb = _dsa_call(q, qi, wi, ki2, lat, wuk_t, wuv_bd, bias_near, bias_far,
                       topk=topk, kc=tiles["kc"])

        x = _out_call(ya, za, yb, zb, ga, gb, x, mod[l], w_pa[l].astype(BF16), w_pb[l].astype(BF16),
                      w_o[l].astype(BF16), final_g, final=(l == depth - 1), tm=tiles["tm_out"])
    return x
```

```python
import functools
import math

import jax
import jax.numpy as jnp
from jax import lax
from jax.experimental import pallas as pl
from jax.experimental.pallas import tpu as pltpu

F32 = jnp.float32
BF16 = jnp.bfloat16
I32 = jnp.int32

DECAY_LORA = 64
ICLR_LORA = 64
GN_EPS = 64e-5
IDX_HEADS = 8
IDX_HEAD_DIM = 64
TOPK_MAX = 256
Q_BLOCK = 128
MAX_DISTANCE = 128
NORM_EPS = 1e-6

LANES = 128
SUBLANES = 8
VMEM_LIMIT_BYTES = 56 * 1024 * 1024

HEAD_DIM = 64
CHUNK = 64
KEY_BITS = 32
INT_MIN = -(2 ** 31)
INT_MAX = 2 ** 31 - 1
NEG_MASK = -2e30
NEG_INIT = -1e30
LOG2E = 1.4426950408889634
BIT_GROUP = KEY_BITS * SUBLANES
ONES_ROWS = 16


def _sigmoid(x):
    return 1.0 / (1.0 + jnp.exp(-x))


_NN = (((1,), (0,)), ((), ()))
_NT = (((1,), (1,)), ((), ()))
_TN = (((0,), (0,)), ((), ()))


def _dot(a, b, dims=_NN):
    return lax.dot_general(a, b, dims, preferred_element_type=F32)


def _split(x, n):
    parts = []
    rem = x
    for i in range(n):
        p = rem.astype(BF16)
        parts.append(p)
        if i + 1 < n:
            rem = rem - p.astype(F32)
    return parts


def _mm(a, b, passes=1, dims=_NN):
    if passes == 1:
        return _dot(a.astype(BF16), b.astype(BF16), dims)
    if passes == 3:
        a1, a2 = _split(a, 2)
        b1, b2 = _split(b, 2)
        return _dot(a1, b1, dims) + (_dot(a1, b2, dims) + _dot(a2, b1, dims))
    a1, a2, a3 = _split(a, 3)
    b1, b2, b3 = _split(b, 3)
    hi = _dot(a1, b1, dims)
    mid = _dot(a1, b2, dims) + _dot(a2, b1, dims)
    lo = _dot(a2, b2, dims) + (_dot(a1, b3, dims) + _dot(a3, b1, dims))
    return hi + (mid + lo)


def _mm_exact_rhs(a, b_bf16, n):
    out = None
    for p in _split(a, n):
        t = _dot(p, b_bf16)
        out = t if out is None else out + t
    return out


def _const_spec(shape):
    return pl.BlockSpec(shape, lambda b, j: tuple(0 for _ in shape))


_GRID_PARAMS = pltpu.CompilerParams(dimension_semantics=("arbitrary", "arbitrary"),
                                    vmem_limit_bytes=VMEM_LIMIT_BYTES)


def _mod_kernel(c_ref, w_ref, b_ref, o_ref):
    c = c_ref[...]
    ca = c * _sigmoid(c)
    o_ref[0] = _mm(ca, w_ref[0], 6) + b_ref[0]


def _mod_call(c, ada_w, ada_b):
    depth, d, d3 = ada_w.shape
    bsz = c.shape[0]
    tn = d
    return pl.pallas_call(
        _mod_kernel,
        grid=(depth, d3 // tn),
        in_specs=[
            pl.BlockSpec((bsz, d), lambda l, n: (0, 0)),
            pl.BlockSpec((1, d, tn), lambda l, n: (l, 0, n)),
            pl.BlockSpec((1, 1, tn), lambda l, n: (l, 0, n)),
        ],
        out_specs=pl.BlockSpec((1, bsz, tn), lambda l, n: (l, 0, n)),
        out_shape=jax.ShapeDtypeStruct((depth, bsz, d3), F32),
        compiler_params=_GRID_PARAMS,
        name="adaln_mod",
    )(c, ada_w, ada_b.reshape(depth, 1, d3))


def _inproj_kernel(x_ref, mod_ref, g_ref, w_ref, mu_ref, kvg_ref,
                   sh_ref, za_ref, q_ref, lat_ref, zb_ref, qi_ref, ki_ref, wi_ref, ga_ref, gb_ref,
                   carry_ref, *, d, n_shift, widths):
    j = pl.program_id(1)
    xb = x_ref[0]
    tm = xb.shape[0]
    shift = mod_ref[0, :, 0:d]
    scale = mod_ref[0, :, d:2 * d]
    ms = jnp.mean(xb * xb, axis=-1, keepdims=True)
    h = xb * lax.rsqrt(ms + NORM_EPS) * g_ref[...]
    h = h * (1.0 + scale) + shift
    hb = h.astype(BF16)

    @pl.when(j == 0)
    def _():
        carry_ref[...] = jnp.zeros_like(carry_ref)

    ps = _dot(hb, w_ref[:, 0:n_shift])
    prev = pltpu.roll(ps, 1, 0)
    row = lax.broadcasted_iota(I32, (tm, 1), 0)
    prev = jnp.where(row == 0, carry_ref[...], prev)
    carry_ref[...] = ps[tm - 1:tm, :]
    sh_ref[0] = ps + (prev - ps) * mu_ref[...]

    off = n_shift
    outs = (za_ref, q_ref, lat_ref, zb_ref, qi_ref, ki_ref, wi_ref, ga_ref, gb_ref)
    for o_ref, wd in zip(outs, widths):
        p = _dot(hb, w_ref[:, off:off + wd])
        if o_ref is lat_ref:
            p = p * lax.rsqrt(jnp.mean(p * p, axis=-1, keepdims=True) + NORM_EPS) * kvg_ref[...]
        o_ref[0] = p.astype(o_ref.dtype)
        off += wd


def _inproj_call(x, mod, norm_g, w_packed, mu, kvg, *, n_shift, widths, tm):
    bsz, s, d = x.shape
    npk = w_packed.shape[1]
    d3 = mod.shape[-1]
    row_spec = lambda w: pl.BlockSpec((1, tm, w), lambda b, j: (b, j, 0))
    out_widths = (n_shift,) + tuple(widths)
    out_dtypes = (F32, BF16, BF16, BF16, BF16, BF16, BF16, F32, BF16, BF16)
    return pl.pallas_call(
        functools.partial(_inproj_kernel, d=d, n_shift=n_shift, widths=tuple(widths)),
        grid=(bsz, s // tm),
        in_specs=[
            row_spec(d),
            pl.BlockSpec((1, 1, d3), lambda b, j: (b, 0, 0)),
            _const_spec((1, d)),
            pl.BlockSpec((d, npk), lambda b, j: (0, 0), pipeline_mode=pl.Buffered(1)),
            _const_spec((1, n_shift)),
            _const_spec((1, widths[2])),
        ],
        out_specs=[row_spec(w) for w in out_widths],
        out_shape=[jax.ShapeDtypeStruct((bsz, s, w), dt) for w, dt in zip(out_widths, out_dtypes)],
        scratch_shapes=[pltpu.VMEM((1, n_shift), F32)],
        compiler_params=_GRID_PARAMS,
        name="inproj",
    )(x, mod.reshape(bsz, 1, d3), norm_g.reshape(1, d), w_packed, mu.reshape(1, n_shift),
      kvg.reshape(1, -1))


def _rwkv_kernel(sh_ref, vec_ref, w2_ref, a2_ref, ones_ref, out_ref, st_ref, *, width):
    j = pl.program_id(1)
    L = CHUNK
    n_pairs = width // LANES

    @pl.when(j == 0)
    def _():
        st_ref[...] = jnp.zeros_like(st_ref)

    blk = sh_ref[0]
    pr = blk[:, 0:width]
    pk = blk[:, width:2 * width]
    pv = blk[:, 2 * width:3 * width]
    lora = blk[:, 3 * width:3 * width + LANES]
    lane = lax.broadcasted_iota(I32, (1, LANES), 1)
    lo_half = lane < HEAD_DIM
    t_in = jnp.where(lo_half, jnp.tanh(lora), lora)
    dec_in = _mm(t_in, w2_ref[...])
    a_in = _mm(t_in, a2_ref[...])

    w0 = vec_ref[0:1, :]
    a0 = vec_ref[1:2, :]
    k_k = vec_ref[2:3, :]
    k_a = vec_ref[3:4, :]
    r_k = vec_ref[4:5, :]
    lnx_g = vec_ref[5:6, :]
    lnx_b = vec_ref[6:7, :]
    ones_bd = ones_ref[...]

    z = -(w0 + dec_in)
    sp = jnp.maximum(z, 0.0) + jnp.log(1.0 + jnp.exp(-jnp.abs(z)))
    logw = -jnp.exp(-sp - 0.5)
    a = _sigmoid(a0 + a_in)
    kk0 = pk * k_k
    ss = _mm_exact_rhs(kk0 * kk0, ones_bd, 1)
    kk = kk0 / jnp.maximum(jnp.sqrt(ss), 1e-12)
    k_eff = pk * (1.0 + (a - 1.0) * k_a)
    a_vec = -kk
    b_vec = kk * a

    tb = blk.shape[0]
    n_chunks = tb // L
    r_i = lax.broadcasted_iota(I32, (tb, tb), 0)
    c_i = lax.broadcasted_iota(I32, (tb, tb), 1)
    tril = jnp.where(((r_i // L) == (c_i // L)) & (c_i <= r_i), 1.0, 0.0).astype(BF16)
    cl = None
    for part in _split(logw, 2):
        t = _dot(tril, part)
        cl = t if cl is None else cl + t
    cl_last = jnp.concatenate(
        [jnp.broadcast_to(cl[c * L + L - 1:c * L + L, :], (L, width)) for c in range(n_chunks)], axis=0)
    rt = pr * jnp.exp(cl)
    at = a_vec * jnp.exp(cl - logw)
    e_inv = jnp.exp(-cl)
    bt = b_vec * e_inv
    kt = k_eff * e_inv
    e_l = jnp.exp(cl_last - cl)
    bl = b_vec * e_l
    kl = k_eff * e_l
    p_last = jnp.exp(cl_last)

    rr = lax.broadcasted_iota(I32, (LANES, LANES), 0)
    cc = lax.broadcasted_iota(I32, (LANES, LANES), 1)
    same = (rr // L) == (cc // L)
    strict = same & ((cc % L) < (rr % L))
    eye = rr == cc
    eye_f = jnp.where(eye, 1.0, 0.0)
    rr2 = lax.broadcasted_iota(I32, (LANES, 2 * LANES), 0)
    cc2 = lax.broadcasted_iota(I32, (LANES, 2 * LANES), 1)
    incl2 = ((rr2 // L) == ((cc2 % LANES) // L)) & ((cc2 % L) <= (rr2 % L))

    def stack(x):
        return jnp.concatenate([jnp.where(lo_half, x, 0.0), jnp.where(lo_half, 0.0, x)], axis=0)

    chains = [(c, p) for c in range(n_chunks) for p in range(n_pairs)]

    def piece(x, c, p):
        return x[c * L:(c + 1) * L, p * LANES:(p + 1) * LANES]

    atm, rtm, vm, blkl, a_ab, a_ak, a_rbk = {}, {}, {}, {}, {}, {}, {}
    for ch in chains:
        atm[ch], rtm[ch] = stack(piece(at, *ch)), stack(piece(rt, *ch))
        vm[ch] = stack(piece(pv, *ch)).astype(BF16)
        btm, ktm = stack(piece(bt, *ch)), stack(piece(kt, *ch))
        blkl[ch] = jnp.concatenate([stack(piece(bl, *ch)), stack(piece(kl, *ch))], axis=0).astype(BF16)
        g = _mm(jnp.concatenate([atm[ch], rtm[ch]], axis=0), jnp.concatenate([btm, ktm], axis=0),
                dims=_NT)
        a_ab[ch] = jnp.where(strict, g[0:LANES, 0:LANES], 0.0)
        a_ak[ch] = jnp.where(strict, g[0:LANES, LANES:], 0.0)
        a_rbk[ch] = jnp.where(incl2, g[LANES:, :], 0.0).astype(BF16)
    tinv = {ch: eye_f + a_ab[ch] for ch in chains}
    qn = {ch: _mm(a_ab[ch], a_ab[ch]) for ch in chains}
    sq = 2
    while 2 * sq < L:
        r = {ch: _mm(qn[ch], jnp.concatenate([qn[ch], tinv[ch]], axis=1)) for ch in chains}
        for ch in chains:
            qn[ch] = r[ch][:, 0:LANES]
            tinv[ch] = tinv[ch] + r[ch][:, LANES:]
        sq *= 2
    tinv = {ch: tinv[ch] + _mm(qn[ch], tinv[ch]) for ch in chains}
    akv = {ch: _mm(a_ak[ch], vm[ch]) for ch in chains}
    wz = {ch: _mm(tinv[ch], jnp.concatenate([atm[ch], akv[ch]], axis=1)) for ch in chains}
    wm = {ch: wz[ch][:, 0:LANES].astype(BF16) for ch in chains}
    zv = {ch: jnp.concatenate([wz[ch][:, LANES:].astype(BF16), vm[ch]], axis=0) for ch in chains}
    rqm = {ch: rtm[ch] + _mm(a_rbk[ch][:, 0:LANES], wm[ch]) for ch in chains}
    y0m = {ch: _mm(a_rbk[ch], zv[ch]) for ch in chains}
    mt = {ch: jnp.where(eye, piece(p_last, *ch)[0:1], 0.0)
          + _mm(blkl[ch][0:LANES], wm[ch], dims=_TN) for ch in chains}
    ct = {ch: _mm(blkl[ch], zv[ch], dims=_TN) for ch in chains}

    st = {p: st_ref[p] for p in range(n_pairs)}
    y_rows = []
    for c in range(n_chunks):
        ys = []
        for p in range(n_pairs):
            ym = _mm(rqm[(c, p)], st[p]) + y0m[(c, p)]
            ys.append(ym[0:L] + ym[L:])
            st[p] = _mm(mt[(c, p)], st[p]) + ct[(c, p)]
        y_rows.append(jnp.concatenate(ys, axis=1))
    for p in range(n_pairs):
        st_ref[p] = st[p]
    y = jnp.concatenate(y_rows, axis=0)

    inv_n = 1.0 / HEAD_DIM
    mu = _mm_exact_rhs(y, ones_bd, 1) * inv_n
    dlt = y - mu
    var = _mm_exact_rhs(dlt * dlt, ones_bd, 1) * inv_n
    yn = dlt * lax.rsqrt(var + GN_EPS) * lnx_g + lnx_b
    bonus = _mm_exact_rhs(pr * k_eff * r_k, ones_bd, 1) * pv
    out_ref[0] = (yn + bonus).astype(out_ref.dtype)


def _rwkv_call(sh, vecs, w2p, a2p, ones_bd, *, width, tb):
    bsz, s, n_shift = sh.shape
    n_pairs = width // LANES
    return pl.pallas_call(
        functools.partial(_rwkv_kernel, width=width),
        grid=(bsz, s // tb),
        in_specs=[
            pl.BlockSpec((1, tb, n_shift), lambda b, j: (b, j, 0)),
            _const_spec(vecs.shape),
            _const_spec(w2p.shape),
            _const_spec(a2p.shape),
            _const_spec(ones_bd.shape),
        ],
        out_specs=pl.BlockSpec((1, tb, width), lambda b, j: (b, j, 0)),
        out_shape=jax.ShapeDtypeStruct((bsz, s, width), BF16),
        scratch_shapes=[pltpu.VMEM((n_pairs, LANES, LANES), F32)],
        compiler_params=_GRID_PARAMS,
        name="rwkv7_scan",
    )(sh, vecs, w2p, a2p, ones_bd)


def _float_to_key(x):
    bits = pltpu.bitcast(x, I32)
    return bits ^ ((bits >> 31) & INT_MAX)


def _key_to_float(k):
    return pltpu.bitcast(k ^ ((k >> 31) & INT_MAX), F32)


def _bit_transpose32(a):
    a = list(a)
    m, j = 0x0000FFFF, 16
    while j:
        k = 0
        while k < KEY_BITS:
            t = (a[k] ^ lax.shift_right_logical(a[k + j], jnp.int32(j))) & m
            a[k] = a[k] ^ t
            a[k + j] = a[k + j] ^ lax.shift_left(t, jnp.int32(j))
            k = (k + j + 1) & ~j
        j >>= 1
        m = (m ^ (m << j)) & 0xFFFFFFFF
    return a


def _dsa_kernel(q_ref, qi_ref, wi_ref, ki_ref, lat_ref, wukt_ref, wuv_ref, bias_ref, far_ref, out_ref,
                score_ref, acc_ref, lg_ref, p_ref, qiw_ref, wqk_ref, madd_ref, mt_ref, planes_ref, latt_ref,
                thr_ref, cnt_ref,
                *, topk, kc, n_heads, seq, scale):
    qb = pl.program_id(1)
    QB = Q_BLOCK
    NEAR = 2 * QB
    E = q_ref.shape[0]
    ES = range(E)
    n_pairs = n_heads // 2
    c_lat = lat_ref.shape[-1]
    q_end = (qb + 1) * QB
    n_idx = (jnp.maximum(q_end, NEAR) + kc - 1) // kc
    tpos = qb * QB + lax.broadcasted_iota(I32, (1, QB), 1)
    pair_cols = [slice(2 * g * QB, (2 * g + 2) * QB) for g in range(n_pairs)]

    def head(x, h):
        return x[:, h * QB:(h + 1) * QB]

    wi_t = []
    for e in ES:
        qi_t = qi_ref[e].T
        qi_w = jnp.concatenate([qi_t[h * HEAD_DIM:(h + 1) * HEAD_DIM, :] for h in range(n_heads)], axis=1)
        qiw_ref[e] = jnp.concatenate([qi_w, jnp.zeros_like(qi_w)], axis=0).astype(BF16)
        wi_t.append(wi_ref[e].T)

    def idx_body(c, carry):
        ks = pl.multiple_of(c * kc, kc)
        kib = [ki_ref[e, pl.ds(ks, kc), :].astype(BF16) for e in ES]
        score = [None] * E
        for g in range(n_pairs):
            for e in ES:
                s_g = jnp.maximum(_dot(kib[e], qiw_ref[e, :, pair_cols[g]]), 0.0)
                part = (wi_t[e][2 * g:2 * g + 1, :] * s_g[:, 0:QB]
                        + wi_t[e][2 * g + 1:2 * g + 2, :] * s_g[:, QB:])
                score[e] = part if score[e] is None else score[e] + part
        spos = ks + lax.broadcasted_iota(I32, (kc, 1), 0)
        for e in ES:
            sc = jnp.where(spos <= tpos, score[e] + 0.0, -jnp.inf)
            score_ref[e, pl.ds(ks, kc), :] = sc
            ukey = _float_to_key(sc) ^ INT_MIN
            for g in range(kc // BIT_GROUP):
                regs = _bit_transpose32(
                    [ukey[g * BIT_GROUP + SUBLANES * i:g * BIT_GROUP + SUBLANES * (i + 1), :]
                     for i in range(KEY_BITS)])
                row0 = pl.multiple_of(c * (kc // KEY_BITS) + SUBLANES * g, SUBLANES)
                for b in range(KEY_BITS):
                    planes_ref[e, b, pl.ds(row0, SUBLANES), :] = regs[KEY_BITS - 1 - b]
        return carry

    @pl.when(qb == 0)
    def _():
        for e in ES:
            for c0 in range(0, seq, kc):
                latt_ref[e, 0:c_lat, c0:c0 + kc] = lat_ref[e, c0:c0 + kc, :].T.astype(BF16)
            latt_ref[e, c_lat:, :] = jnp.ones((ONES_ROWS, seq), BF16)

    @pl.when((pl.program_id(0) == 0) & (qb == 0))
    def _():
        planes_ref[...] = jnp.zeros(planes_ref.shape, I32)

    lax.fori_loop(0, n_idx, idx_body, 0)

    q_t = [q_ref[e].T.astype(BF16) for e in ES]
    for h in range(n_heads):
        for e in ES:
            qa = _dot(wukt_ref[h], q_t[e][h * HEAD_DIM:(h + 1) * HEAD_DIM, :])
            wqk_ref[e, :, h * QB:(h + 1) * QB] = (qa * (scale * LOG2E)).astype(BF16)
    far_end = jnp.maximum(qb - 1, 0) * QB
    n_far = (far_end + kc - 1) // kc
    near_start = pl.multiple_of(far_end, QB)
    near_ld = pl.multiple_of(jnp.minimum(near_start, seq - kc), QB)
    near_off = pl.multiple_of(near_start - near_ld, QB)
    ks0 = pl.multiple_of(jnp.where(n_far > 0, 0, near_ld), QB)

    def lat_chunk(e, ks, width):
        return lat_ref[e, pl.ds(ks, width), :].astype(BF16)

    def store_logits(e, g, lg):
        lg_ref[e, 2 * g] = lg[:, 0:QB]
        lg_ref[e, 2 * g + 1] = lg[:, QB:]

    lat0 = [lat_chunk(e, ks0, kc) for e in ES]
    for g in range(n_pairs):
        for e in ES:
            store_logits(e, g, _dot(lat0[e], wqk_ref[e, :, pair_cols[g]]))

    n_rows = seq // KEY_BITS
    valid = lax.broadcasted_iota(I32, (n_rows, 1), 0) < n_idx * (kc // KEY_BITS)
    kf = float(topk)

    def colsum(w):
        return jnp.sum(lax.population_count(w).astype(F32), axis=0, keepdims=True)

    def bit_body(i, carry):
        alive, c_gt, thr_u = (list(t) for t in carry)
        b0 = KEY_BITS - 2 - 2 * i
        w0, w11, w10, w01 = [], [], [], []
        for e in ES:
            p1 = planes_ref[e, b0 + 1]
            p0 = planes_ref[e, b0]
            w1 = alive[e] & p1
            w0.append(alive[e] ^ w1)
            w11.append(w1 & p0)
            w10.append(w1 ^ w11[e])
            w01.append(w0[e] & p0)
        t11 = [c_gt[e] + colsum(w11[e]) for e in ES]
        t10 = [t11[e] + colsum(w10[e]) for e in ES]
        t01 = [t10[e] + colsum(w01[e]) for e in ES]
        for e in ES:
            k11, k10, k01 = t11[e] >= kf, t10[e] >= kf, t01[e] >= kf
            alive[e] = jnp.where(k11, w11[e], jnp.where(k10, w10[e], jnp.where(k01, w01[e], w0[e] ^ w01[e])))
            c_gt[e] = jnp.where(k11, c_gt[e], jnp.where(k10, t11[e], jnp.where(k01, t10[e], t01[e])))
            digit = jnp.where(k11, 3, jnp.where(k10, 2, jnp.where(k01, 1, 0)))
            thr_u[e] = thr_u[e] | lax.shift_left(digit, b0)
        return tuple(alive), tuple(c_gt), tuple(thr_u)

    alive0 = jnp.broadcast_to(jnp.where(valid, -1, 0), (n_rows, QB)).astype(I32)
    _, _, thr_u = lax.fori_loop(
        0, KEY_BITS // 2, bit_body,
        (tuple(alive0 for _ in ES), tuple(jnp.zeros((1, QB), F32) for _ in ES),
         tuple(jnp.zeros((1, QB), I32) for _ in ES)))
    few = (tpos + 1) <= topk

    def counts(e, preds):
        def body(c, accs):
            ks = pl.multiple_of(c * kc, kc)
            sc = score_ref[e, pl.ds(ks, kc), :]
            row = ks + lax.broadcasted_iota(I32, (kc, 1), 0)
            out = []
            for pred, acc in zip(preds, accs):
                hit = jnp.where(pred(sc, row), 1.0, 0.0).reshape(kc // SUBLANES, SUBLANES, QB)
                parts = [hit[i] for i in range(kc // SUBLANES)]
                while len(parts) > 1:
                    parts = [parts[i] + parts[i + 1] for i in range(0, len(parts), 2)]
                out.append(acc + parts[0])
            return tuple(out)
        accs = lax.fori_loop(0, n_idx, body, tuple(jnp.zeros((SUBLANES, QB), F32) for _ in preds))
        return [jnp.sum(a, axis=0, keepdims=True) for a in accs]

    def store_threshold(e, thr_f):
        thr_f = jnp.where(few, -jnp.inf, thr_f)
        gt, ge = counts(e, (lambda sc, row: sc > thr_f, lambda sc, row: sc >= thr_f))
        thr_ref[e] = thr_f
        cnt_ref[e, 0:1, :] = gt
        cnt_ref[e, 1:2, :] = ge

    for e in ES:
        store_threshold(e, _key_to_float(thr_u[e] ^ INT_MIN))

    bad = None
    for e in ES:
        ok = few | ((cnt_ref[e, 0:1, :] < kf) & (cnt_ref[e, 1:2, :] >= kf))
        b = jnp.where(ok, 0.0, 1.0)
        bad = b if bad is None else jnp.maximum(bad, b)

    @pl.when(jnp.max(bad) > 0.0)
    def _():
        for e in ES:
            def bisect_body(i, pre_u, e=e):
                cand_u = pre_u | lax.shift_left(jnp.int32(1), KEY_BITS - 1 - i)
                cand = _key_to_float(cand_u ^ INT_MIN)
                cnt, = counts(e, (lambda sc, row: sc >= cand,))
                return jnp.where(cnt >= kf, cand_u, pre_u)

            pre_u = lax.fori_loop(0, KEY_BITS, bisect_body, jnp.zeros((1, QB), I32))
            store_threshold(e, _key_to_float(pre_u ^ INT_MIN))

    thr = [thr_ref[e] for e in ES]
    need = [kf - cnt_ref[e, 0:1, :] for e in ES]
    excess = None
    for e in ES:
        ex = jnp.where((cnt_ref[e, 1:2, :] > kf) & jnp.logical_not(few), 1.0, 0.0)
        excess = ex if excess is None else jnp.maximum(excess, ex)
        mt_ref[e] = jnp.full((1, QB), INT_MAX, I32)

    @pl.when(jnp.max(excess) > 0.0)
    def _():
        nbits = max(1, (seq - 1).bit_length())
        for e in ES:
            def tie_body(i, mcur, e=e):
                cand = mcur | lax.shift_left(jnp.int32(1), nbits - 1 - i)
                cnt, = counts(e, (lambda sc, row: (sc == thr[e]) & (row < cand),))
                return jnp.where(cnt < need[e], cand, mcur)

            mt_ref[e] = lax.fori_loop(0, nbits, tie_body, jnp.zeros((1, QB), I32))

    mtie = [jnp.where(few, -1, mt_ref[e]) for e in ES]

    def selected(e, ks, width, hi_limit):
        sc = score_ref[e, pl.ds(ks, width), :]
        row = ks + lax.broadcasted_iota(I32, (width, 1), 0)
        sel = (sc > thr[e]) | ((sc == thr[e]) & (row <= mtie[e]))
        if hi_limit is not None:
            sel = sel & (row < hi_limit)
        return sel

    acc_ref[...] = jnp.zeros(acc_ref.shape, F32)
    p_ref[...] = jnp.zeros(p_ref.shape, BF16)

    def lat_t_ones(e, ks, width):
        return latt_ref[e, :, pl.ds(ks, width)]

    def pair_probs(e, g):
        return jnp.concatenate([p_ref[e, 2 * g], p_ref[e, 2 * g + 1]], axis=1)

    def far_body(c, ms_flat):
        ms = [list(ms_flat[e * n_heads:(e + 1) * n_heads]) for e in ES]
        ksp = pl.multiple_of(jnp.maximum(c - 1, 0) * kc, kc)
        ksn = pl.multiple_of(jnp.where(c + 1 < n_far, (c + 1) * kc, near_ld), QB)
        ks = pl.multiple_of(c * kc, kc)
        latp_t = [lat_t_ones(e, ksp, kc) for e in ES]
        latn = [lat_chunk(e, ksn, kc) for e in ES]
        for e in ES:
            madd_ref[e] = jnp.where(selected(e, ks, kc, far_end), 0.0, NEG_MASK)
        for g in range(n_pairs):
            pv_g = [_dot(latp_t[e], pair_probs(e, g)) for e in ES]
            lg_next = [_dot(latn[e], wqk_ref[e, :, pair_cols[g]]) for e in ES]
            for h in (2 * g, 2 * g + 1):
                bfar = far_ref[h]
                for e in ES:
                    lgm = lg_ref[e, h] + madd_ref[e]
                    m_new = jnp.maximum(ms[e][h], jnp.max(lgm, axis=0, keepdims=True) + bfar)
                    p = jnp.exp2(lgm - (m_new - bfar))
                    alpha = jnp.exp2(ms[e][h] - m_new)
                    ms[e][h] = m_new
                    p_ref[e, h] = p.astype(BF16)
                    acc_ref[e, h] = (acc_ref[e, h] + head(pv_g[e], h - 2 * g)) * alpha
            for e in ES:
                store_logits(e, g, lg_next[e])
        return tuple(m for e in ES for m in ms[e])

    ms_flat = lax.fori_loop(0, n_far, far_body,
                            tuple(jnp.full((1, QB), NEG_INIT, F32) for _ in range(E * n_heads)))
    ms = [ms_flat[e * n_heads:(e + 1) * n_heads] for e in ES]
    ks_last = pl.multiple_of(jnp.maximum(n_far - 1, 0) * kc, kc)
    pv = [_dot(lat_t_ones(e, ks_last, kc),
               jnp.concatenate([p_ref[e, h] for h in range(n_heads)], axis=1)) for e in ES]

    tile_id = jnp.where(qb == 0, 1, 0)
    sel = [selected(e, near_start, NEAR, None) for e in ES]
    ps = [[] for _ in ES]
    alphas = [[] for _ in ES]
    for h in range(n_heads):
        for e in ES:
            lgm = jnp.where(sel[e], lg_ref[e, h, pl.ds(near_off, NEAR), :] + bias_ref[tile_id, h], NEG_MASK)
            m_new = jnp.maximum(ms[e][h], jnp.max(lgm, axis=0, keepdims=True))
            alphas[e].append(jnp.exp2(ms[e][h] - m_new))
            ps[e].append(jnp.exp2(lgm - m_new).astype(BF16))
    pvn = [_dot(lat_t_ones(e, near_start, NEAR), jnp.concatenate(ps[e], axis=1)) for e in ES]
    o_stack = []
    for e in ES:
        o_parts = []
        for h in range(n_heads):
            tot = (acc_ref[e, h] + head(pv[e], h)) * alphas[e][h] + head(pvn[e], h)
            o_parts.append((tot[0:c_lat] / tot[c_lat:c_lat + 1]).astype(BF16))
        o_stack.append(jnp.concatenate(o_parts, axis=0))
    for e in ES:
        out_ref[e] = _dot(o_stack[e], wuv_ref[...], _TN).astype(out_ref.dtype)


def _dsa_call(q, qi, wi, ki2, lat, wuk_t, wuv_bd, bias_near, bias_far, *, topk, kc):
    bsz, s, aw = q.shape
    n_heads = aw // HEAD_DIM
    c_lat = lat.shape[-1]
    nb = s // Q_BLOCK
    hq = n_heads * Q_BLOCK
    e = 2 if bsz % 2 == 0 else 1
    blk = lambda w: pl.BlockSpec((e, Q_BLOCK, w), lambda b, j: (b, j, 0))
    full = lambda w: pl.BlockSpec((e, s, w), lambda b, j: (b, 0, 0))
    return pl.pallas_call(
        functools.partial(_dsa_kernel, topk=topk, kc=kc, n_heads=n_heads, seq=s,
                          scale=float(HEAD_DIM) ** -0.5),
        grid=(bsz // e, nb),
        in_specs=[
            blk(aw), blk(qi.shape[-1]), blk(LANES), full(LANES), full(c_lat),
            _const_spec(wuk_t.shape), _const_spec(wuv_bd.shape), _const_spec(bias_near.shape),
            _const_spec(bias_far.shape),
        ],
        out_specs=blk(aw),
        out_shape=jax.ShapeDtypeStruct((bsz, s, aw), BF16),
        scratch_shapes=[
            pltpu.VMEM((e, s, Q_BLOCK), F32),
            pltpu.VMEM((e, n_heads, c_lat + ONES_ROWS, Q_BLOCK), F32),
            pltpu.VMEM((e, n_heads, kc, Q_BLOCK), F32),
            pltpu.VMEM((e, n_heads, kc, Q_BLOCK), BF16),
            pltpu.VMEM((e, LANES, hq), BF16),
            pltpu.VMEM((e, c_lat, hq), BF16),
            pltpu.VMEM((e, kc, Q_BLOCK), F32),
            pltpu.VMEM((e, 1, Q_BLOCK), I32),
            pltpu.VMEM((e, KEY_BITS, s // KEY_BITS, Q_BLOCK), I32),
            pltpu.VMEM((e, c_lat + ONES_ROWS, s), BF16),
            pltpu.VMEM((e, 1, Q_BLOCK), F32),
            pltpu.VMEM((e, SUBLANES, Q_BLOCK), F32),
        ],
        compiler_params=_GRID_PARAMS,
        name="dsa_attention",
    )(q, qi, wi, ki2, lat, wuk_t, wuv_bd, bias_near, bias_far)


def _out_kernel(ya_ref, za_ref, yb_ref, zb_ref, ga_ref, gb_ref, x_ref, mod_ref,
                wpa_ref, wpb_ref, wo_ref, fg_ref, o_ref, *, d, final):
    za = za_ref[0].astype(F32)
    zb = zb_ref[0].astype(F32)
    ua = (ya_ref[0].astype(F32) * (za * _sigmoid(za))).astype(BF16)
    ub = (yb_ref[0].astype(F32) * (zb * _sigmoid(zb))).astype(BF16)
    br_a = _dot(ua, wpa_ref[...])
    br_b = _dot(ub, wpb_ref[...])
    merged = _sigmoid(ga_ref[0].astype(F32)) * br_a + _sigmoid(gb_ref[0].astype(F32)) * br_b
    gate = mod_ref[0, :, 2 * d:3 * d]
    xo = x_ref[0] + gate * _dot(merged.astype(BF16), wo_ref[...])
    if final:
        xo = xo * lax.rsqrt(jnp.mean(xo * xo, axis=-1, keepdims=True) + NORM_EPS) * fg_ref[...]
    o_ref[0] = xo


def _out_call(ya, za, yb, zb, ga, gb, x, mod, wpa, wpb, wo, final_g, *, final, tm):
    bsz, s, d = x.shape
    d3 = mod.shape[-1]
    row = lambda w: pl.BlockSpec((1, tm, w), lambda b, j: (b, j, 0))
    return pl.pallas_call(
        functools.partial(_out_kernel, d=d, final=final),
        grid=(bsz, s // tm),
        in_specs=[
            row(ya.shape[-1]), row(za.shape[-1]), row(yb.shape[-1]), row(zb.shape[-1]),
            row(d), row(d), row(d),
            pl.BlockSpec((1, 1, d3), lambda b, j: (b, 0, 0)),
            _const_spec(wpa.shape), _const_spec(wpb.shape), _const_spec(wo.shape), _const_spec((1, d)),
        ],
        out_specs=row(d),
        out_shape=jax.ShapeDtypeStruct((bsz, s, d), F32),
        compiler_params=_GRID_PARAMS,
        name="gated_out",
    )(ya, za, yb, zb, ga, gb, x, mod.reshape(bsz, 1, d3), wpa, wpb, wo, final_g.reshape(1, d))


def _t5_bucket(dist, num_buckets):
    max_exact = num_buckets // 2
    is_small = dist < max_exact
    dd = jnp.maximum(dist, 1).astype(F32)
    x = jnp.log(dd / max_exact) / math.log(MAX_DISTANCE / max_exact) * (num_buckets - max_exact)
    large = max_exact + jnp.where(x >= 0, jnp.floor(x), jnp.ceil(x)).astype(I32)
    large = jnp.minimum(large, num_buckets - 1)
    return jnp.where(is_small, dist, large)


def _bias_tiles(rel_bias):
    nbk = rel_bias.shape[0]
    i = jnp.arange(Q_BLOCK, dtype=I32)[:, None]
    jj = jnp.arange(2 * Q_BLOCK, dtype=I32)[None, :]
    tiles = []
    for base in (Q_BLOCK, 0):
        bucket = _t5_bucket(jnp.maximum(base + i - jj, 0), nbk)
        onehot = (bucket[None, :, :] == jnp.arange(nbk, dtype=I32)[:, None, None]).astype(F32)
        tiles.append(jnp.einsum('nh,nij->hji', rel_bias, onehot,
                                precision=lax.Precision.HIGHEST))
    far = rel_bias[_t5_bucket(jnp.full((1, 1), MAX_DISTANCE, I32), nbk)]
    return jnp.stack(tiles, axis=0) * LOG2E, jnp.moveaxis(far, -1, 0) * LOG2E


def _block_diag_heads(w, rows_per_head, cols_per_head):
    n = w.shape[0]
    eye = jnp.eye(n, dtype=w.dtype)
    return jnp.einsum('hrc,hg->hrgc', w, eye).reshape(n * rows_per_head, n * cols_per_head)


def _tiles(s):
    big = s % 512 == 0
    return dict(tm_in=512 if big else 2 * Q_BLOCK, tm_out=512 if big else 2 * Q_BLOCK,
                kc=512 if big else 2 * Q_BLOCK, tb=4 * CHUNK)


def kernel(x, c, ada_w, ada_b, norm_g, w_in, shift_mu, w0, w2, a0, a2, k_k, k_a, r_k, lnx_g, lnx_b,
           kv_norm_g, w_uk, w_uv, w_pa, w_pb, w_o, rel_bias, final_g):
    bsz, s, d = x.shape
    depth = w_in.shape[0]
    rw = w0.shape[-1]
    c_lat, n_heads, dh = w_uk.shape[1:]
    aw = n_heads * dh
    iw = IDX_HEADS * IDX_HEAD_DIM
    n_shift = 3 * rw + DECAY_LORA + ICLR_LORA
    topk = min(TOPK_MAX, s // 4)
    assert DECAY_LORA + ICLR_LORA == LANES and dh == HEAD_DIM and IDX_HEAD_DIM == HEAD_DIM
    assert s % (2 * Q_BLOCK) == 0 and rw % LANES == 0 and n_heads % 2 == 0
    tiles = _tiles(s)

    sizes = (rw, rw, rw, DECAY_LORA, ICLR_LORA, rw, aw, c_lat, aw, iw, IDX_HEAD_DIM, IDX_HEADS, d, d)
    offs = [0]
    for sz in sizes:
        offs.append(offs[-1] + sz)
    col = lambda i: slice(offs[i], offs[i + 1])
    widths = (rw, aw, c_lat, aw, iw, LANES, LANES, d, d)

    mod = _mod_call(c, ada_w, ada_b)
    bias_near, bias_far = _bias_tiles(rel_bias)
    ones_bd = _block_diag_heads(jnp.ones((rw // HEAD_DIM, HEAD_DIM, HEAD_DIM), BF16), HEAD_DIM, HEAD_DIM)

    for l in range(depth):
        w = w_in[l]
        wi_pad = jnp.zeros((d, LANES - IDX_HEADS), w.dtype)
        w_packed = jnp.concatenate(
            [w[:, 0:n_shift], w[:, col(5)], w[:, col(6)], w[:, col(7)], w[:, col(8)], w[:, col(9)],
             w[:, col(10)], w[:, col(10)], w[:, col(11)], wi_pad, w[:, col(12)], w[:, col(13)]],
            axis=1).astype(BF16)
        sh, za, q, lat, zb, qi, ki2, wi, ga, gb = _inproj_call(
            x, mod[l], norm_g[l], w_packed, shift_mu[l], kv_norm_g[l],
            n_shift=n_shift, widths=widths, tm=tiles["tm_in"])

        vecs = jnp.stack([w0[l], a0[l], k_k[l], k_a[l], r_k[l].reshape(-1), lnx_g[l], lnx_b[l],
                          jnp.zeros((rw,), F32)], axis=0)
        w2p = jnp.concatenate([w2[l], jnp.zeros((ICLR_LORA, rw), F32)], axis=0)
        a2p = jnp.concatenate([jnp.zeros((DECAY_LORA, rw), F32), a2[l]], axis=0)
        ya = _rwkv_call(sh, vecs, w2p, a2p, ones_bd, width=rw, tb=tiles["tb"])

        wuk_t = jnp.transpose(w_uk[l], (1, 0, 2)).astype(BF16)
        wuv_bd = _block_diag_heads(jnp.transpose(w_uv[l], (1, 0, 2)), c_lat, dh).astype(BF16)
        yb = _dsa_call(q, qi, wi, ki2, lat, wuk_t, wuv_bd, bias_near, bias_far,
                       topk=topk, kc=tiles["kc"])

        x = _out_call(ya, za, yb, zb, ga, gb, x, mod[l], w_pa[l].astype(BF16), w_pb[l].astype(BF16),
                      w_o[l].astype(BF16), final_g, final=(l == depth - 1), tm=tiles["tm_out"])
    return x
```
